```python
import math
import jax
import jax.numpy as jnp
from jax import lax
import numpy as np

D_MODEL = 2048
BATCH = 4
SEQ = 2048
DEPTH = 2

N_MIXERS = 2
N_ATTN_LAYERS = (DEPTH + 1) // 2
N_SSM_LAYERS = DEPTH // 2

ATTN_HEAD_DIM = 128
ATTN_HEADS = D_MODEL // ATTN_HEAD_DIM
ROT_DIM = ATTN_HEAD_DIM // 4
ROPE_THETA = 500000.0
DILATION_PATTERNS = ((128, 1), (512, 4), (2048, 16))
ATTN_BLOCK = 128

SSM_EXPAND = 2
SSM_INNER = SSM_EXPAND * D_MODEL
SSM_HEAD_DIM = 64
SSM_HEADS = SSM_INNER // SSM_HEAD_DIM
SSM_GROUPS = 8
SSM_STATE = 128
SSM_CONV = 4
SSM_CHUNK = 128
SSM_CONV_DIM = SSM_INNER + 2 * SSM_GROUPS * SSM_STATE
SSM_IN_DIM = SSM_INNER + SSM_CONV_DIM + SSM_HEADS

N_EXPERTS = 64
N_EXPERT_GROUPS = 8
EXPERTS_PER_GROUP = N_EXPERTS // N_EXPERT_GROUPS
TOP_K = 2
D_EXPERT = D_MODEL // 4
MOE_BLOCK = 128

DEEPNORM_ALPHA = (2 * DEPTH) ** 0.25
DEEPNORM_BETA = (8 * DEPTH) ** -0.25
LN_EPS = 1e-5
RMS_EPS = 1e-5

kernel_name = 'hybrid_dilated_attn_mamba2_grouped_moe'


def _layer_norm(x, g, b):
    xf = x.astype(jnp.float32)
    mu = jnp.mean(xf, axis=-1, keepdims=True)
    var = jnp.mean(jnp.square(xf - mu), axis=-1, keepdims=True)
    y = (xf - mu) * lax.rsqrt(var + LN_EPS) * g.astype(jnp.float32) + b.astype(jnp.float32)
    return y.astype(x.dtype)


def _rope_tables(positions):
    inv_freq = ROPE_THETA ** (-jnp.arange(0, ROT_DIM, 2, dtype=jnp.float32) / ROT_DIM)
    ang = positions.astype(jnp.float32)[..., None] * inv_freq
    return jnp.cos(ang)[:, :, None, :], jnp.sin(ang)[:, :, None, :]


def _partial_rope(t, cos, sin):
    half = ROT_DIM // 2
    tr = t[..., :ROT_DIM].astype(jnp.float32)
    t1, t2 = tr[..., :half], tr[..., half:]
    rot = jnp.concatenate([t1 * cos - t2 * sin, t2 * cos + t1 * sin], axis=-1)
    return jnp.concatenate([rot.astype(t.dtype), t[..., ROT_DIM:]], axis=-1)


def _banded_causal_attention(q, k, v, wsub):
    lead = q.shape[:-2]
    L, hd = q.shape[-2], q.shape[-1]
    nb = -(-L // ATTN_BLOCK)
    pad = nb * ATTN_BLOCK - L
    nlead = len(lead)
    qb = jnp.pad(q, [(0, 0)] * nlead + [(0, pad), (0, 0)]).reshape(*lead, nb, ATTN_BLOCK, hd)
    kvcfg = [(0, 0)] * nlead + [(ATTN_BLOCK, pad), (0, 0)]
    kb = jnp.pad(k, kvcfg).reshape(*lead, nb + 1, ATTN_BLOCK, hd)
    vb = jnp.pad(v, kvcfg).reshape(*lead, nb + 1, ATTN_BLOCK, hd)
    kw = jnp.concatenate([kb[..., :-1, :, :], kb[..., 1:, :, :]], axis=-2)
    vw = jnp.concatenate([vb[..., :-1, :, :], vb[..., 1:, :, :]], axis=-2)
    s = jnp.einsum('...iqd,...ikd->...iqk', qb, kw).astype(jnp.float32)
    qi = jnp.arange(ATTN_BLOCK)[:, None]
    ki = jnp.arange(2 * ATTN_BLOCK)[None, :]
    dist = ATTN_BLOCK + qi - ki
    keypos = (jnp.arange(nb)[:, None, None] - 1) * ATTN_BLOCK + ki[None]
    mask = (dist >= 0)[None] & (dist <= wsub)[None] & (keypos >= 0)
    s = jnp.where(mask, s, -jnp.inf)
    lse = jax.nn.logsumexp(s, axis=-1)
    p = jnp.exp(s - lse[..., None])
    o = jnp.einsum('...iqk,...ikd->...iqd', p.astype(v.dtype), vw)
    o = o.reshape(*lead, nb * ATTN_BLOCK, hd)[..., :L, :]
    lse = lse.reshape(*lead, nb * ATTN_BLOCK)[..., :L]
    return o, lse


def _dilated_branch(q, k, v, window, dilation):
    B, S, H, hd = q.shape
    L = S // dilation

    def to_sub(t):
        return t.reshape(B, L, dilation, H, hd).transpose(0, 2, 3, 1, 4)

    o, lse = _banded_causal_attention(to_sub(q), to_sub(k), to_sub(v), window // dilation)
    o = o.transpose(0, 3, 1, 2, 4).reshape(B, S, H, hd)
    lse = lse.transpose(0, 3, 1, 2).reshape(B, S, H)
    return o, lse


def _dilated_attention_mixer(h, w_qkv, w_o, cos, sin):
    B, S, _ = h.shape
    qkv = (h @ w_qkv).reshape(B, S, 3, ATTN_HEADS, ATTN_HEAD_DIM)
    q = _partial_rope(qkv[:, :, 0], cos, sin) * (ATTN_HEAD_DIM ** -0.5)
    k = _partial_rope(qkv[:, :, 1], cos, sin)
    v = qkv[:, :, 2]
    outs, lses = [], []
    for window, dilation in DILATION_PATTERNS:
        o, lse = _dilated_branch(q, k, v, window, dilation)
        outs.append(o)
        lses.append(lse)
    wts = jax.nn.softmax(jnp.stack(lses, axis=0), axis=0)
    o = jnp.einsum('gbsh,gbshd->bshd', wts.astype(v.dtype), jnp.stack(outs, axis=0))
    return o.reshape(B, S, D_MODEL) @ w_o


def _ssd_chunked(x, dt, A, Bm, Cm):
    Bsz, S, H, P = x.shape
    G, N = Bm.shape[2], Bm.shape[3]
    hg = H // G
    c = S // SSM_CHUNK
    f32 = jnp.float32
    dtc = dt.reshape(Bsz, c, SSM_CHUNK, G, hg)
    a = dtc * A.reshape(G, hg)
    xdt = x.reshape(Bsz, c, SSM_CHUNK, G, hg, P).astype(f32) * dtc[..., None]
    Bc = Bm.reshape(Bsz, c, SSM_CHUNK, G, N).astype(f32)
    Cc = Cm.reshape(Bsz, c, SSM_CHUNK, G, N).astype(f32)
    a_cum = jnp.cumsum(a, axis=2)
    seg = a_cum[:, :, :, None] - a_cum[:, :, None, :]
    causal = (jnp.arange(SSM_CHUNK)[:, None] >= jnp.arange(SSM_CHUNK)[None, :])[None, None, :, :, None, None]
    Lmat = jnp.exp(jnp.where(causal, seg, -jnp.inf))
    CB = jnp.einsum('bclgn,bcsgn->bclsg', Cc, Bc)
    y_diag = jnp.einsum('bclsg,bclsgh,bcsghp->bclghp', CB, Lmat, xdt)
    decay_states = jnp.exp(a_cum[:, :, -1:] - a_cum)
    states = jnp.einsum('bclgn,bclgh,bclghp->bcghpn', Bc, decay_states, xdt)
    chunk_decay = jnp.exp(a_cum[:, :, -1])

    def step(hstate, inp):
        st, dec = inp
        return hstate * dec[..., None, None] + st, hstate

    init = jnp.zeros((Bsz, G, hg, P, N), f32)
    _, prev = lax.scan(step, init, (jnp.moveaxis(states, 1, 0), jnp.moveaxis(chunk_decay, 1, 0)))
    prev = jnp.moveaxis(prev, 0, 1)
    y_off = jnp.einsum('bclgn,bcghpn,bclgh->bclghp', Cc, prev, jnp.exp(a_cum))
    return (y_diag + y_off).reshape(Bsz, S, H, P)


def _mamba2_mixer(h, w_in, conv_w, conv_b, dt_bias, A_log, D_skip, norm_w, w_out):
    Bsz, S, _ = h.shape
    zxbcdt = h @ w_in
    z = zxbcdt[..., :SSM_INNER]
    xBC = zxbcdt[..., SSM_INNER:SSM_INNER + SSM_CONV_DIM]
    dt = zxbcdt[..., SSM_INNER + SSM_CONV_DIM:]
    xBC = lax.conv_general_dilated(xBC, conv_w[:, None, :], window_strides=(1,),
                                   padding=[(SSM_CONV - 1, 0)],
                                   dimension_numbers=('NWC', 'WIO', 'NWC'),
                                   feature_group_count=SSM_CONV_DIM) + conv_b
    xBC = jax.nn.silu(xBC)
    gn = SSM_GROUPS * SSM_STATE
    xs = xBC[..., :SSM_INNER].reshape(Bsz, S, SSM_HEADS, SSM_HEAD_DIM)
    Bm = xBC[..., SSM_INNER:SSM_INNER + gn].reshape(Bsz, S, SSM_GROUPS, SSM_STATE)
    Cm = xBC[..., SSM_INNER + gn:].reshape(Bsz, S, SSM_GROUPS, SSM_STATE)
    dt = jax.nn.softplus((dt + dt_bias).astype(jnp.float32))
    A = -jnp.exp(A_log.astype(jnp.float32))
    y = _ssd_chunked(xs, dt, A, Bm, Cm) + D_skip.astype(jnp.float32)[:, None] * xs.astype(jnp.float32)
    y = y.reshape(Bsz, S, SSM_INNER) * jax.nn.silu(z.astype(jnp.float32))
    yg = y.reshape(Bsz, S, SSM_GROUPS, SSM_INNER // SSM_GROUPS)
    yg = yg * lax.rsqrt(jnp.mean(jnp.square(yg), axis=-1, keepdims=True) + RMS_EPS)
    y = yg.reshape(Bsz, S, SSM_INNER) * norm_w.astype(jnp.float32)
    return y.astype(h.dtype) @ w_out


def _moe_ffn(h, router_w, router_bias, w_gate, w_up, w_down):
    Bsz, S, D = h.shape
    T = Bsz * S
    xt = h.reshape(T, D)
    scores = jax.nn.softmax((xt @ router_w).astype(jnp.float32), axis=-1)
    sel = (scores + router_bias.astype(jnp.float32)).reshape(T, N_EXPERT_GROUPS, EXPERTS_PER_GROUP)
    group_score = jnp.sum(lax.top_k(sel, TOP_K)[0], axis=-1)
    g_idx = jnp.argmax(group_score, axis=-1)
    in_group = sel[jnp.arange(T), g_idx]
    _, local_idx = lax.top_k(in_group, TOP_K)
    e_idx = g_idx[:, None] * EXPERTS_PER_GROUP + local_idx
    gate = jnp.take_along_axis(scores, e_idx, axis=1)
    gate = gate / jnp.sum(gate, axis=-1, keepdims=True)
    n_slots = T * TOP_K
    e_flat = e_idx.reshape(n_slots)
    order = jnp.argsort(e_flat, stable=True)
    sorted_e = e_flat[order]
    counts = jnp.zeros((N_EXPERTS,), jnp.int32).at[e_flat].add(1)
    offsets = jnp.cumsum(counts) - counts
    rank = jnp.arange(n_slots, dtype=jnp.int32) - offsets[sorted_e]
    padded = ((counts + MOE_BLOCK - 1) // MOE_BLOCK) * MOE_BLOCK
    pad_end = jnp.cumsum(padded)
    dest = (pad_end - padded)[sorted_e] + rank
    n_blocks = -(-n_slots // MOE_BLOCK) + N_EXPERTS
    rows = n_blocks * MOE_BLOCK
    row_tok = jnp.full((rows,), T, jnp.int32).at[dest].set((order // TOP_K).astype(jnp.int32))
    block_e = jnp.clip(jnp.searchsorted(pad_end, jnp.arange(n_blocks) * MOE_BLOCK, side='right'), 0, N_EXPERTS - 1)
    x_pad = jnp.concatenate([xt, jnp.zeros((1, D), xt.dtype)], axis=0)
    xb = x_pad[row_tok].reshape(n_blocks, MOE_BLOCK, D)

    def expert_block(args):
        xblk, e = args
        return (jax.nn.silu(xblk @ w_gate[e]) * (xblk @ w_up[e])) @ w_down[e]

    yb = lax.map(expert_block, (xb, block_e)).reshape(rows, D)
    slot_row = jnp.zeros((n_slots,), jnp.int32).at[order].set(dest)
    y = yb[slot_row].reshape(T, TOP_K, D)
    out = jnp.einsum('tk,tkd->td', gate.astype(y.dtype), y)
    return out.reshape(Bsz, S, D)


def setup_inputs(seed: int = 0) -> dict:
    key = jax.random.key(seed)
    ks = jax.random.split(key, 24)
    f32 = jnp.float32

    def nrm(k, shape, scale):
        return jax.random.normal(k, shape, f32) * scale

    x = nrm(ks[0], (BATCH, SEQ, D_MODEL), 1.0)
    positions = (jax.random.randint(ks[1], (BATCH, 1), 0, 1024) + jnp.arange(SEQ)[None, :]).astype(jnp.int32)
    attn_w_qkv = nrm(ks[2], (N_ATTN_LAYERS, D_MODEL, 3 * D_MODEL), D_MODEL ** -0.5)
    attn_w_qkv = attn_w_qkv.at[..., 2 * D_MODEL:].multiply(DEEPNORM_BETA)
    attn_w_o = nrm(ks[3], (N_ATTN_LAYERS, D_MODEL, D_MODEL), D_MODEL ** -0.5 * DEEPNORM_BETA)
    ssm_w_in = nrm(ks[4], (N_SSM_LAYERS, D_MODEL, SSM_IN_DIM), D_MODEL ** -0.5)
    ssm_conv_w = nrm(ks[5], (N_SSM_LAYERS, SSM_CONV, SSM_CONV_DIM), SSM_CONV ** -0.5)
    ssm_conv_b = nrm(ks[6], (N_SSM_LAYERS, SSM_CONV_DIM), 0.02)
    dt0 = jnp.exp(jax.random.uniform(ks[7], (N_SSM_LAYERS, SSM_HEADS), f32)
                  * (math.log(0.1) - math.log(0.001)) + math.log(0.001))
    ssm_dt_bias = dt0 + jnp.log(-jnp.expm1(-dt0))
    ssm_A_log = jnp.log(jax.random.uniform(ks[8], (N_SSM_LAYERS, SSM_HEADS), f32, minval=1.0, maxval=16.0))
    ssm_D = 1.0 + nrm(ks[9], (N_SSM_LAYERS, SSM_HEADS), 0.1)
    ssm_norm_w = 1.0 + nrm(ks[10], (N_SSM_LAYERS, SSM_INNER), 0.02)
    ssm_w_out = nrm(ks[11], (N_SSM_LAYERS, SSM_INNER, D_MODEL), SSM_INNER ** -0.5 * DEEPNORM_BETA)
    router_w = nrm(ks[12], (D_MODEL, N_EXPERTS), D_MODEL ** -0.5)
    router_bias = nrm(ks[13], (N_EXPERTS,), 0.01)
    moe_w_gate = nrm(ks[14], (DEPTH, N_EXPERTS, D_MODEL, D_EXPERT), D_MODEL ** -0.5)
    moe_w_up = nrm(ks[15], (DEPTH, N_EXPERTS, D_MODEL, D_EXPERT), D_MODEL ** -0.5)
    moe_w_down = nrm(ks[16], (DEPTH, N_EXPERTS, D_EXPERT, D_MODEL), D_EXPERT ** -0.5 * DEEPNORM_BETA)
    ln_mix_g = 1.0 + nrm(ks[17], (DEPTH, D_MODEL), 0.02)
    ln_mix_b = nrm(ks[18], (DEPTH, D_MODEL), 0.02)
    ln_ffn_g = 1.0 + nrm(ks[19], (DEPTH, D_MODEL), 0.02)
    ln_ffn_b = nrm(ks[20], (DEPTH, D_MODEL), 0.02)
    return {'x': x, 'positions': positions, 'attn_w_qkv': attn_w_qkv, 'attn_w_o': attn_w_o,
            'ssm_w_in': ssm_w_in, 'ssm_conv_w': ssm_conv_w, 'ssm_conv_b': ssm_conv_b,
            'ssm_dt_bias': ssm_dt_bias, 'ssm_A_log': ssm_A_log, 'ssm_D': ssm_D,
            'ssm_norm_w': ssm_norm_w, 'ssm_w_out': ssm_w_out, 'router_w': router_w,
            'router_bias': router_bias, 'moe_w_gate': moe_w_gate, 'moe_w_up': moe_w_up,
            'moe_w_down': moe_w_down, 'ln_mix_g': ln_mix_g, 'ln_mix_b': ln_mix_b,
            'ln_ffn_g': ln_ffn_g, 'ln_ffn_b': ln_ffn_b}


def reference(x, positions, attn_w_qkv, attn_w_o, ssm_w_in, ssm_conv_w, ssm_conv_b,
              ssm_dt_bias, ssm_A_log, ssm_D, ssm_norm_w, ssm_w_out, router_w, router_bias,
              moe_w_gate, moe_w_up, moe_w_down, ln_mix_g, ln_mix_b, ln_ffn_g, ln_ffn_b):
    cos, sin = _rope_tables(positions)
    h = x
    for i in range(DEPTH):
        j = i // N_MIXERS
        if i % N_MIXERS == 0:
            mix = _dilated_attention_mixer(h, attn_w_qkv[j], attn_w_o[j], cos, sin)
        else:
            mix = _mamba2_mixer(h, ssm_w_in[j], ssm_conv_w[j], ssm_conv_b[j], ssm_dt_bias[j],
                                ssm_A_log[j], ssm_D[j], ssm_norm_w[j], ssm_w_out[j])
        h = _layer_norm(DEEPNORM_ALPHA * h + mix, ln_mix_g[i], ln_mix_b[i])
        ffn = _moe_ffn(h, router_w, router_bias, moe_w_gate[i], moe_w_up[i], moe_w_down[i])
        h = _layer_norm(DEEPNORM_ALPHA * h + ffn, ln_ffn_g[i], ln_ffn_b[i])
    return h
```

```python
import functools
import math

import jax
import jax.numpy as jnp
from jax import lax
from jax.experimental import pallas as pl
from jax.experimental.pallas import tpu as pltpu

F32 = jnp.float32
BF16 = jnp.bfloat16
HIGHEST = lax.Precision.HIGHEST

LANES = 128
SUBLANES = 8
MIB = 1024 * 1024

HEAD_DIM = 128
ROT_DIM = HEAD_DIM // 4
ROPE_THETA = 500000.0
N_EXPERTS = 64
N_GROUPS = 8
EPG = N_EXPERTS // N_GROUPS
TOP_K = 2
MOE_BLOCK = 128
SSM_HEAD_DIM = 64
SSM_GROUPS = 8
SSM_STATE = 128
SSM_CONV = 4
SSM_CHUNK = 128
LN_EPS = 1e-5
RMS_EPS = 1e-5
NEG = -1e30


def _params(sem, vmem_mib):
    return pltpu.CompilerParams(dimension_semantics=sem, vmem_limit_bytes=vmem_mib * MIB)


def _silu(x):
    return x / (1.0 + jnp.exp(-x))


def _layer_norm_rows(y, g, b):
    mu = jnp.mean(y, axis=-1, keepdims=True)
    yc = y - mu
    var = jnp.mean(yc * yc, axis=-1, keepdims=True)
    return yc * lax.rsqrt(var + LN_EPS) * g + b


def _mm_kernel(a_ref, w_ref, o_ref):
    o_ref[...] = jnp.dot(a_ref[...], w_ref[...], preferred_element_type=F32).astype(o_ref.dtype)


def _mm(a, w, out_dtype, tm=1024, tn=512):
    m, k = a.shape
    n = w.shape[1]
    tn = min(tn, n)
    return pl.pallas_call(
        _mm_kernel,
        grid=(m // tm, n // tn),
        in_specs=[pl.BlockSpec((tm, k), lambda i, j: (i, 0)),
                  pl.BlockSpec((k, tn), lambda i, j: (0, j))],
        out_specs=pl.BlockSpec((tm, tn), lambda i, j: (i, j)),
        out_shape=jax.ShapeDtypeStruct((m, n), out_dtype),
        compiler_params=_params(("parallel", "arbitrary"), 48),
        name="mm",
    )(a, w)


def _qkv_rope_kernel(x_ref, w_ref, cos_ref, sa_ref, sb_ref, o_ref, *, n_rope_tiles, n_q_tiles, scale):
    n = pl.program_id(1)
    acc = jnp.dot(x_ref[...], w_ref[...], preferred_element_type=F32)
    tn = acc.shape[1]

    @pl.when(n < n_rope_tiles)
    def _():
        c = cos_ref[...]
        sa = sa_ref[...]
        sb = sb_ref[...]
        sc = jnp.where(n < n_q_tiles, scale, 1.0).astype(F32)
        for j in range(tn // LANES):
            t = acc[:, j * LANES:(j + 1) * LANES]
            r = t * c + pltpu.roll(t, LANES - ROT_DIM // 2, 1) * sa + pltpu.roll(t, ROT_DIM // 2, 1) * sb
            o_ref[:, j * LANES:(j + 1) * LANES] = (r * sc).astype(o_ref.dtype)

    @pl.when(n >= n_rope_tiles)
    def _():
        o_ref[...] = acc.astype(o_ref.dtype)


def _qkv_rope(xb, w, cos_t, sa_t, sb_t, d_model, tm=1024, tn=512):
    m, k = xb.shape
    n = w.shape[1]
    kern = functools.partial(_qkv_rope_kernel, n_rope_tiles=2 * d_model // tn, n_q_tiles=d_model // tn,
                             scale=HEAD_DIM ** -0.5)
    tab = pl.BlockSpec((tm, LANES), lambda i, j: (i, 0))
    return pl.pallas_call(
        kern,
        grid=(m // tm, n // tn),
        in_specs=[pl.BlockSpec((tm, k), lambda i, j: (i, 0)),
                  pl.BlockSpec((k, tn), lambda i, j: (0, j)),
                  tab, tab, tab],
        out_specs=pl.BlockSpec((tm, tn), lambda i, j: (i, j)),
        out_shape=jax.ShapeDtypeStruct((m, n), BF16),
        compiler_params=_params(("parallel", "arbitrary"), 48),
        name="qkv_rope",
    )(xb, w, cos_t, sa_t, sb_t)


def _attn_kernel(q_ref, k_ref, v_ref, bias_ref, o_ref, m_sc, l_sc, acc_sc):
    qi = pl.program_id(2)
    ki = pl.program_id(3)

    @pl.when(ki == 0)
    def _():
        m_sc[...] = jnp.full_like(m_sc, 0.1 * NEG)
        l_sc[...] = jnp.zeros_like(l_sc)
        acc_sc[...] = jnp.zeros_like(acc_sc)

    @pl.when(ki <= qi)
    def _():
        s = lax.dot_general(q_ref[0], k_ref[0], (((1,), (1,)), ((), ())), preferred_element_type=F32)
        s = s + bias_ref[qi - ki]
        m_prev = m_sc[...]
        m_new = jnp.maximum(m_prev, jnp.max(s, axis=1, keepdims=True))
        alpha = jnp.exp(m_prev - m_new)
        p = jnp.exp(s - m_new)
        l_sc[...] = alpha * l_sc[...] + jnp.sum(p, axis=1, keepdims=True)
        acc_sc[...] = alpha * acc_sc[...] + jnp.dot(p.astype(BF16), v_ref[0], preferred_element_type=F32)
        m_sc[...] = m_new

    @pl.when(ki == qi)
    def _():
        o_ref[0] = (acc_sc[...] / l_sc[...]).astype(o_ref.dtype)


def _attention(qkv3, bias, n_heads, blk):
    b, s, _ = qkv3.shape
    nq = s // blk
    return pl.pallas_call(
        _attn_kernel,
        grid=(b, n_heads, nq, nq),
        in_specs=[pl.BlockSpec((1, blk, HEAD_DIM), lambda bi, h, qi, ki: (bi, qi, h)),
                  pl.BlockSpec((1, blk, HEAD_DIM), lambda bi, h, qi, ki: (bi, jnp.minimum(ki, qi), n_heads + h)),
                  pl.BlockSpec((1, blk, HEAD_DIM), lambda bi, h, qi, ki: (bi, jnp.minimum(ki, qi), 2 * n_heads + h)),
                  pl.BlockSpec((nq, blk, blk), lambda bi, h, qi, ki: (0, 0, 0))],
        out_specs=pl.BlockSpec((1, blk, HEAD_DIM), lambda bi, h, qi, ki: (bi, qi, h)),
        out_shape=jax.ShapeDtypeStruct((b, s, n_heads * HEAD_DIM), BF16),
        scratch_shapes=[pltpu.VMEM((blk, 1), F32), pltpu.VMEM((blk, 1), F32), pltpu.VMEM((blk, HEAD_DIM), F32)],
        compiler_params=_params(("parallel", "parallel", "parallel", "arbitrary"), 40),
        name="attention",
    )(qkv3, qkv3, qkv3, bias)


def _branch_count_bias(s, blk, patterns):
    nq = s // blk
    off = jnp.arange(nq, dtype=jnp.int32)[:, None, None] * blk
    d = off + jnp.arange(blk, dtype=jnp.int32)[None, :, None] - jnp.arange(blk, dtype=jnp.int32)[None, None, :]
    cnt = jnp.zeros(d.shape, F32)
    for window, dilation in patterns:
        cnt = cnt + ((d >= 0) & (d % dilation == 0) & (d <= window)).astype(F32)
    return jnp.where(cnt > 0, jnp.log(jnp.maximum(cnt, 1.0)), NEG)


def _mm_res_ln_kernel(a_ref, w_ref, h_ref, g_ref, b_ref, of_ref, ob_ref, acc_sc, *, nk, alpha):
    k = pl.program_id(1)

    @pl.when(k == 0)
    def _():
        acc_sc[...] = jnp.zeros_like(acc_sc)

    acc_sc[...] += jnp.dot(a_ref[...], w_ref[...], preferred_element_type=F32)

    @pl.when(k == nk - 1)
    def _():
        out = _layer_norm_rows(alpha * h_ref[...] + acc_sc[...], g_ref[...], b_ref[...])
        of_ref[...] = out
        ob_ref[...] = out.astype(BF16)


def _mm_res_ln(a, w, h, g, b, alpha, tm=512, tk=1024):
    m, kdim = a.shape
    n = w.shape[1]
    nk = kdim // tk
    kern = functools.partial(_mm_res_ln_kernel, nk=nk, alpha=alpha)
    row = pl.BlockSpec((tm, n), lambda i, k: (i, 0))
    vec = pl.BlockSpec((1, n), lambda i, k: (0, 0))
    return pl.pallas_call(
        kern,
        grid=(m // tm, nk),
        in_specs=[pl.BlockSpec((tm, tk), lambda i, k: (i, k)),
                  pl.BlockSpec((tk, n), lambda i, k: (k, 0)),
                  row, vec, vec],
        out_specs=[row, row],
        out_shape=[jax.ShapeDtypeStruct((m, n), F32), jax.ShapeDtypeStruct((m, n), BF16)],
        scratch_shapes=[pltpu.VMEM((tm, n), F32)],
        compiler_params=_params(("parallel", "arbitrary"), 48),
        name="mm_res_ln",
    )(a, w, h, g.reshape(1, n), b.reshape(1, n))


def _router_kernel(h_ref, wt_ref, b_ref, e_ref, g_ref):
    logits = lax.dot_general(wt_ref[...], h_ref[...], (((1,), (1,)), ((), ())),
                             precision=HIGHEST, preferred_element_type=F32)
    mx = jnp.max(logits, axis=0, keepdims=True)
    ex = jnp.exp(logits - mx)
    scores = ex / jnp.sum(ex, axis=0, keepdims=True)
    sel = scores + b_ref[...]
    tm = sel.shape[1]
    li = lax.broadcasted_iota(jnp.int32, (EPG, tm), 0)

    def first_argmax(v):
        m = jnp.max(v, axis=0, keepdims=True)
        return m, jnp.min(jnp.where(v == m, li, EPG), axis=0, keepdims=True)

    gs, i1s, i2s = [], [], []
    for g in range(N_GROUPS):
        slab = sel[g * EPG:(g + 1) * EPG, :]
        m1, i1 = first_argmax(slab)
        m2, i2 = first_argmax(jnp.where(li == i1, NEG, slab))
        gs.append(m1 + m2)
        i1s.append(i1)
        i2s.append(i2)
    best, gidx = gs[0], jnp.zeros((1, tm), jnp.int32)
    for g in range(1, N_GROUPS):
        better = gs[g] > best
        best = jnp.where(better, gs[g], best)
        gidx = jnp.where(better, g, gidx)
    l1 = jnp.zeros((1, tm), jnp.int32)
    l2 = jnp.zeros((1, tm), jnp.int32)
    s1 = jnp.zeros((1, tm), F32)
    s2 = jnp.zeros((1, tm), F32)
    for g in range(N_GROUPS):
        here = gidx == g
        slab = scores[g * EPG:(g + 1) * EPG, :]
        l1 = jnp.where(here, i1s[g], l1)
        l2 = jnp.where(here, i2s[g], l2)
        s1 = jnp.where(here, jnp.sum(jnp.where(li == i1s[g], slab, 0.0), axis=0, keepdims=True), s1)
        s2 = jnp.where(here, jnp.sum(jnp.where(li == i2s[g], slab, 0.0), axis=0, keepdims=True), s2)
    tot = s1 + s2
    e_ref[0:1, :] = gidx * EPG + l1
    e_ref[1:2, :] = gidx * EPG + l2
    g_ref[0:1, :] = s1 / tot
    g_ref[1:2, :] = s2 / tot


def _router(h, router_wt, bias_col, tm=512):
    t, d = h.shape
    return pl.pallas_call(
        _router_kernel,
        grid=(t // tm,),
        in_specs=[pl.BlockSpec((tm, d), lambda i: (i, 0)),
                  pl.BlockSpec((N_EXPERTS, d), lambda i: (0, 0)),
                  pl.BlockSpec((N_EXPERTS, 1), lambda i: (0, 0))],
        out_specs=[pl.BlockSpec((TOP_K, tm), lambda i: (0, i)), pl.BlockSpec((TOP_K, tm), lambda i: (0, i))],
        out_shape=[jax.ShapeDtypeStruct((TOP_K, t), jnp.int32), jax.ShapeDtypeStruct((TOP_K, t), F32)],
        compiler_params=_params(("parallel",), 32),
        name="router",
    )(h, router_wt, bias_col)


def _row_copy_all(n_rows, make_copy):
    def start(r, carry):
        for k in range(TOP_K):
            make_copy(r, k).start()
        return carry

    def wait(r, carry):
        for k in range(TOP_K):
            make_copy(r, k).wait()
        return carry

    lax.fori_loop(0, n_rows, start, 0)
    lax.fori_loop(0, n_rows, wait, 0)


def _dispatch_kernel(dest_ref, h_ref, zeros_hbm, xs_hbm, sem, *, tm):
    del zeros_hbm
    base = pl.program_id(0) * tm * TOP_K

    def make_copy(r, k):
        d = dest_ref[base + TOP_K * r + k]
        return pltpu.make_async_copy(h_ref.at[pl.ds(r, 1), :], xs_hbm.at[pl.ds(d, 1), :], sem)

    _row_copy_all(tm, make_copy)


def _dispatch(dest, h, rows, tm=128):
    t, d = h.shape
    return pl.pallas_call(
        functools.partial(_dispatch_kernel, tm=tm),
        grid_spec=pltpu.PrefetchScalarGridSpec(
            num_scalar_prefetch=1,
            grid=(t // tm,),
            in_specs=[pl.BlockSpec((tm, d), lambda i, dest: (i, 0)),
                      pl.BlockSpec(memory_space=pl.ANY)],
            out_specs=pl.BlockSpec(memory_space=pl.ANY),
            scratch_shapes=[pltpu.SemaphoreType.DMA(())],
        ),
        out_shape=jax.ShapeDtypeStruct((rows, d), F32),
        input_output_aliases={2: 0},
        compiler_params=_params(("arbitrary",), 32),
        name="dispatch",
    )(dest, h, jnp.zeros((rows, d), F32))


def _ffn_kernel(be_ref, nv_ref, x_ref, wg_ref, wu_ref, wd_ref, o_ref, wgb, wub, wdb):
    i = pl.program_id(0)
    nv = nv_ref[i]

    @pl.when(nv > 0)
    def _():
        changed = jnp.logical_or(i == 0, be_ref[i] != be_ref[jnp.maximum(i - 1, 0)])

        @pl.when(changed)
        def _():
            wgb[...] = wg_ref[0].astype(BF16)
            wub[...] = wu_ref[0].astype(BF16)
            wdb[...] = wd_ref[0].astype(BF16)

        x = x_ref[...].astype(BF16)
        hg = jnp.dot(x, wgb[...], preferred_element_type=F32)
        hu = jnp.dot(x, wub[...], preferred_element_type=F32)
        act = (_silu(hg) * hu).astype(BF16)
        o_ref[...] = jnp.dot(act, wdb[...], preferred_element_type=F32)

    @pl.when(nv == 0)
    def _():
        o_ref[...] = jnp.zeros_like(o_ref)


def _ffn(block_e, n_valid, xs, w_gate, w_up, w_down):
    rows, d = xs.shape
    de = w_gate.shape[2]
    nb = rows // MOE_BLOCK
    return pl.pallas_call(
        _ffn_kernel,
        grid_spec=pltpu.PrefetchScalarGridSpec(
            num_scalar_prefetch=2,
            grid=(nb,),
            in_specs=[pl.BlockSpec((MOE_BLOCK, d), lambda i, be, nv: (i, 0)),
                      pl.BlockSpec((1, d, de), lambda i, be, nv: (be[i], 0, 0)),
                      pl.BlockSpec((1, d, de), lambda i, be, nv: (be[i], 0, 0)),
                      pl.BlockSpec((1, de, d), lambda i, be, nv: (be[i], 0, 0))],
            out_specs=pl.BlockSpec((MOE_BLOCK, d), lambda i, be, nv: (i, 0)),
            scratch_shapes=[pltpu.VMEM((d, de), BF16), pltpu.VMEM((d, de), BF16), pltpu.VMEM((de, d), BF16)],
        ),
        out_shape=jax.ShapeDtypeStruct((rows, d), F32),
        compiler_params=_params(("arbitrary",), 56),
        name="ffn",
    )(block_e, n_valid, xs, w_gate, w_up, w_down)


def _combine_ln_kernel(dest_ref, yb_hbm, gate_ref, h_ref, g_ref, b_ref, of_ref, ob_ref, ybuf, sem, *, tm, alpha):
    base = pl.program_id(0) * tm * TOP_K

    def make_copy(r, k):
        d = dest_ref[base + TOP_K * r + k]
        return pltpu.make_async_copy(yb_hbm.at[pl.ds(d, 1), :], ybuf.at[k, pl.ds(r, 1), :], sem)

    _row_copy_all(tm, make_copy)
    gt = gate_ref[...]
    moe = gt[:, 0:1] * ybuf[0] + gt[:, 1:2] * ybuf[1]
    out = _layer_norm_rows(alpha * h_ref[...] + moe, g_ref[...], b_ref[...])
    of_ref[...] = out
    ob_ref[...] = out.astype(BF16)


def _combine_ln(dest, yb, gate_t, h, g, b, alpha, tm=128):
    t, d = h.shape
    row = pl.BlockSpec((tm, d), lambda i, dest: (i, 0))
    vec = pl.BlockSpec((1, d), lambda i, dest: (0, 0))
    return pl.pallas_call(
        functools.partial(_combine_ln_kernel, tm=tm, alpha=alpha),
        grid_spec=pltpu.PrefetchScalarGridSpec(
            num_scalar_prefetch=1,
            grid=(t // tm,),
            in_specs=[pl.BlockSpec(memory_space=pl.ANY),
                      pl.BlockSpec((tm, TOP_K), lambda i, dest: (i, 0)),
                      row, vec, vec],
            out_specs=[row, row],
            scratch_shapes=[pltpu.VMEM((TOP_K, tm, d), F32), pltpu.SemaphoreType.DMA(())],
        ),
        out_shape=[jax.ShapeDtypeStruct((t, d), F32), jax.ShapeDtypeStruct((t, d), BF16)],
        compiler_params=_params(("arbitrary",), 32),
        name="combine_ln",
    )(dest, yb, gate_t, h, g.reshape(1, d), b.reshape(1, d))


def _moe_layer(h, router_wt, bias_col, w_gate, w_up, w_down, ln_g, ln_b, alpha):
    t, d = h.shape
    e_idx, gate = _router(h, router_wt, bias_col)
    n_slots = t * TOP_K
    e_flat = e_idx.T.reshape(n_slots)
    onehot = (e_flat[:, None] == jnp.arange(N_EXPERTS, dtype=jnp.int32)[None, :]).astype(jnp.int32)
    csum = jnp.cumsum(onehot, axis=0)
    counts = csum[-1]
    padded = ((counts + MOE_BLOCK - 1) // MOE_BLOCK) * MOE_BLOCK
    pad_end = jnp.cumsum(padded)
    pad_start = pad_end - padded
    dest = jnp.sum(onehot * (csum - 1 + pad_start[None, :]), axis=1).astype(jnp.int32)
    n_blocks = n_slots // MOE_BLOCK + N_EXPERTS
    blk_row0 = jnp.arange(n_blocks, dtype=jnp.int32) * MOE_BLOCK
    block_e = jnp.clip(jnp.searchsorted(pad_end, blk_row0, side='right'), 0, N_EXPERTS - 1).astype(jnp.int32)
    n_valid = jnp.where(blk_row0 < pad_end[-1],
                        jnp.clip(counts[block_e] - (blk_row0 - pad_start[block_e]), 0, MOE_BLOCK), 0).astype(jnp.int32)
    xs = _dispatch(dest, h, n_blocks * MOE_BLOCK)
    yb = _ffn(block_e, n_valid, xs, w_gate, w_up, w_down)
    return _combine_ln(dest, yb, gate.T, h, ln_g, ln_b, alpha)


def _expand_heads(x, e_bf16):
    hi = x.astype(BF16)
    r1 = x - hi.astype(F32)
    mid = r1.astype(BF16)
    lo = (r1 - mid.astype(F32)).astype(BF16)
    dot = lambda a: jnp.dot(a, e_bf16, preferred_element_type=F32)
    return dot(hi) + dot(mid) + dot(lo)


def _ssd_kernel(z_ref, xbc_ref, prev_ref, dt_ref, cw_ref, cb_ref, dtb_ref, alog_ref, dexp_ref, nw_ref, e_ref,
                o_ref, state_sc, xs_sc, *, d_inner):
    c = pl.program_id(1)
    L = SSM_CHUNK
    gw = d_inner // SSM_GROUPS
    hpg = gw // SSM_HEAD_DIM
    pad = SUBLANES

    @pl.when(c == 0)
    def _():
        state_sc[...] = jnp.zeros_like(state_sc)

    xs_sc[0:pad, :] = jnp.where(c > 0, prev_ref[0], 0.0)
    xs_sc[pad:pad + L, :] = xbc_ref[0]

    def conv_silu(col0, width):
        cols = slice(col0, col0 + width)
        acc = cb_ref[:, cols] + xs_sc[pad:pad + L, cols] * cw_ref[SSM_CONV - 1:SSM_CONV, cols]
        for j in range(1, SSM_CONV):
            acc = acc + xs_sc[pad - j:pad - j + L, cols] * cw_ref[SSM_CONV - 1 - j:SSM_CONV - j, cols]
        return _silu(acc)

    dtv = dt_ref[0] + dtb_ref[...]
    dt = jnp.maximum(dtv, 0.0) + jnp.log1p(jnp.exp(-jnp.abs(dtv)))
    a = dt * (-jnp.exp(alog_ref[...]))
    ri = lax.broadcasted_iota(jnp.int32, (L, L), 0)
    ci = lax.broadcasted_iota(jnp.int32, (L, L), 1)
    causal = ri >= ci
    a_cum = jnp.dot(causal.astype(F32), a, precision=HIGHEST, preferred_element_type=F32)
    a_cum_t = a_cum.T
    e_mat = e_ref[...]
    acum_x = _expand_heads(a_cum, e_mat)
    dt_x = _expand_heads(dt, e_mat)
    lane = lax.broadcasted_iota(jnp.int32, (L, LANES), 1)
    first_head = lane < SSM_HEAD_DIM

    for g in range(SSM_GROUPS):
        cols = slice(g * gw, (g + 1) * gw)
        xg = conv_silu(g * gw, gw)
        bg = conv_silu(d_inner + g * SSM_STATE, SSM_STATE)
        cg = conv_silu(d_inner + (SSM_GROUPS + g) * SSM_STATE, SSM_STATE)
        ac = acum_x[:, cols]
        last = ac[L - 1:L, :]
        xdt = xg * dt_x[:, cols]
        cb16 = cg.astype(BF16)
        bb16 = bg.astype(BF16)
        cbm = lax.dot_general(cb16, bb16, (((1,), (1,)), ((), ())), preferred_element_type=F32)
        st = state_sc[g]
        y_off = jnp.dot(cb16, st.astype(BF16), preferred_element_type=F32) * jnp.exp(ac)
        xd = (xdt * jnp.exp(last - ac)).astype(BF16)
        state_sc[g] = st * jnp.exp(last) + jnp.dot(bg.T.astype(BF16), xd, preferred_element_type=F32)
        xdt16 = xdt.astype(BF16)
        pieces = []
        for jp in range(hpg // 2):
            lhs = []
            for hh in (g * hpg + 2 * jp, g * hpg + 2 * jp + 1):
                seg = jnp.where(causal, a_cum[:, hh:hh + 1] - a_cum_t[hh:hh + 1, :], NEG)
                lhs.append((cbm * jnp.exp(seg)).astype(BF16))
            xp = xdt16[:, jp * LANES:(jp + 1) * LANES]
            zero = jnp.zeros_like(xp)
            rhs = jnp.concatenate([jnp.where(first_head, xp, zero), jnp.where(first_head, zero, xp)], axis=0)
            pieces.append(jnp.dot(jnp.concatenate(lhs, axis=1), rhs, preferred_element_type=F32))
        y = jnp.concatenate(pieces, axis=1) + y_off + dexp_ref[:, cols] * xg
        y = y * _silu(z_ref[0, :, cols])
        y = y * lax.rsqrt(jnp.mean(y * y, axis=1, keepdims=True) + RMS_EPS) * nw_ref[:, cols]
        o_ref[0, :, cols] = y.astype(o_ref.dtype)


def _ssd(z3, xbc3, dt3, conv_w, conv_b, dt_bias_p, a_log_p, d_exp, norm_w, e_mat):
    b, s, d_inner = z3.shape
    cdim = xbc3.shape[2]
    nc = s // SSM_CHUNK
    gw = d_inner // SSM_GROUPS
    rows_per_prev = SSM_CHUNK // SUBLANES
    full = lambda shape: pl.BlockSpec(shape, lambda bi, c: tuple(0 for _ in shape))
    return pl.pallas_call(
        functools.partial(_ssd_kernel, d_inner=d_inner),
        grid=(b, nc),
        in_specs=[pl.BlockSpec((1, SSM_CHUNK, d_inner), lambda bi, c: (bi, c, 0)),
                  pl.BlockSpec((1, SSM_CHUNK, cdim), lambda bi, c: (bi, c, 0)),
                  pl.BlockSpec((1, SUBLANES, cdim), lambda bi, c: (bi, jnp.maximum(c * rows_per_prev - 1, 0), 0)),
                  pl.BlockSpec((1, SSM_CHUNK, LANES), lambda bi, c: (bi, c, 0)),
                  full((SSM_CONV, cdim)), full((1, cdim)), full((1, LANES)), full((1, LANES)),
                  full((1, d_inner)), full((1, d_inner)), full((LANES, d_inner))],
        out_specs=pl.BlockSpec((1, SSM_CHUNK, d_inner), lambda bi, c: (bi, c, 0)),
        out_shape=jax.ShapeDtypeStruct((b, s, d_inner), BF16),
        scratch_shapes=[pltpu.VMEM((SSM_GROUPS, SSM_STATE, gw), F32),
                        pltpu.VMEM((SUBLANES + SSM_CHUNK, cdim), F32)],
        compiler_params=_params(("parallel", "arbitrary"), 56),
        name="ssd",
    )(z3, xbc3, xbc3, dt3, conv_w, conv_b, dt_bias_p, a_log_p, d_exp, norm_w, e_mat)


def kernel(x, positions, attn_w_qkv, attn_w_o, ssm_w_in, ssm_conv_w, ssm_conv_b, ssm_dt_bias, ssm_A_log, ssm_D, ssm_norm_w, ssm_w_out, router_w, router_bias, moe_w_gate, moe_w_up, moe_w_down, ln_mix_g, ln_mix_b, ln_ffn_g, ln_ffn_b):
    bsz, seq, d_model = x.shape
    t = bsz * seq
    depth = moe_w_gate.shape[0]
    alpha = (2 * depth) ** 0.25
    n_heads = d_model // HEAD_DIM
    patterns = ((128, 1), (512, 4), (2048, 16))
    attn_blk = 512

    half = ROT_DIM // 2
    inv_freq = ROPE_THETA ** (-jnp.arange(0, ROT_DIM, 2, dtype=F32) / ROT_DIM)
    ang = positions.astype(F32).reshape(t, 1) * inv_freq
    cos, sin = jnp.cos(ang), jnp.sin(ang)
    zeros_r = jnp.zeros((t, LANES - ROT_DIM), F32)
    cos_t = jnp.concatenate([cos, cos, jnp.ones((t, LANES - ROT_DIM), F32)], axis=1)
    sa_t = jnp.concatenate([-sin, jnp.zeros((t, half), F32), zeros_r], axis=1)
    sb_t = jnp.concatenate([jnp.zeros((t, half), F32), sin, zeros_r], axis=1)
    bias = _branch_count_bias(seq, attn_blk, patterns)

    router_wt = router_w.T
    bias_col = router_bias.reshape(N_EXPERTS, 1).astype(F32)

    h = x.reshape(t, d_model)
    hb = h.astype(BF16)
    for i in range(depth):
        j = i // 2
        if i % 2 == 0:
            qkv = _qkv_rope(hb, attn_w_qkv[j].astype(BF16), cos_t, sa_t, sb_t, d_model)
            att = _attention(qkv.reshape(bsz, seq, 3 * d_model), bias, n_heads, attn_blk)
            h, hb = _mm_res_ln(att.reshape(t, d_model), attn_w_o[j].astype(BF16), h, ln_mix_g[i], ln_mix_b[i], alpha)
        else:
            d_inner = ssm_norm_w.shape[1]
            n_ssm_heads = ssm_dt_bias.shape[1]
            cdim = ssm_conv_w.shape[2]
            w_in = ssm_w_in[j]
            z = _mm(hb, w_in[:, :d_inner].astype(BF16), F32)
            xbc = _mm(hb, w_in[:, d_inner:d_inner + cdim].astype(BF16), F32)
            w_dt = jnp.pad(w_in[:, d_inner + cdim:], ((0, 0), (0, LANES - n_ssm_heads))).astype(BF16)
            dt = _mm(hb, w_dt, F32)
            hpad = (0, LANES - n_ssm_heads)
            head_of_ch = jnp.arange(d_inner, dtype=jnp.int32) // SSM_HEAD_DIM
            e_mat = (jnp.arange(LANES, dtype=jnp.int32)[:, None] == head_of_ch[None, :]).astype(BF16)
            y = _ssd(z.reshape(bsz, seq, d_inner), xbc.reshape(bsz, seq, cdim), dt.reshape(bsz, seq, LANES),
                     ssm_conv_w[j], ssm_conv_b[j].reshape(1, cdim),
                     jnp.pad(ssm_dt_bias[j], hpad).reshape(1, LANES), jnp.pad(ssm_A_log[j], hpad).reshape(1, LANES),
                     jnp.repeat(ssm_D[j], SSM_HEAD_DIM).reshape(1, d_inner), ssm_norm_w[j].reshape(1, d_inner), e_mat)
            h, hb = _mm_res_ln(y.reshape(t, d_inner), ssm_w_out[j].astype(BF16), h, ln_mix_g[i], ln_mix_b[i], alpha)
        h, hb = _moe_layer(h, router_wt, bias_col, moe_w_gate[i], moe_w_up[i], moe_w_down[i],
                           ln_ffn_g[i], ln_ffn_b[i], alpha)
    return h.reshape(bsz, seq, d_model)
```

```python
import functools
import math

import jax
import jax.numpy as jnp
from jax import lax
from jax.experimental import pallas as pl
from jax.experimental.pallas import tpu as pltpu

F32 = jnp.float32
BF16 = jnp.bfloat16
HIGHEST = lax.Precision.HIGHEST

LANES = 128
SUBLANES = 8
MIB = 1024 * 1024

HEAD_DIM = 128
ROT_DIM = HEAD_DIM // 4
ROPE_THETA = 500000.0
N_EXPERTS = 64
N_GROUPS = 8
EPG = N_EXPERTS // N_GROUPS
TOP_K = 2
MOE_BLOCK = 128
SSM_HEAD_DIM = 64
SSM_GROUPS = 8
SSM_STATE = 128
SSM_CONV = 4
SSM_CHUNK = 128
LN_EPS = 1e-5
RMS_EPS = 1e-5
NEG = -1e30


def _params(sem, vmem_mib):
    return pltpu.CompilerParams(dimension_semantics=sem, vmem_limit_bytes=vmem_mib * MIB)


def _silu(x):
    return x / (1.0 + jnp.exp(-x))


def _layer_norm_rows(y, g, b):
    mu = jnp.mean(y, axis=-1, keepdims=True)
    yc = y - mu
    var = jnp.mean(yc * yc, axis=-1, keepdims=True)
    return yc * lax.rsqrt(var + LN_EPS) * g + b


def _mm_kernel(a_ref, w_ref, o_ref):
    o_ref[...] = jnp.dot(a_ref[...], w_ref[...], preferred_element_type=F32).astype(o_ref.dtype)


def _mm(a, w, out_dtype, tm=1024, tn=512):
    m, k = a.shape
    n = w.shape[1]
    tn = min(tn, n)
    return pl.pallas_call(
        _mm_kernel,
        grid=(m // tm, n // tn),
        in_specs=[pl.BlockSpec((tm, k), lambda i, j: (i, 0)),
                  pl.BlockSpec((k, tn), lambda i, j: (0, j))],
        out_specs=pl.BlockSpec((tm, tn), lambda i, j: (i, j)),
        out_shape=jax.ShapeDtypeStruct((m, n), out_dtype),
        compiler_params=_params(("parallel", "arbitrary"), 48),
        name="mm",
    )(a, w)


def _qkv_rope_kernel(x_ref, w_ref, cos_ref, sa_ref, sb_ref, o_ref, *, n_rope_tiles, n_q_tiles, scale):
    n = pl.program_id(1)
    acc = jnp.dot(x_ref[...], w_ref[...], preferred_element_type=F32)
    tn = acc.shape[1]

    @pl.when(n < n_rope_tiles)
    def _():
        c = cos_ref[...]
        sa = sa_ref[...]
        sb = sb_ref[...]
        sc = jnp.where(n < n_q_tiles, scale, 1.0).astype(F32)
        for j in range(tn // LANES):
            t = acc[:, j * LANES:(j + 1) * LANES]
            r = t * c + pltpu.roll(t, LANES - ROT_DIM // 2, 1) * sa + pltpu.roll(t, ROT_DIM // 2, 1) * sb
            o_ref[:, j * LANES:(j + 1) * LANES] = (r * sc).astype(o_ref.dtype)

    @pl.when(n >= n_rope_tiles)
    def _():
        o_ref[...] = acc.astype(o_ref.dtype)


def _qkv_rope(xb, w, cos_t, sa_t, sb_t, d_model, tm=1024, tn=512):
    m, k = xb.shape
    n = w.shape[1]
    kern = functools.partial(_qkv_rope_kernel, n_rope_tiles=2 * d_model // tn, n_q_tiles=d_model // tn,
                             scale=HEAD_DIM ** -0.5)
    tab = pl.BlockSpec((tm, LANES), lambda i, j: (i, 0))
    return pl.pallas_call(
        kern,
        grid=(m // tm, n // tn),
        in_specs=[pl.BlockSpec((tm, k), lambda i, j: (i, 0)),
                  pl.BlockSpec((k, tn), lambda i, j: (0, j)),
                  tab, tab, tab],
        out_specs=pl.BlockSpec((tm, tn), lambda i, j: (i, j)),
        out_shape=jax.ShapeDtypeStruct((m, n), BF16),
        compiler_params=_params(("parallel", "arbitrary"), 48),
        name="qkv_rope",
    )(xb, w, cos_t, sa_t, sb_t)


def _attn_kernel(qi_ref, ki_ref, q_ref, k_ref, v_ref, bias_ref, o_ref, m_sc, l_sc, acc_sc, *, heads):
    p = pl.program_id(2)
    qi = qi_ref[p]
    ki = ki_ref[p]
    blk = q_ref.shape[1]

    @pl.when(ki == 0)
    def _():
        m_sc[...] = jnp.full_like(m_sc, 0.1 * NEG)
        l_sc[...] = jnp.zeros_like(l_sc)
        acc_sc[...] = jnp.zeros_like(acc_sc)

    bias = bias_ref[qi - ki]
    for j in range(heads):
        cols = slice(j * HEAD_DIM, (j + 1) * HEAD_DIM)
        s = lax.dot_general(q_ref[0, :, cols], k_ref[0, :, cols], (((1,), (1,)), ((), ())),
                            preferred_element_type=F32) + bias
        m_prev = m_sc[j]
        m_new = jnp.maximum(m_prev, jnp.max(s, axis=1, keepdims=True))
        alpha = jnp.exp(m_prev - m_new)
        pr = jnp.exp(s - jnp.concatenate([m_new] * (blk // LANES), axis=1))
        l_sc[j] = alpha * l_sc[j] + jnp.sum(pr, axis=1, keepdims=True)
        acc_sc[j] = alpha * acc_sc[j] + jnp.dot(pr.astype(BF16), v_ref[0, :, cols], preferred_element_type=F32)
        m_sc[j] = m_new

    @pl.when(ki == qi)
    def _():
        for j in range(heads):
            o_ref[0, :, j * HEAD_DIM:(j + 1) * HEAD_DIM] = (acc_sc[j] / l_sc[j]).astype(o_ref.dtype)


def _attention(qkv3, bias, n_heads, blk, heads=4):
    b, s, _ = qkv3.shape
    nq = s // blk
    pairs = [(qi, ki) for qi in range(nq) for ki in range(qi + 1)]
    qi_tab = jnp.asarray([pq for pq, _ in pairs], jnp.int32)
    ki_tab = jnp.asarray([pk for _, pk in pairs], jnp.int32)
    hg = n_heads // heads
    w = heads * HEAD_DIM
    return pl.pallas_call(
        functools.partial(_attn_kernel, heads=heads),
        grid_spec=pltpu.PrefetchScalarGridSpec(
            num_scalar_prefetch=2,
            grid=(b, hg, len(pairs)),
            in_specs=[pl.BlockSpec((1, blk, w), lambda bi, h, p, qt, kt: (bi, qt[p], h)),
                      pl.BlockSpec((1, blk, w), lambda bi, h, p, qt, kt: (bi, kt[p], hg + h)),
                      pl.BlockSpec((1, blk, w), lambda bi, h, p, qt, kt: (bi, kt[p], 2 * hg + h)),
                      pl.BlockSpec((nq, blk, blk), lambda bi, h, p, qt, kt: (0, 0, 0))],
            out_specs=pl.BlockSpec((1, blk, w), lambda bi, h, p, qt, kt: (bi, qt[p], h)),
            scratch_shapes=[pltpu.VMEM((heads, blk, LANES), F32), pltpu.VMEM((heads, blk, LANES), F32),
                            pltpu.VMEM((heads, blk, HEAD_DIM), F32)],
        ),
        out_shape=jax.ShapeDtypeStruct((b, s, n_heads * HEAD_DIM), BF16),
        compiler_params=_params(("parallel", "parallel", "arbitrary"), 48),
        name="attention",
    )(qi_tab, ki_tab, qkv3, qkv3, qkv3, bias)


def _branch_count_bias(s, blk, patterns):
    nq = s // blk
    off = jnp.arange(nq, dtype=jnp.int32)[:, None, None] * blk
    d = off + jnp.arange(blk, dtype=jnp.int32)[None, :, None] - jnp.arange(blk, dtype=jnp.int32)[None, None, :]
    cnt = jnp.zeros(d.shape, F32)
    for window, dilation in patterns:
        cnt = cnt + ((d >= 0) & (d % dilation == 0) & (d <= window)).astype(F32)
    return jnp.where(cnt > 0, jnp.log(jnp.maximum(cnt, 1.0)), NEG)


def _mm_res_ln_kernel(a_ref, w_ref, h_ref, g_ref, b_ref, of_ref, ob_ref, acc_sc, *, nk, alpha):
    k = pl.program_id(1)

    @pl.when(k == 0)
    def _():
        acc_sc[...] = jnp.zeros_like(acc_sc)

    acc_sc[...] += jnp.dot(a_ref[...], w_ref[...], preferred_element_type=F32)

    @pl.when(k == nk - 1)
    def _():
        out = _layer_norm_rows(alpha * h_ref[...] + acc_sc[...], g_ref[...], b_ref[...])
        of_ref[...] = out
        ob_ref[...] = out.astype(BF16)


def _mm_res_ln(a, w, h, g, b, alpha, tm=512, tk=1024):
    m, kdim = a.shape
    n = w.shape[1]
    nk = kdim // tk
    kern = functools.partial(_mm_res_ln_kernel, nk=nk, alpha=alpha)
    row = pl.BlockSpec((tm, n), lambda i, k: (i, 0))
    vec = pl.BlockSpec((1, n), lambda i, k: (0, 0))
    return pl.pallas_call(
        kern,
        grid=(m // tm, nk),
        in_specs=[pl.BlockSpec((tm, tk), lambda i, k: (i, k)),
                  pl.BlockSpec((tk, n), lambda i, k: (k, 0)),
                  row, vec, vec],
        out_specs=[row, row],
        out_shape=[jax.ShapeDtypeStruct((m, n), F32), jax.ShapeDtypeStruct((m, n), BF16)],
        scratch_shapes=[pltpu.VMEM((tm, n), F32)],
        compiler_params=_params(("parallel", "arbitrary"), 48),
        name="mm_res_ln",
    )(a, w, h, g.reshape(1, n), b.reshape(1, n))


def _router_kernel(h_ref, wt_ref, b_ref, e_ref, g_ref):
    logits = lax.dot_general(wt_ref[...], h_ref[...], (((1,), (1,)), ((), ())),
                             precision=HIGHEST, preferred_element_type=F32)
    mx = jnp.max(logits, axis=0, keepdims=True)
    ex = jnp.exp(logits - mx)
    scores = ex / jnp.sum(ex, axis=0, keepdims=True)
    sel = scores + b_ref[...]
    tm = sel.shape[1]
    li = lax.broadcasted_iota(jnp.int32, (EPG, tm), 0)

    def first_argmax(v):
        m = jnp.max(v, axis=0, keepdims=True)
        return m, jnp.min(jnp.where(v == m, li, EPG), axis=0, keepdims=True)

    gs, i1s, i2s = [], [], []
    for g in range(N_GROUPS):
        slab = sel[g * EPG:(g + 1) * EPG, :]
        m1, i1 = first_argmax(slab)
        m2, i2 = first_argmax(jnp.where(li == i1, NEG, slab))
        gs.append(m1 + m2)
        i1s.append(i1)
        i2s.append(i2)
    best, gidx = gs[0], jnp.zeros((1, tm), jnp.int32)
    for g in range(1, N_GROUPS):
        better = gs[g] > best
        best = jnp.where(better, gs[g], best)
        gidx = jnp.where(better, g, gidx)
    l1 = jnp.zeros((1, tm), jnp.int32)
    l2 = jnp.zeros((1, tm), jnp.int32)
    s1 = jnp.zeros((1, tm), F32)
    s2 = jnp.zeros((1, tm), F32)
    for g in range(N_GROUPS):
        here = gidx == g
        slab = scores[g * EPG:(g + 1) * EPG, :]
        l1 = jnp.where(here, i1s[g], l1)
        l2 = jnp.where(here, i2s[g], l2)
        s1 = jnp.where(here, jnp.sum(jnp.where(li == i1s[g], slab, 0.0), axis=0, keepdims=True), s1)
        s2 = jnp.where(here, jnp.sum(jnp.where(li == i2s[g], slab, 0.0), axis=0, keepdims=True), s2)
    tot = s1 + s2
    e_ref[0:1, :] = gidx * EPG + l1
    e_ref[1:2, :] = gidx * EPG + l2
    g_ref[0:1, :] = s1 / tot
    g_ref[1:2, :] = s2 / tot


def _router(h, router_wt, bias_col, tm=512):
    t, d = h.shape
    return pl.pallas_call(
        _router_kernel,
        grid=(t // tm,),
        in_specs=[pl.BlockSpec((tm, d), lambda i: (i, 0)),
                  pl.BlockSpec((N_EXPERTS, d), lambda i: (0, 0)),
                  pl.BlockSpec((N_EXPERTS, 1), lambda i: (0, 0))],
        out_specs=[pl.BlockSpec((TOP_K, tm), lambda i: (0, i)), pl.BlockSpec((TOP_K, tm), lambda i: (0, i))],
        out_shape=[jax.ShapeDtypeStruct((TOP_K, t), jnp.int32), jax.ShapeDtypeStruct((TOP_K, t), F32)],
        compiler_params=_params(("parallel",), 32),
        name="router",
    )(h, router_wt, bias_col)


def _row_copy_all(n_rows, make_copy):
    def start(r, carry):
        for k in range(TOP_K):
            make_copy(r, k).start()
        return carry

    def wait(r, carry):
        for k in range(TOP_K):
            make_copy(r, k).wait()
        return carry

    lax.fori_loop(0, n_rows, start, 0)
    lax.fori_loop(0, n_rows, wait, 0)


def _dispatch_kernel(dest_ref, h_ref, zeros_hbm, xs_hbm, sem, *, tm):
    del zeros_hbm
    base = pl.program_id(0) * tm * TOP_K

    def make_copy(r, k):
        d = dest_ref[base + TOP_K * r + k]
        return pltpu.make_async_copy(h_ref.at[pl.ds(r, 1), :], xs_hbm.at[pl.ds(d, 1), :], sem)

    _row_copy_all(tm, make_copy)


def _dispatch(dest, h, rows, tm=128):
    t, d = h.shape
    return pl.pallas_call(
        functools.partial(_dispatch_kernel, tm=tm),
        grid_spec=pltpu.PrefetchScalarGridSpec(
            num_scalar_prefetch=1,
            grid=(t // tm,),
            in_specs=[pl.BlockSpec((tm, d), lambda i, dest: (i, 0)),
                      pl.BlockSpec(memory_space=pl.ANY)],
            out_specs=pl.BlockSpec(memory_space=pl.ANY),
            scratch_shapes=[pltpu.SemaphoreType.DMA(())],
        ),
        out_shape=jax.ShapeDtypeStruct((rows, d), F32),
        input_output_aliases={2: 0},
        compiler_params=_params(("arbitrary",), 32),
        name="dispatch",
    )(dest, h, jnp.zeros((rows, d), F32))


def _ffn_kernel(be_ref, nv_ref, x_ref, wg_ref, wu_ref, wd_ref, o_ref, wgb, wub, wdb):
    i = pl.program_id(0)
    nv = nv_ref[i]

    @pl.when(nv > 0)
    def _():
        changed = jnp.logical_or(i == 0, be_ref[i] != be_ref[jnp.maximum(i - 1, 0)])

        @pl.when(changed)
        def _():
            wgb[...] = wg_ref[0, 0].astype(BF16)
            wub[...] = wu_ref[0, 0].astype(BF16)
            wdb[...] = wd_ref[0, 0].astype(BF16)

        x = x_ref[...].astype(BF16)
        hg = jnp.dot(x, wgb[...], preferred_element_type=F32)
        hu = jnp.dot(x, wub[...], preferred_element_type=F32)
        act = (_silu(hg) * hu).astype(BF16)
        o_ref[...] = jnp.dot(act, wdb[...], preferred_element_type=F32)

    @pl.when(nv == 0)
    def _():
        o_ref[...] = jnp.zeros_like(o_ref)


def _ffn(block_e, n_valid, xs, w_gate, w_up, w_down, layer):
    rows, d = xs.shape
    de = w_gate.shape[3]
    nb = rows // MOE_BLOCK
    return pl.pallas_call(
        _ffn_kernel,
        grid_spec=pltpu.PrefetchScalarGridSpec(
            num_scalar_prefetch=2,
            grid=(nb,),
            in_specs=[pl.BlockSpec((MOE_BLOCK, d), lambda i, be, nv: (i, 0)),
                      pl.BlockSpec((1, 1, d, de), lambda i, be, nv: (layer, be[i], 0, 0)),
                      pl.BlockSpec((1, 1, d, de), lambda i, be, nv: (layer, be[i], 0, 0)),
                      pl.BlockSpec((1, 1, de, d), lambda i, be, nv: (layer, be[i], 0, 0))],
            out_specs=pl.BlockSpec((MOE_BLOCK, d), lambda i, be, nv: (i, 0)),
            scratch_shapes=[pltpu.VMEM((d, de), BF16), pltpu.VMEM((d, de), BF16), pltpu.VMEM((de, d), BF16)],
        ),
        out_shape=jax.ShapeDtypeStruct((rows, d), F32),
        compiler_params=_params(("arbitrary",), 56),
        name="ffn",
    )(block_e, n_valid, xs, w_gate, w_up, w_down)


def _combine_ln_kernel(dest_ref, yb_hbm, gate_ref, h_ref, g_ref, b_ref, of_ref, ob_ref, ybuf, sem, *, tm, alpha):
    base = pl.program_id(0) * tm * TOP_K

    def make_copy(r, k):
        d = dest_ref[base + TOP_K * r + k]
        return pltpu.make_async_copy(yb_hbm.at[pl.ds(d, 1), :], ybuf.at[k, pl.ds(r, 1), :], sem)

    _row_copy_all(tm, make_copy)
    gt = gate_ref[...]
    moe = gt[:, 0:1] * ybuf[0] + gt[:, 1:2] * ybuf[1]
    out = _layer_norm_rows(alpha * h_ref[...] + moe, g_ref[...], b_ref[...])
    of_ref[...] = out
    ob_ref[...] = out.astype(BF16)


def _combine_ln(dest, yb, gate_t, h, g, b, alpha, tm=128):
    t, d = h.shape
    row = pl.BlockSpec((tm, d), lambda i, dest: (i, 0))
    vec = pl.BlockSpec((1, d), lambda i, dest: (0, 0))
    return pl.pallas_call(
        functools.partial(_combine_ln_kernel, tm=tm, alpha=alpha),
        grid_spec=pltpu.PrefetchScalarGridSpec(
            num_scalar_prefetch=1,
            grid=(t // tm,),
            in_specs=[pl.BlockSpec(memory_space=pl.ANY),
                      pl.BlockSpec((tm, TOP_K), lambda i, dest: (i, 0)),
                      row, vec, vec],
            out_specs=[row, row],
            scratch_shapes=[pltpu.VMEM((TOP_K, tm, d), F32), pltpu.SemaphoreType.DMA(())],
        ),
        out_shape=[jax.ShapeDtypeStruct((t, d), F32), jax.ShapeDtypeStruct((t, d), BF16)],
        compiler_params=_params(("arbitrary",), 32),
        name="combine_ln",
    )(dest, yb, gate_t, h, g.reshape(1, d), b.reshape(1, d))


def _moe_layer(h, router_wt, bias_col, w_gate, w_up, w_down, layer, ln_g, ln_b, alpha):
    t, d = h.shape
    e_idx, gate = _router(h, router_wt, bias_col)
    n_slots = t * TOP_K
    e_flat = e_idx.T.reshape(n_slots)
    onehot = (e_flat[:, None] == jnp.arange(N_EXPERTS, dtype=jnp.int32)[None, :]).astype(jnp.int32)
    csum = jnp.cumsum(onehot, axis=0)
    counts = csum[-1]
    padded = ((counts + MOE_BLOCK - 1) // MOE_BLOCK) * MOE_BLOCK
    pad_end = jnp.cumsum(padded)
    pad_start = pad_end - padded
    dest = jnp.sum(onehot * (csum - 1 + pad_start[None, :]), axis=1).astype(jnp.int32)
    n_blocks = n_slots // MOE_BLOCK + N_EXPERTS
    blk_row0 = jnp.arange(n_blocks, dtype=jnp.int32) * MOE_BLOCK
    block_e = jnp.clip(jnp.searchsorted(pad_end, blk_row0, side='right'), 0, N_EXPERTS - 1).astype(jnp.int32)
    n_valid = jnp.where(blk_row0 < pad_end[-1],
                        jnp.clip(counts[block_e] - (blk_row0 - pad_start[block_e]), 0, MOE_BLOCK), 0).astype(jnp.int32)
    xs = _dispatch(dest, h, n_blocks * MOE_BLOCK)
    yb = _ffn(block_e, n_valid, xs, w_gate, w_up, w_down, layer)
    return _combine_ln(dest, yb, gate.T, h, ln_g, ln_b, alpha)


def _expand_heads(x, e_bf16):
    hi = x.astype(BF16)
    r1 = x - hi.astype(F32)
    mid = r1.astype(BF16)
    lo = (r1 - mid.astype(F32)).astype(BF16)
    dot = lambda a: jnp.dot(a, e_bf16, preferred_element_type=F32)
    return dot(hi) + dot(mid) + dot(lo)


def _ssd_kernel(z_ref, xbc_ref, prev_ref, dt_ref, cw_ref, cb_ref, dtb_ref, alog_ref, dexp_ref, nw_ref, e_ref,
                o_ref, state_sc, xs_sc, *, d_inner):
    c = pl.program_id(1)
    L = SSM_CHUNK
    gw = d_inner // SSM_GROUPS
    hpg = gw // SSM_HEAD_DIM
    pad = SUBLANES

    @pl.when(c == 0)
    def _():
        state_sc[...] = jnp.zeros_like(state_sc)

    xs_sc[0:pad, :] = jnp.where(c > 0, prev_ref[0], 0.0)
    xs_sc[pad:pad + L, :] = xbc_ref[0]

    def conv_silu(col0, width):
        cols = slice(col0, col0 + width)
        acc = cb_ref[:, cols] + xs_sc[pad:pad + L, cols] * cw_ref[SSM_CONV - 1:SSM_CONV, cols]
        for j in range(1, SSM_CONV):
            acc = acc + xs_sc[pad - j:pad - j + L, cols] * cw_ref[SSM_CONV - 1 - j:SSM_CONV - j, cols]
        return _silu(acc)

    dtv = dt_ref[0] + dtb_ref[...]
    dt = jnp.maximum(dtv, 0.0) + jnp.log1p(jnp.exp(-jnp.abs(dtv)))
    a = dt * (-jnp.exp(alog_ref[...]))
    ri = lax.broadcasted_iota(jnp.int32, (L, L), 0)
    ci = lax.broadcasted_iota(jnp.int32, (L, L), 1)
    causal = ri >= ci
    a_cum = jnp.dot(causal.astype(F32), a, precision=HIGHEST, preferred_element_type=F32)
    a_cum_t = a_cum.T
    e_mat = e_ref[...]
    acum_x = _expand_heads(a_cum, e_mat)
    dt_x = _expand_heads(dt, e_mat)
    lane = lax.broadcasted_iota(jnp.int32, (L, LANES), 1)
    first_head = lane < SSM_HEAD_DIM

    for g in range(SSM_GROUPS):
        cols = slice(g * gw, (g + 1) * gw)
        xg = conv_silu(g * gw, gw)
        bg = conv_silu(d_inner + g * SSM_STATE, SSM_STATE)
        cg = conv_silu(d_inner + (SSM_GROUPS + g) * SSM_STATE, SSM_STATE)
        ac = acum_x[:, cols]
        last = ac[L - 1:L, :]
        xdt = xg * dt_x[:, cols]
        cb16 = cg.astype(BF16)
        bb16 = bg.astype(BF16)
        cbm = lax.dot_general(cb16, bb16, (((1,), (1,)), ((), ())), preferred_element_type=F32)
        st = state_sc[g]
        y_off = jnp.dot(cb16, st.astype(BF16), preferred_element_type=F32) * jnp.exp(ac)
        xd = (xdt * jnp.exp(last - ac)).astype(BF16)
        state_sc[g] = st * jnp.exp(last) + jnp.dot(bg.T.astype(BF16), xd, preferred_element_type=F32)
        xdt16 = xdt.astype(BF16)
        pieces = []
        for jp in range(hpg // 2):
            lhs = []
            for hh in (g * hpg + 2 * jp, g * hpg + 2 * jp + 1):
                seg = jnp.where(causal, a_cum[:, hh:hh + 1] - a_cum_t[hh:hh + 1, :], NEG)
                lhs.append((cbm * jnp.exp(seg)).astype(BF16))
            xp = xdt16[:, jp * LANES:(jp + 1) * LANES]
            zero = jnp.zeros_like(xp)
            rhs = jnp.concatenate([jnp.where(first_head, xp, zero), jnp.where(first_head, zero, xp)], axis=0)
            pieces.append(jnp.dot(jnp.concatenate(lhs, axis=1), rhs, preferred_element_type=F32))
        y = jnp.concatenate(pieces, axis=1) + y_off + dexp_ref[:, cols] * xg
        y = y * _silu(z_ref[0, :, cols])
        y = y * lax.rsqrt(jnp.mean(y * y, axis=1, keepdims=True) + RMS_EPS) * nw_ref[:, cols]
        o_ref[0, :, cols] = y.astype(o_ref.dtype)


def _ssd(z3, xbc3, dt3, conv_w, conv_b, dt_bias_p, a_log_p, d_exp, norm_w, e_mat):
    b, s, d_inner = z3.shape
    cdim = xbc3.shape[2]
    nc = s // SSM_CHUNK
    gw = d_inner // SSM_GROUPS
    rows_per_prev = SSM_CHUNK // SUBLANES
    full = lambda shape: pl.BlockSpec(shape, lambda bi, c: tuple(0 for _ in shape))
    return pl.pallas_call(
        functools.partial(_ssd_kernel, d_inner=d_inner),
        grid=(b, nc),
        in_specs=[pl.BlockSpec((1, SSM_CHUNK, d_inner), lambda bi, c: (bi, c, 0)),
                  pl.BlockSpec((1, SSM_CHUNK, cdim), lambda bi, c: (bi, c, 0)),
                  pl.BlockSpec((1, SUBLANES, cdim), lambda bi, c: (bi, jnp.maximum(c * rows_per_prev - 1, 0), 0)),
                  pl.BlockSpec((1, SSM_CHUNK, LANES), lambda bi, c: (bi, c, 0)),
                  full((SSM_CONV, cdim)), full((1, cdim)), full((1, LANES)), full((1, LANES)),
                  full((1, d_inner)), full((1, d_inner)), full((LANES, d_inner))],
        out_specs=pl.BlockSpec((1, SSM_CHUNK, d_inner), lambda bi, c: (bi, c, 0)),
        out_shape=jax.ShapeDtypeStruct((b, s, d_inner), BF16),
        scratch_shapes=[pltpu.VMEM((SSM_GROUPS, SSM_STATE, gw), F32),
                        pltpu.VMEM((SUBLANES + SSM_CHUNK, cdim), F32)],
        compiler_params=_params(("parallel", "arbitrary"), 56),
        name="ssd",
    )(z3, xbc3, xbc3, dt3, conv_w, conv_b, dt_bias_p, a_log_p, d_exp, norm_w, e_mat)


def kernel(x, positions, attn_w_qkv, attn_w_o, ssm_w_in, ssm_conv_w, ssm_conv_b, ssm_dt_bias, ssm_A_log, ssm_D, ssm_norm_w, ssm_w_out, router_w, router_bias, moe_w_gate, moe_w_up, moe_w_down, ln_mix_g, ln_mix_b, ln_ffn_g, ln_ffn_b):
    bsz, seq, d_model = x.shape
    t = bsz * seq
    depth = moe_w_gate.shape[0]
    alpha = (2 * depth) ** 0.25
    n_heads = d_model // HEAD_DIM
    patterns = ((128, 1), (512, 4), (2048, 16))
    attn_blk = 512

    half = ROT_DIM // 2
    inv_freq = ROPE_THETA ** (-jnp.arange(0, ROT_DIM, 2, dtype=F32) / ROT_DIM)
    ang = positions.astype(F32).reshape(t, 1) * inv_freq
    cos, sin = jnp.cos(ang), jnp.sin(ang)
    zeros_r = jnp.zeros((t, LANES - ROT_DIM), F32)
    cos_t = jnp.concatenate([cos, cos, jnp.ones((t, LANES - ROT_DIM), F32)], axis=1)
    sa_t = jnp.concatenate([-sin, jnp.zeros((t, half), F32), zeros_r], axis=1)
    sb_t = jnp.concatenate([jnp.zeros((t, half), F32), sin, zeros_r], axis=1)
    bias = _branch_count_bias(seq, attn_blk, patterns)

    router_wt = router_w.T
    bias_col = router_bias.reshape(N_EXPERTS, 1).astype(F32)

    h = x.reshape(t, d_model)
    hb = h.astype(BF16)
    for i in range(depth):
        j = i // 2
        if i % 2 == 0:
            qkv = _qkv_rope(hb, attn_w_qkv[j].astype(BF16), cos_t, sa_t, sb_t, d_model)
            att = _attention(qkv.reshape(bsz, seq, 3 * d_model), bias, n_heads, attn_blk)
            h, hb = _mm_res_ln(att.reshape(t, d_model), attn_w_o[j].astype(BF16), h, ln_mix_g[i], ln_mix_b[i], alpha)
        else:
            d_inner = ssm_norm_w.shape[1]
            n_ssm_heads = ssm_dt_bias.shape[1]
            cdim = ssm_conv_w.shape[2]
            w_in = ssm_w_in[j]
            z = _mm(hb, w_in[:, :d_inner].astype(BF16), F32)
            xbc = _mm(hb, w_in[:, d_inner:d_inner + cdim].astype(BF16), F32)
            w_dt = jnp.pad(w_in[:, d_inner + cdim:], ((0, 0), (0, LANES - n_ssm_heads))).astype(BF16)
            dt = _mm(hb, w_dt, F32)
            hpad = (0, LANES - n_ssm_heads)
            head_of_ch = jnp.arange(d_inner, dtype=jnp.int32) // SSM_HEAD_DIM
            e_mat = (jnp.arange(LANES, dtype=jnp.int32)[:, None] == head_of_ch[None, :]).astype(BF16)
            y = _ssd(z.reshape(bsz, seq, d_inner), xbc.reshape(bsz, seq, cdim), dt.reshape(bsz, seq, LANES),
                     ssm_conv_w[j], ssm_conv_b[j].reshape(1, cdim),
                     jnp.pad(ssm_dt_bias[j], hpad).reshape(1, LANES), jnp.pad(ssm_A_log[j], hpad).reshape(1, LANES),
                     jnp.repeat(ssm_D[j], SSM_HEAD_DIM).reshape(1, d_inner), ssm_norm_w[j].reshape(1, d_inner), e_mat)
            h, hb = _mm_res_ln(y.reshape(t, d_inner), ssm_w_out[j].astype(BF16), h, ln_mix_g[i], ln_mix_b[i], alpha)
        h, hb = _moe_layer(h, router_wt, bias_col, moe_w_gate, moe_w_up, moe_w_down, i,
                           ln_ffn_g[i], ln_ffn_b[i], alpha)
    return h.reshape(bsz, seq, d_model)
```

```python
import functools
import math

import jax
import jax.numpy as jnp
from jax import lax
from jax.experimental import pallas as pl
from jax.experimental.pallas import tpu as pltpu

F32 = jnp.float32
BF16 = jnp.bfloat16
HIGHEST = lax.Precision.HIGHEST

LANES = 128
SUBLANES = 8
MIB = 1024 * 1024

HEAD_DIM = 128
ROT_DIM = HEAD_DIM // 4
ROPE_THETA = 500000.0
N_EXPERTS = 64
N_GROUPS = 8
EPG = N_EXPERTS // N_GROUPS
TOP_K = 2
MOE_BLOCK = 128
SSM_HEAD_DIM = 64
SSM_GROUPS = 8
SSM_STATE = 128
SSM_CONV = 4
SSM_CHUNK = 128
LN_EPS = 1e-5
RMS_EPS = 1e-5
NEG = -1e30


def _params(sem, vmem_mib):
    return pltpu.CompilerParams(dimension_semantics=sem, vmem_limit_bytes=vmem_mib * MIB)


def _silu(x):
    return x / (1.0 + jnp.exp(-x))


def _layer_norm_rows(y, g, b):
    mu = jnp.mean(y, axis=-1, keepdims=True)
    yc = y - mu
    var = jnp.mean(yc * yc, axis=-1, keepdims=True)
    return yc * lax.rsqrt(var + LN_EPS) * g + b


def _mm_kernel(a_ref, w_ref, o_ref):
    o_ref[...] = jnp.dot(a_ref[...], w_ref[...], preferred_element_type=F32).astype(o_ref.dtype)


def _mm(a, w, out_dtype, tm=1024, tn=512):
    m, k = a.shape
    n = w.shape[1]
    tn = min(tn, n)
    return pl.pallas_call(
        _mm_kernel,
        grid=(m // tm, n // tn),
        in_specs=[pl.BlockSpec((tm, k), lambda i, j: (i, 0)),
                  pl.BlockSpec((k, tn), lambda i, j: (0, j))],
        out_specs=pl.BlockSpec((tm, tn), lambda i, j: (i, j)),
        out_shape=jax.ShapeDtypeStruct((m, n), out_dtype),
        compiler_params=_params(("parallel", "arbitrary"), 48),
        name="mm",
    )(a, w)


def _qkv_rope_kernel(x_ref, w_ref, cos_ref, sa_ref, sb_ref, o_ref, *, n_rope_tiles, n_q_tiles, scale):
    n = pl.program_id(1)
    acc = jnp.dot(x_ref[...], w_ref[...], preferred_element_type=F32)
    tn = acc.shape[1]

    @pl.when(n < n_rope_tiles)
    def _():
        c = cos_ref[...]
        sa = sa_ref[...]
        sb = sb_ref[...]
        sc = jnp.where(n < n_q_tiles, scale, 1.0).astype(F32)
        for j in range(tn // LANES):
            t = acc[:, j * LANES:(j + 1) * LANES]
            r = t * c + pltpu.roll(t, LANES - ROT_DIM // 2, 1) * sa + pltpu.roll(t, ROT_DIM // 2, 1) * sb
            o_ref[:, j * LANES:(j + 1) * LANES] = (r * sc).astype(o_ref.dtype)

    @pl.when(n >= n_rope_tiles)
    def _():
        o_ref[...] = acc.astype(o_ref.dtype)


def _qkv_rope(xb, w, cos_t, sa_t, sb_t, d_model, tm=1024, tn=512):
    m, k = xb.shape
    n = w.shape[1]
    kern = functools.partial(_qkv_rope_kernel, n_rope_tiles=2 * d_model // tn, n_q_tiles=d_model // tn,
                             scale=HEAD_DIM ** -0.5)
    tab = pl.BlockSpec((tm, LANES), lambda i, j: (i, 0))
    return pl.pallas_call(
        kern,
        grid=(m // tm, n // tn),
        in_specs=[pl.BlockSpec((tm, k), lambda i, j: (i, 0)),
                  pl.BlockSpec((k, tn), lambda i, j: (0, j)),
                  tab, tab, tab],
        out_specs=pl.BlockSpec((tm, tn), lambda i, j: (i, j)),
        out_shape=jax.ShapeDtypeStruct((m, n), BF16),
        compiler_params=_params(("parallel", "arbitrary"), 48),
        name="qkv_rope",
    )(xb, w, cos_t, sa_t, sb_t)


def _attn_kernel(qi_ref, ki_ref, q_ref, k_ref, v_ref, bias_ref, o_ref, m_sc, l_sc, acc_sc, *, heads):
    p = pl.program_id(2)
    qi = qi_ref[p]
    ki = ki_ref[p]
    blk = q_ref.shape[1]

    @pl.when(ki == 0)
    def _():
        m_sc[...] = jnp.full_like(m_sc, 0.1 * NEG)
        l_sc[...] = jnp.zeros_like(l_sc)
        acc_sc[...] = jnp.zeros_like(acc_sc)

    bias = bias_ref[qi - ki]
    for j in range(heads):
        cols = slice(j * HEAD_DIM, (j + 1) * HEAD_DIM)
        s = lax.dot_general(q_ref[0, :, cols], k_ref[0, :, cols], (((1,), (1,)), ((), ())),
                            preferred_element_type=F32) + bias
        m_prev = m_sc[j]
        m_new = jnp.maximum(m_prev, jnp.max(s, axis=1, keepdims=True))
        alpha = jnp.exp(m_prev - m_new)
        pr = jnp.exp(s - jnp.concatenate([m_new] * (blk // LANES), axis=1))
        l_sc[j] = alpha * l_sc[j] + jnp.sum(pr, axis=1, keepdims=True)
        acc_sc[j] = alpha * acc_sc[j] + jnp.dot(pr.astype(BF16), v_ref[0, :, cols], preferred_element_type=F32)
        m_sc[j] = m_new

    @pl.when(ki == qi)
    def _():
        for j in range(heads):
            o_ref[0, :, j * HEAD_DIM:(j + 1) * HEAD_DIM] = (acc_sc[j] / l_sc[j]).astype(o_ref.dtype)


def _attention(qkv3, bias, n_heads, blk, heads=4):
    b, s, _ = qkv3.shape
    nq = s // blk
    pairs = [(qi, ki) for qi in range(nq) for ki in range(qi + 1)]
    qi_tab = jnp.asarray([pq for pq, _ in pairs], jnp.int32)
    ki_tab = jnp.asarray([pk for _, pk in pairs], jnp.int32)
    hg = n_heads // heads
    w = heads * HEAD_DIM
    return pl.pallas_call(
        functools.partial(_attn_kernel, heads=heads),
        grid_spec=pltpu.PrefetchScalarGridSpec(
            num_scalar_prefetch=2,
            grid=(b, hg, len(pairs)),
            in_specs=[pl.BlockSpec((1, blk, w), lambda bi, h, p, qt, kt: (bi, qt[p], h)),
                      pl.BlockSpec((1, blk, w), lambda bi, h, p, qt, kt: (bi, kt[p], hg + h)),
                      pl.BlockSpec((1, blk, w), lambda bi, h, p, qt, kt: (bi, kt[p], 2 * hg + h)),
                      pl.BlockSpec((nq, blk, blk), lambda bi, h, p, qt, kt: (0, 0, 0))],
            out_specs=pl.BlockSpec((1, blk, w), lambda bi, h, p, qt, kt: (bi, qt[p], h)),
            scratch_shapes=[pltpu.VMEM((heads, blk, LANES), F32), pltpu.VMEM((heads, blk, LANES), F32),
                            pltpu.VMEM((heads, blk, HEAD_DIM), F32)],
        ),
        out_shape=jax.ShapeDtypeStruct((b, s, n_heads * HEAD_DIM), BF16),
        compiler_params=_params(("parallel", "parallel", "arbitrary"), 48),
        name="attention",
    )(qi_tab, ki_tab, qkv3, qkv3, qkv3, bias)


def _branch_count_bias(s, blk, patterns):
    nq = s // blk
    off = jnp.arange(nq, dtype=jnp.int32)[:, None, None] * blk
    d = off + jnp.arange(blk, dtype=jnp.int32)[None, :, None] - jnp.arange(blk, dtype=jnp.int32)[None, None, :]
    cnt = jnp.zeros(d.shape, F32)
    for window, dilation in patterns:
        cnt = cnt + ((d >= 0) & (d % dilation == 0) & (d <= window)).astype(F32)
    return jnp.where(cnt > 0, jnp.log(jnp.maximum(cnt, 1.0)), NEG)


def _mm_res_ln_kernel(a_ref, w_ref, h_ref, g_ref, b_ref, of_ref, ob_ref, acc_sc, *, nk, alpha):
    k = pl.program_id(1)

    @pl.when(k == 0)
    def _():
        acc_sc[...] = jnp.zeros_like(acc_sc)

    acc_sc[...] += jnp.dot(a_ref[...], w_ref[...], preferred_element_type=F32)

    @pl.when(k == nk - 1)
    def _():
        out = _layer_norm_rows(alpha * h_ref[...] + acc_sc[...], g_ref[...], b_ref[...])
        of_ref[...] = out
        ob_ref[...] = out.astype(BF16)


def _mm_res_ln(a, w, h, g, b, alpha, tm=512, tk=1024):
    m, kdim = a.shape
    n = w.shape[1]
    nk = kdim // tk
    kern = functools.partial(_mm_res_ln_kernel, nk=nk, alpha=alpha)
    row = pl.BlockSpec((tm, n), lambda i, k: (i, 0))
    vec = pl.BlockSpec((1, n), lambda i, k: (0, 0))
    return pl.pallas_call(
        kern,
        grid=(m // tm, nk),
        in_specs=[pl.BlockSpec((tm, tk), lambda i, k: (i, k)),
                  pl.BlockSpec((tk, n), lambda i, k: (k, 0)),
                  row, vec, vec],
        out_specs=[row, row],
        out_shape=[jax.ShapeDtypeStruct((m, n), F32), jax.ShapeDtypeStruct((m, n), BF16)],
        scratch_shapes=[pltpu.VMEM((tm, n), F32)],
        compiler_params=_params(("parallel", "arbitrary"), 48),
        name="mm_res_ln",
    )(a, w, h, g.reshape(1, n), b.reshape(1, n))


def _router_kernel(h_ref, wt_ref, b_ref, e_ref, g_ref):
    logits = lax.dot_general(wt_ref[...], h_ref[...], (((1,), (1,)), ((), ())),
                             precision=HIGHEST, preferred_element_type=F32)
    mx = jnp.max(logits, axis=0, keepdims=True)
    ex = jnp.exp(logits - mx)
    scores = ex / jnp.sum(ex, axis=0, keepdims=True)
    sel = scores + b_ref[...]
    tm = sel.shape[1]
    li = lax.broadcasted_iota(jnp.int32, (EPG, tm), 0)

    def first_argmax(v):
        m = jnp.max(v, axis=0, keepdims=True)
        return m, jnp.min(jnp.where(v == m, li, EPG), axis=0, keepdims=True)

    gs, i1s, i2s = [], [], []
    for g in range(N_GROUPS):
        slab = sel[g * EPG:(g + 1) * EPG, :]
        m1, i1 = first_argmax(slab)
        m2, i2 = first_argmax(jnp.where(li == i1, NEG, slab))
        gs.append(m1 + m2)
        i1s.append(i1)
        i2s.append(i2)
    best, gidx = gs[0], jnp.zeros((1, tm), jnp.int32)
    for g in range(1, N_GROUPS):
        better = gs[g] > best
        best = jnp.where(better, gs[g], best)
        gidx = jnp.where(better, g, gidx)
    l1 = jnp.zeros((1, tm), jnp.int32)
    l2 = jnp.zeros((1, tm), jnp.int32)
    s1 = jnp.zeros((1, tm), F32)
    s2 = jnp.zeros((1, tm), F32)
    for g in range(N_GROUPS):
        here = gidx == g
        slab = scores[g * EPG:(g + 1) * EPG, :]
        l1 = jnp.where(here, i1s[g], l1)
        l2 = jnp.where(here, i2s[g], l2)
        s1 = jnp.where(here, jnp.sum(jnp.where(li == i1s[g], slab, 0.0), axis=0, keepdims=True), s1)
        s2 = jnp.where(here, jnp.sum(jnp.where(li == i2s[g], slab, 0.0), axis=0, keepdims=True), s2)
    tot = s1 + s2
    e_ref[0:1, :] = gidx * EPG + l1
    e_ref[1:2, :] = gidx * EPG + l2
    g_ref[0:1, :] = s1 / tot
    g_ref[1:2, :] = s2 / tot


def _router(h, router_wt, bias_col, tm=512):
    t, d = h.shape
    return pl.pallas_call(
        _router_kernel,
        grid=(t // tm,),
        in_specs=[pl.BlockSpec((tm, d), lambda i: (i, 0)),
                  pl.BlockSpec((N_EXPERTS, d), lambda i: (0, 0)),
                  pl.BlockSpec((N_EXPERTS, 1), lambda i: (0, 0))],
        out_specs=[pl.BlockSpec((TOP_K, tm), lambda i: (0, i)), pl.BlockSpec((TOP_K, tm), lambda i: (0, i))],
        out_shape=[jax.ShapeDtypeStruct((TOP_K, t), jnp.int32), jax.ShapeDtypeStruct((TOP_K, t), F32)],
        compiler_params=_params(("parallel",), 32),
        name="router",
    )(h, router_wt, bias_col)


ROW_GROUP = SUBLANES
N_SLOT_BUFS = 2


def _experts_kernel(be_ref, ng_ref, first_ref, nxt_ref, nused_ref, src_ref, dst_ref,
                    h_hbm, wg_hbm, wu_hbm, wd_hbm, ys_hbm,
                    xbuf, obuf, wgl, wul, wdl, wgb, wub, wdb, gsem, ssem, wsem, *, layer, n_slots):
    i = pl.program_id(0)
    n_used = nused_ref[0]
    slot = i % N_SLOT_BUFS

    def start_gather(blk, sl):
        base = blk * MOE_BLOCK

        def body(c, carry):
            for u in range(ROW_GROUP):
                r = c * ROW_GROUP + u
                pltpu.make_async_copy(h_hbm.at[pl.ds(src_ref[base + r], 1), :],
                                      xbuf.at[sl, pl.ds(r, 1), :], gsem.at[sl]).start()
            return carry

        lax.fori_loop(0, ng_ref[blk], body, 0)

    def start_scatter(blk, sl):
        base = blk * MOE_BLOCK

        def body(c, carry):
            for u in range(ROW_GROUP):
                r = c * ROW_GROUP + u
                pltpu.make_async_copy(obuf.at[sl, pl.ds(r, 1), :],
                                      ys_hbm.at[pl.ds(dst_ref[base + r], 1), :], ssem.at[sl]).start()
            return carry

        lax.fori_loop(0, ng_ref[blk], body, 0)

    def wait_groups(n_groups, group_copy):
        def body(c, carry):
            group_copy.wait()
            return carry

        lax.fori_loop(0, n_groups, body, 0)

    def gather_group(sl):
        return pltpu.make_async_copy(h_hbm.at[pl.ds(0, ROW_GROUP), :], xbuf.at[sl, pl.ds(0, ROW_GROUP), :], gsem.at[sl])

    def scatter_group(sl):
        return pltpu.make_async_copy(obuf.at[sl, pl.ds(0, ROW_GROUP), :], ys_hbm.at[pl.ds(0, ROW_GROUP), :], ssem.at[sl])

    w_hbm = (wg_hbm, wu_hbm, wd_hbm)
    w_land = (wgl, wul, wdl)
    w_work = (wgb, wub, wdb)

    def weight_copy(e, k):
        return pltpu.make_async_copy(w_hbm[k].at[layer, e], w_land[k], wsem.at[k])

    @pl.when(i == 0)
    def _():
        xbuf[...] = jnp.zeros_like(xbuf)
        spare = pltpu.make_async_copy(xbuf.at[0, pl.ds(0, N_SLOT_BUFS * ROW_GROUP), :],
                                      ys_hbm.at[pl.ds(n_slots, N_SLOT_BUFS * ROW_GROUP), :], ssem.at[0])
        spare.start()
        spare.wait()
        start_gather(0, 0)
        for k in range(len(w_hbm)):
            weight_copy(be_ref[0], k).start()

    @pl.when(i < n_used)
    def _():
        @pl.when(i + 1 < n_used)
        def _():
            start_gather(i + 1, 1 - slot)

        @pl.when(first_ref[i] == 1)
        def _():
            nxt = nxt_ref[i]
            for k in range(len(w_hbm)):
                weight_copy(be_ref[i], k).wait()
                w_work[k][...] = w_land[k][...].astype(BF16)

                @pl.when(nxt >= 0)
                def _():
                    weight_copy(nxt, k).start()

        wait_groups(ng_ref[i], gather_group(slot))

        @pl.when(i >= N_SLOT_BUFS)
        def _():
            wait_groups(ng_ref[i - N_SLOT_BUFS], scatter_group(slot))

        x = xbuf[slot].astype(BF16)
        hg = jnp.dot(x, wgb[...], preferred_element_type=F32)
        hu = jnp.dot(x, wub[...], preferred_element_type=F32)
        act = (_silu(hg) * hu).astype(BF16)
        obuf[slot] = jnp.dot(act, wdb[...], preferred_element_type=F32)
        start_scatter(i, slot)

        @pl.when(i == n_used - 1)
        def _():
            wait_groups(ng_ref[i], scatter_group(slot))

            @pl.when(i >= 1)
            def _():
                wait_groups(ng_ref[i - 1], scatter_group(1 - slot))


def _experts(meta, h, w_gate, w_up, w_down, layer, n_slots):
    t, d = h.shape
    de = w_gate.shape[3]
    nb = meta[0].shape[0]
    any_spec = pl.BlockSpec(memory_space=pl.ANY)
    return pl.pallas_call(
        functools.partial(_experts_kernel, layer=layer, n_slots=n_slots),
        grid_spec=pltpu.PrefetchScalarGridSpec(
            num_scalar_prefetch=len(meta),
            grid=(nb,),
            in_specs=[any_spec, any_spec, any_spec, any_spec],
            out_specs=any_spec,
            scratch_shapes=[pltpu.VMEM((N_SLOT_BUFS, MOE_BLOCK, d), F32), pltpu.VMEM((N_SLOT_BUFS, MOE_BLOCK, d), F32),
                            pltpu.VMEM((d, de), F32), pltpu.VMEM((d, de), F32), pltpu.VMEM((de, d), F32),
                            pltpu.VMEM((d, de), BF16), pltpu.VMEM((d, de), BF16), pltpu.VMEM((de, d), BF16),
                            pltpu.SemaphoreType.DMA((N_SLOT_BUFS,)), pltpu.SemaphoreType.DMA((N_SLOT_BUFS,)),
                            pltpu.SemaphoreType.DMA((3,))],
        ),
        out_shape=jax.ShapeDtypeStruct((n_slots + N_SLOT_BUFS * ROW_GROUP, d), F32),
        compiler_params=_params(("arbitrary",), 48),
        name="experts",
    )(*meta, h, w_gate, w_up, w_down)


def _combine_ln_kernel(y0_ref, y1_ref, gate_ref, h_ref, g_ref, b_ref, of_ref, ob_ref, *, alpha):
    gt = gate_ref[...]
    moe = gt[:, 0:1] * y0_ref[...] + gt[:, 1:2] * y1_ref[...]
    out = _layer_norm_rows(alpha * h_ref[...] + moe, g_ref[...], b_ref[...])
    of_ref[...] = out
    ob_ref[...] = out.astype(BF16)


def _combine_ln(ys, gate_t, h, g, b, alpha, tm=256):
    t, d = h.shape
    row = pl.BlockSpec((tm, d), lambda i: (i, 0))
    vec = pl.BlockSpec((1, d), lambda i: (0, 0))
    return pl.pallas_call(
        functools.partial(_combine_ln_kernel, alpha=alpha),
        grid=(t // tm,),
        in_specs=[row, pl.BlockSpec((tm, d), lambda i: (t // tm + i, 0)),
                  pl.BlockSpec((tm, TOP_K), lambda i: (i, 0)), row, vec, vec],
        out_specs=[row, row],
        out_shape=[jax.ShapeDtypeStruct((t, d), F32), jax.ShapeDtypeStruct((t, d), BF16)],
        compiler_params=_params(("parallel",), 40),
        name="combine_ln",
    )(ys, ys, gate_t, h, g.reshape(1, d), b.reshape(1, d))


def _moe_layer(h, router_wt, bias_col, w_gate, w_up, w_down, layer, ln_g, ln_b, alpha):
    t, d = h.shape
    e_idx, gate = _router(h, router_wt, bias_col)
    n_slots = t * TOP_K
    e_flat = e_idx.reshape(n_slots)
    experts = jnp.arange(N_EXPERTS, dtype=jnp.int32)
    onehot = (e_flat[:, None] == experts[None, :]).astype(jnp.int32)
    csum = jnp.cumsum(onehot, axis=0)
    counts = csum[-1]
    padded = ((counts + MOE_BLOCK - 1) // MOE_BLOCK) * MOE_BLOCK
    pad_end = jnp.cumsum(padded)
    pad_start = pad_end - padded
    dest = jnp.sum(onehot * (csum - 1 + pad_start[None, :]), axis=1).astype(jnp.int32)
    n_blocks = n_slots // MOE_BLOCK + N_EXPERTS
    rows = n_blocks * MOE_BLOCK
    row_slot = jnp.full((rows,), -1, jnp.int32).at[dest].set(jnp.arange(n_slots, dtype=jnp.int32),
                                                             unique_indices=True)
    row_id = jnp.arange(rows, dtype=jnp.int32)
    valid = row_slot >= 0
    row_src = jnp.where(valid, row_slot % t, 0)
    spare = n_slots + ((row_id // MOE_BLOCK) % N_SLOT_BUFS) * ROW_GROUP + row_id % ROW_GROUP
    row_dst = jnp.where(valid, row_slot, spare)
    blk_row0 = jnp.arange(n_blocks, dtype=jnp.int32) * MOE_BLOCK
    block_e = jnp.clip(jnp.searchsorted(pad_end, blk_row0, side='right'), 0, N_EXPERTS - 1).astype(jnp.int32)
    n_valid = jnp.where(blk_row0 < pad_end[-1],
                        jnp.clip(counts[block_e] - (blk_row0 - pad_start[block_e]), 0, MOE_BLOCK), 0)
    n_groups = ((n_valid + ROW_GROUP - 1) // ROW_GROUP).astype(jnp.int32)
    first = jnp.concatenate([jnp.ones((1,), jnp.int32), (block_e[1:] != block_e[:-1]).astype(jnp.int32)])
    has = jnp.where(counts > 0, experts, N_EXPERTS)
    suffix_min = lax.cummin(has, axis=0, reverse=True)
    next_nonempty = jnp.concatenate([suffix_min[1:], jnp.full((1,), N_EXPERTS, jnp.int32)])
    nxt = jnp.where(next_nonempty[block_e] < N_EXPERTS, next_nonempty[block_e], -1).astype(jnp.int32)
    n_used = (pad_end[-1] // MOE_BLOCK).astype(jnp.int32).reshape(1)
    meta = (block_e, n_groups, first, nxt, n_used, row_src.astype(jnp.int32), row_dst.astype(jnp.int32))
    ys = _experts(meta, h, w_gate, w_up, w_down, layer, n_slots)
    return _combine_ln(ys, gate.T, h, ln_g, ln_b, alpha)


def _expand_heads(x, e_bf16):
    hi = x.astype(BF16)
    r1 = x - hi.astype(F32)
    mid = r1.astype(BF16)
    lo = (r1 - mid.astype(F32)).astype(BF16)
    dot = lambda a: jnp.dot(a, e_bf16, preferred_element_type=F32)
    return dot(hi) + dot(mid) + dot(lo)


def _ssd_kernel(z_ref, xbc_ref, prev_ref, dt_ref, cw_ref, cb_ref, dtb_ref, alog_ref, dexp_ref, nw_ref, e_ref,
                o_ref, state_sc, xs_sc, *, d_inner):
    c = pl.program_id(1)
    L = SSM_CHUNK
    gw = d_inner // SSM_GROUPS
    hpg = gw // SSM_HEAD_DIM
    pad = SUBLANES

    @pl.when(c == 0)
    def _():
        state_sc[...] = jnp.zeros_like(state_sc)

    xs_sc[0:pad, :] = jnp.where(c > 0, prev_ref[0], 0.0)
    xs_sc[pad:pad + L, :] = xbc_ref[0]

    def conv_silu(col0, width):
        cols = slice(col0, col0 + width)
        acc = cb_ref[:, cols] + xs_sc[pad:pad + L, cols] * cw_ref[SSM_CONV - 1:SSM_CONV, cols]
        for j in range(1, SSM_CONV):
            acc = acc + xs_sc[pad - j:pad - j + L, cols] * cw_ref[SSM_CONV - 1 - j:SSM_CONV - j, cols]
        return _silu(acc)

    dtv = dt_ref[0] + dtb_ref[...]
    dt = jnp.maximum(dtv, 0.0) + jnp.log1p(jnp.exp(-jnp.abs(dtv)))
    a = dt * (-jnp.exp(alog_ref[...]))
    ri = lax.broadcasted_iota(jnp.int32, (L, L), 0)
    ci = lax.broadcasted_iota(jnp.int32, (L, L), 1)
    causal = ri >= ci
    a_cum = jnp.dot(causal.astype(F32), a, precision=HIGHEST, preferred_element_type=F32)
    a_cum_t = a_cum.T
    e_mat = e_ref[...]
    acum_x = _expand_heads(a_cum, e_mat)
    dt_x = _expand_heads(dt, e_mat)
    lane = lax.broadcasted_iota(jnp.int32, (L, LANES), 1)
    first_head = lane < SSM_HEAD_DIM

    for g in range(SSM_GROUPS):
        cols = slice(g * gw, (g + 1) * gw)
        xg = conv_silu(g * gw, gw)
        bg = conv_silu(d_inner + g * SSM_STATE, SSM_STATE)
        cg = conv_silu(d_inner + (SSM_GROUPS + g) * SSM_STATE, SSM_STATE)
        ac = acum_x[:, cols]
        last = ac[L - 1:L, :]
        xdt = xg * dt_x[:, cols]
        cb16 = cg.astype(BF16)
        bb16 = bg.astype(BF16)
        cbm = lax.dot_general(cb16, bb16, (((1,), (1,)), ((), ())), preferred_element_type=F32)
        st = state_sc[g]
        y_off = jnp.dot(cb16, st.astype(BF16), preferred_element_type=F32) * jnp.exp(ac)
        xd = (xdt * jnp.exp(last - ac)).astype(BF16)
        state_sc[g] = st * jnp.exp(last) + jnp.dot(bg.T.astype(BF16), xd, preferred_element_type=F32)
        xdt16 = xdt.astype(BF16)
        pieces = []
        for jp in range(hpg // 2):
            lhs = []
            for hh in (g * hpg + 2 * jp, g * hpg + 2 * jp + 1):
                seg = jnp.where(causal, a_cum[:, hh:hh + 1] - a_cum_t[hh:hh + 1, :], NEG)
                lhs.append((cbm * jnp.exp(seg)).astype(BF16))
            xp = xdt16[:, jp * LANES:(jp + 1) * LANES]
            zero = jnp.zeros_like(xp)
            rhs = jnp.concatenate([jnp.where(first_head, xp, zero), jnp.where(first_head, zero, xp)], axis=0)
            pieces.append(jnp.dot(jnp.concatenate(lhs, axis=1), rhs, preferred_element_type=F32))
        y = jnp.concatenate(pieces, axis=1) + y_off + dexp_ref[:, cols] * xg
        y = y * _silu(z_ref[0, :, cols])
        y = y * lax.rsqrt(jnp.mean(y * y, axis=1, keepdims=True) + RMS_EPS) * nw_ref[:, cols]
        o_ref[0, :, cols] = y.astype(o_ref.dtype)


def _ssd(z3, xbc3, dt3, conv_w, conv_b, dt_bias_p, a_log_p, d_exp, norm_w, e_mat):
    b, s, d_inner = z3.shape
    cdim = xbc3.shape[2]
    nc = s // SSM_CHUNK
    gw = d_inner // SSM_GROUPS
    rows_per_prev = SSM_CHUNK // SUBLANES
    full = lambda shape: pl.BlockSpec(shape, lambda bi, c: tuple(0 for _ in shape))
    return pl.pallas_call(
        functools.partial(_ssd_kernel, d_inner=d_inner),
        grid=(b, nc),
        in_specs=[pl.BlockSpec((1, SSM_CHUNK, d_inner), lambda bi, c: (bi, c, 0)),
                  pl.BlockSpec((1, SSM_CHUNK, cdim), lambda bi, c: (bi, c, 0)),
                  pl.BlockSpec((1, SUBLANES, cdim), lambda bi, c: (bi, jnp.maximum(c * rows_per_prev - 1, 0), 0)),
                  pl.BlockSpec((1, SSM_CHUNK, LANES), lambda bi, c: (bi, c, 0)),
                  full((SSM_CONV, cdim)), full((1, cdim)), full((1, LANES)), full((1, LANES)),
                  full((1, d_inner)), full((1, d_inner)), full((LANES, d_inner))],
        out_specs=pl.BlockSpec((1, SSM_CHUNK, d_inner), lambda bi, c: (bi, c, 0)),
        out_shape=jax.ShapeDtypeStruct((b, s, d_inner), BF16),
        scratch_shapes=[pltpu.VMEM((SSM_GROUPS, SSM_STATE, gw), F32),
                        pltpu.VMEM((SUBLANES + SSM_CHUNK, cdim), F32)],
        compiler_params=_params(("parallel", "arbitrary"), 56),
        name="ssd",
    )(z3, xbc3, xbc3, dt3, conv_w, conv_b, dt_bias_p, a_log_p, d_exp, norm_w, e_mat)


def kernel(x, positions, attn_w_qkv, attn_w_o, ssm_w_in, ssm_conv_w, ssm_conv_b, ssm_dt_bias, ssm_A_log, ssm_D, ssm_norm_w, ssm_w_out, router_w, router_bias, moe_w_gate, moe_w_up, moe_w_down, ln_mix_g, ln_mix_b, ln_ffn_g, ln_ffn_b):
    bsz, seq, d_model = x.shape
    t = bsz * seq
    depth = moe_w_gate.shape[0]
    alpha = (2 * depth) ** 0.25
    n_heads = d_model // HEAD_DIM
    patterns = ((128, 1), (512, 4), (2048, 16))
    attn_blk = 512

    half = ROT_DIM // 2
    inv_freq = ROPE_THETA ** (-jnp.arange(0, ROT_DIM, 2, dtype=F32) / ROT_DIM)
    ang = positions.astype(F32).reshape(t, 1) * inv_freq
    cos, sin = jnp.cos(ang), jnp.sin(ang)
    zeros_r = jnp.zeros((t, LANES - ROT_DIM), F32)
    cos_t = jnp.concatenate([cos, cos, jnp.ones((t, LANES - ROT_DIM), F32)], axis=1)
    sa_t = jnp.concatenate([-sin, jnp.zeros((t, half), F32), zeros_r], axis=1)
    sb_t = jnp.concatenate([jnp.zeros((t, half), F32), sin, zeros_r], axis=1)
    bias = _branch_count_bias(seq, attn_blk, patterns)

    router_wt = router_w.T
    bias_col = router_bias.reshape(N_EXPERTS, 1).astype(F32)

    h = x.reshape(t, d_model)
    hb = h.astype(BF16)
    for i in range(depth):
        j = i // 2
        if i % 2 == 0:
            qkv = _qkv_rope(hb, attn_w_qkv[j].astype(BF16), cos_t, sa_t, sb_t, d_model)
            att = _attention(qkv.reshape(bsz, seq, 3 * d_model), bias, n_heads, attn_blk)
            h, hb = _mm_res_ln(att.reshape(t, d_model), attn_w_o[j].astype(BF16), h, ln_mix_g[i], ln_mix_b[i], alpha)
        else:
            d_inner = ssm_norm_w.shape[1]
            n_ssm_heads = ssm_dt_bias.shape[1]
            cdim = ssm_conv_w.shape[2]
            w_in = ssm_w_in[j]
            z = _mm(hb, w_in[:, :d_inner].astype(BF16), F32)
            xbc = _mm(hb, w_in[:, d_inner:d_inner + cdim].astype(BF16), F32)
            w_dt = jnp.pad(w_in[:, d_inner + cdim:], ((0, 0), (0, LANES - n_ssm_heads))).astype(BF16)
            dt = _mm(hb, w_dt, F32)
            hpad = (0, LANES - n_ssm_heads)
            head_of_ch = jnp.arange(d_inner, dtype=jnp.int32) // SSM_HEAD_DIM
            e_mat = (jnp.arange(LANES, dtype=jnp.int32)[:, None] == head_of_ch[None, :]).astype(BF16)
            y = _ssd(z.reshape(bsz, seq, d_inner), xbc.reshape(bsz, seq, cdim), dt.reshape(bsz, seq, LANES),
                     ssm_conv_w[j], ssm_conv_b[j].reshape(1, cdim),
                     jnp.pad(ssm_dt_bias[j], hpad).reshape(1, LANES), jnp.pad(ssm_A_log[j], hpad).reshape(1, LANES),
                     jnp.repeat(ssm_D[j], SSM_HEAD_DIM).reshape(1, d_inner), ssm_norm_w[j].reshape(1, d_inner), e_mat)
            h, hb = _mm_res_ln(y.reshape(t, d_inner), ssm_w_out[j].astype(BF16), h, ln_mix_g[i], ln_mix_b[i], alpha)
        h, hb = _moe_layer(h, router_wt, bias_col, moe_w_gate, moe_w_up, moe_w_down, i,
                           ln_ffn_g[i], ln_ffn_b[i], alpha)
    return h.reshape(bsz, seq, d_model)
```

```python
import functools
import math

import jax
import jax.numpy as jnp
from jax import lax
from jax.experimental import pallas as pl
from jax.experimental.pallas import tpu as pltpu

F32 = jnp.float32
BF16 = jnp.bfloat16
HIGHEST = lax.Precision.HIGHEST

LANES = 128
SUBLANES = 8
MIB = 1024 * 1024

HEAD_DIM = 128
ROT_DIM = HEAD_DIM // 4
ROPE_THETA = 500000.0
N_EXPERTS = 64
N_GROUPS = 8
EPG = N_EXPERTS // N_GROUPS
TOP_K = 2
MOE_BLOCK = 128
SSM_HEAD_DIM = 64
SSM_GROUPS = 8
SSM_STATE = 128
SSM_CONV = 4
SSM_CHUNK = 128
LN_EPS = 1e-5
RMS_EPS = 1e-5
NEG = -1e30


def _params(sem, vmem_mib):
    return pltpu.CompilerParams(dimension_semantics=sem, vmem_limit_bytes=vmem_mib * MIB)


def _silu(x):
    return x / (1.0 + jnp.exp(-x))


def _layer_norm_rows(y, g, b):
    mu = jnp.mean(y, axis=-1, keepdims=True)
    yc = y - mu
    var = jnp.mean(yc * yc, axis=-1, keepdims=True)
    return yc * lax.rsqrt(var + LN_EPS) * g + b


def _mm_kernel(a_ref, w_ref, o_ref):
    o_ref[...] = jnp.dot(a_ref[...], w_ref[...], preferred_element_type=F32).astype(o_ref.dtype)


def _mm(a, w, out_dtype, tm=1024, tn=512):
    m, k = a.shape
    n = w.shape[1]
    tn = min(tn, n)
    return pl.pallas_call(
        _mm_kernel,
        grid=(m // tm, n // tn),
        in_specs=[pl.BlockSpec((tm, k), lambda i, j: (i, 0)),
                  pl.BlockSpec((k, tn), lambda i, j: (0, j))],
        out_specs=pl.BlockSpec((tm, tn), lambda i, j: (i, j)),
        out_shape=jax.ShapeDtypeStruct((m, n), out_dtype),
        compiler_params=_params(("parallel", "arbitrary"), 48),
        name="mm",
    )(a, w)


def _qkv_rope_kernel(x_ref, w_ref, cos_ref, sa_ref, sb_ref, o_ref, *, n_rope_tiles, n_q_tiles, scale):
    n = pl.program_id(1)
    acc = jnp.dot(x_ref[...], w_ref[...], preferred_element_type=F32)
    tn = acc.shape[1]

    @pl.when(n < n_rope_tiles)
    def _():
        c = cos_ref[...]
        sa = sa_ref[...]
        sb = sb_ref[...]
        sc = jnp.where(n < n_q_tiles, scale, 1.0).astype(F32)
        for j in range(tn // LANES):
            t = acc[:, j * LANES:(j + 1) * LANES]
            r = t * c + pltpu.roll(t, LANES - ROT_DIM // 2, 1) * sa + pltpu.roll(t, ROT_DIM // 2, 1) * sb
            o_ref[:, j * LANES:(j + 1) * LANES] = (r * sc).astype(o_ref.dtype)

    @pl.when(n >= n_rope_tiles)
    def _():
        o_ref[...] = acc.astype(o_ref.dtype)


def _qkv_rope(xb, w, cos_t, sa_t, sb_t, d_model, tm=1024, tn=512):
    m, k = xb.shape
    n = w.shape[1]
    kern = functools.partial(_qkv_rope_kernel, n_rope_tiles=2 * d_model // tn, n_q_tiles=d_model // tn,
                             scale=HEAD_DIM ** -0.5)
    tab = pl.BlockSpec((tm, LANES), lambda i, j: (i, 0))
    return pl.pallas_call(
        kern,
        grid=(m // tm, n // tn),
        in_specs=[pl.BlockSpec((tm, k), lambda i, j: (i, 0)),
                  pl.BlockSpec((k, tn), lambda i, j: (0, j)),
                  tab, tab, tab],
        out_specs=pl.BlockSpec((tm, tn), lambda i, j: (i, j)),
        out_shape=jax.ShapeDtypeStruct((m, n), BF16),
        compiler_params=_params(("parallel", "arbitrary"), 48),
        name="qkv_rope",
    )(xb, w, cos_t, sa_t, sb_t)


def _attn_kernel(qi_ref, ki_ref, q_ref, k_ref, v_ref, bias_ref, o_ref, m_sc, l_sc, acc_sc, *, heads):
    p = pl.program_id(2)
    qi = qi_ref[p]
    ki = ki_ref[p]
    blk = q_ref.shape[1]

    @pl.when(ki == 0)
    def _():
        m_sc[...] = jnp.full_like(m_sc, 0.1 * NEG)
        l_sc[...] = jnp.zeros_like(l_sc)
        acc_sc[...] = jnp.zeros_like(acc_sc)

    bias = bias_ref[qi - ki]
    for j in range(heads):
        cols = slice(j * HEAD_DIM, (j + 1) * HEAD_DIM)
        s = lax.dot_general(q_ref[0, :, cols], k_ref[0, :, cols], (((1,), (1,)), ((), ())),
                            preferred_element_type=F32) + bias
        m_prev = m_sc[j]
        m_new = jnp.maximum(m_prev, jnp.max(s, axis=1, keepdims=True))
        alpha = jnp.exp(m_prev - m_new)
        pr = jnp.exp(s - jnp.concatenate([m_new] * (blk // LANES), axis=1))
        l_sc[j] = alpha * l_sc[j] + jnp.sum(pr, axis=1, keepdims=True)
        acc_sc[j] = alpha * acc_sc[j] + jnp.dot(pr.astype(BF16), v_ref[0, :, cols], preferred_element_type=F32)
        m_sc[j] = m_new

    @pl.when(ki == qi)
    def _():
        for j in range(heads):
            o_ref[0, :, j * HEAD_DIM:(j + 1) * HEAD_DIM] = (acc_sc[j] / l_sc[j]).astype(o_ref.dtype)


def _attention(qkv3, bias, n_heads, blk, heads=4):
    b, s, _ = qkv3.shape
    nq = s // blk
    pairs = [(qi, ki) for qi in range(nq) for ki in range(qi + 1)]
    qi_tab = jnp.asarray([pq for pq, _ in pairs], jnp.int32)
    ki_tab = jnp.asarray([pk for _, pk in pairs], jnp.int32)
    hg = n_heads // heads
    w = heads * HEAD_DIM
    return pl.pallas_call(
        functools.partial(_attn_kernel, heads=heads),
        grid_spec=pltpu.PrefetchScalarGridSpec(
            num_scalar_prefetch=2,
            grid=(b, hg, len(pairs)),
            in_specs=[pl.BlockSpec((1, blk, w), lambda bi, h, p, qt, kt: (bi, qt[p], h)),
                      pl.BlockSpec((1, blk, w), lambda bi, h, p, qt, kt: (bi, kt[p], hg + h)),
                      pl.BlockSpec((1, blk, w), lambda bi, h, p, qt, kt: (bi, kt[p], 2 * hg + h)),
                      pl.BlockSpec((nq, blk, blk), lambda bi, h, p, qt, kt: (0, 0, 0))],
            out_specs=pl.BlockSpec((1, blk, w), lambda bi, h, p, qt, kt: (bi, qt[p], h)),
            scratch_shapes=[pltpu.VMEM((heads, blk, LANES), F32), pltpu.VMEM((heads, blk, LANES), F32),
                            pltpu.VMEM((heads, blk, HEAD_DIM), F32)],
        ),
        out_shape=jax.ShapeDtypeStruct((b, s, n_heads * HEAD_DIM), BF16),
        compiler_params=_params(("parallel", "parallel", "arbitrary"), 48),
        name="attention",
    )(qi_tab, ki_tab, qkv3, qkv3, qkv3, bias)


def _branch_count_bias(s, blk, patterns):
    nq = s // blk
    off = jnp.arange(nq, dtype=jnp.int32)[:, None, None] * blk
    d = off + jnp.arange(blk, dtype=jnp.int32)[None, :, None] - jnp.arange(blk, dtype=jnp.int32)[None, None, :]
    cnt = jnp.zeros(d.shape, F32)
    for window, dilation in patterns:
        cnt = cnt + ((d >= 0) & (d % dilation == 0) & (d <= window)).astype(F32)
    return jnp.where(cnt > 0, jnp.log(jnp.maximum(cnt, 1.0)), NEG)


def _mm_res_ln_kernel(a_ref, w_ref, h_ref, g_ref, b_ref, of_ref, ob_ref, acc_sc, *, nk, alpha):
    k = pl.program_id(1)

    @pl.when(k == 0)
    def _():
        acc_sc[...] = jnp.zeros_like(acc_sc)

    acc_sc[...] += jnp.dot(a_ref[...], w_ref[...], preferred_element_type=F32)

    @pl.when(k == nk - 1)
    def _():
        out = _layer_norm_rows(alpha * h_ref[...] + acc_sc[...], g_ref[...], b_ref[...])
        of_ref[...] = out
        ob_ref[...] = out.astype(BF16)


def _mm_res_ln(a, w, h, g, b, alpha, tm=512, tk=1024):
    m, kdim = a.shape
    n = w.shape[1]
    nk = kdim // tk
    kern = functools.partial(_mm_res_ln_kernel, nk=nk, alpha=alpha)
    row = pl.BlockSpec((tm, n), lambda i, k: (i, 0))
    vec = pl.BlockSpec((1, n), lambda i, k: (0, 0))
    return pl.pallas_call(
        kern,
        grid=(m // tm, nk),
        in_specs=[pl.BlockSpec((tm, tk), lambda i, k: (i, k)),
                  pl.BlockSpec((tk, n), lambda i, k: (k, 0)),
                  row, vec, vec],
        out_specs=[row, row],
        out_shape=[jax.ShapeDtypeStruct((m, n), F32), jax.ShapeDtypeStruct((m, n), BF16)],
        scratch_shapes=[pltpu.VMEM((tm, n), F32)],
        compiler_params=_params(("parallel", "arbitrary"), 48),
        name="mm_res_ln",
    )(a, w, h, g.reshape(1, n), b.reshape(1, n))


def _router_kernel(h_ref, wt_ref, b_ref, e_ref, g_ref):
    logits = lax.dot_general(wt_ref[...], h_ref[...], (((1,), (1,)), ((), ())),
                             precision=HIGHEST, preferred_element_type=F32)
    mx = jnp.max(logits, axis=0, keepdims=True)
    ex = jnp.exp(logits - mx)
    scores = ex / jnp.sum(ex, axis=0, keepdims=True)
    sel = scores + b_ref[...]
    tm = sel.shape[1]
    li = lax.broadcasted_iota(jnp.int32, (EPG, tm), 0)

    def first_argmax(v):
        m = jnp.max(v, axis=0, keepdims=True)
        return m, jnp.min(jnp.where(v == m, li, EPG), axis=0, keepdims=True)

    gs, i1s, i2s = [], [], []
    for g in range(N_GROUPS):
        slab = sel[g * EPG:(g + 1) * EPG, :]
        m1, i1 = first_argmax(slab)
        m2, i2 = first_argmax(jnp.where(li == i1, NEG, slab))
        gs.append(m1 + m2)
        i1s.append(i1)
        i2s.append(i2)
    best, gidx = gs[0], jnp.zeros((1, tm), jnp.int32)
    for g in range(1, N_GROUPS):
        better = gs[g] > best
        best = jnp.where(better, gs[g], best)
        gidx = jnp.where(better, g, gidx)
    l1 = jnp.zeros((1, tm), jnp.int32)
    l2 = jnp.zeros((1, tm), jnp.int32)
    s1 = jnp.zeros((1, tm), F32)
    s2 = jnp.zeros((1, tm), F32)
    for g in range(N_GROUPS):
        here = gidx == g
        slab = scores[g * EPG:(g + 1) * EPG, :]
        l1 = jnp.where(here, i1s[g], l1)
        l2 = jnp.where(here, i2s[g], l2)
        s1 = jnp.where(here, jnp.sum(jnp.where(li == i1s[g], slab, 0.0), axis=0, keepdims=True), s1)
        s2 = jnp.where(here, jnp.sum(jnp.where(li == i2s[g], slab, 0.0), axis=0, keepdims=True), s2)
    tot = s1 + s2
    e_ref[0:1, :] = gidx * EPG + l1
    e_ref[1:2, :] = gidx * EPG + l2
    g_ref[0:1, :] = s1 / tot
    g_ref[1:2, :] = s2 / tot


def _router(h, router_wt, bias_col, tm=512):
    t, d = h.shape
    return pl.pallas_call(
        _router_kernel,
        grid=(t // tm,),
        in_specs=[pl.BlockSpec((tm, d), lambda i: (i, 0)),
                  pl.BlockSpec((N_EXPERTS, d), lambda i: (0, 0)),
                  pl.BlockSpec((N_EXPERTS, 1), lambda i: (0, 0))],
        out_specs=[pl.BlockSpec((TOP_K, tm), lambda i: (0, i)), pl.BlockSpec((TOP_K, tm), lambda i: (0, i))],
        out_shape=[jax.ShapeDtypeStruct((TOP_K, t), jnp.int32), jax.ShapeDtypeStruct((TOP_K, t), F32)],
        compiler_params=_params(("parallel",), 32),
        name="router",
    )(h, router_wt, bias_col)


N_ROW_BUFS = 2
N_SPARE_ROWS = N_ROW_BUFS * MOE_BLOCK


def _experts_kernel(be_ref, first_ref, nxt_ref, nused_ref, src_ref, dst_ref,
                    h_hbm, wg_hbm, wu_hbm, wd_hbm, ys_hbm,
                    xb0, xb1, ob0, ob1, wgl, wul, wdl, wgb, wub, wdb, gsem, ssem, wsem, *, layer, n_slots):
    i = pl.program_id(0)
    n_used = nused_ref[0]
    xbufs = (xb0, xb1)
    obufs = (ob0, ob1)

    def gather_start(blk, s):
        base = (blk + 1) * MOE_BLOCK
        for r in range(MOE_BLOCK):
            pltpu.make_async_copy(h_hbm.at[pl.ds(src_ref[base + r], 1), :],
                                  xbufs[s].at[pl.ds(r, 1), :], gsem.at[s]).start()

    def scatter_start(blk, s):
        base = (blk + 1) * MOE_BLOCK
        for r in range(MOE_BLOCK):
            pltpu.make_async_copy(obufs[s].at[pl.ds(r, 1), :],
                                  ys_hbm.at[pl.ds(dst_ref[base + r], 1), :], ssem.at[s]).start()

    def gather_wait(s):
        pltpu.make_async_copy(h_hbm.at[pl.ds(0, MOE_BLOCK), :], xbufs[s], gsem.at[s]).wait()

    def scatter_wait(s):
        pltpu.make_async_copy(obufs[s], ys_hbm.at[pl.ds(0, MOE_BLOCK), :], ssem.at[s]).wait()

    w_hbm = (wg_hbm, wu_hbm, wd_hbm)
    w_land = (wgl, wul, wdl)
    w_work = (wgb, wub, wdb)

    def weight_copy(e, k):
        return pltpu.make_async_copy(w_hbm[k].at[layer, e], w_land[k], wsem.at[k])

    @pl.when(i == 0)
    def _():
        ob1[...] = jnp.zeros_like(ob1)
        spare = pltpu.make_async_copy(ob1, ys_hbm.at[pl.ds(n_slots + MOE_BLOCK, MOE_BLOCK), :], ssem.at[1])
        spare.start()
        spare.wait()
        gather_start(0, 0)
        for k in range(len(w_hbm)):
            weight_copy(be_ref[0], k).start()

    @pl.when(i < n_used)
    def _():
        @pl.when(first_ref[i] == 1)
        def _():
            nxt = nxt_ref[i]
            for k in range(len(w_hbm)):
                weight_copy(be_ref[i], k).wait()
                w_work[k][...] = w_land[k][...].astype(BF16)

                @pl.when(nxt >= 0)
                def _():
                    weight_copy(nxt, k).start()

        for s in range(N_ROW_BUFS):
            @pl.when(i % N_ROW_BUFS == s)
            def _():
                gather_wait(s)

                @pl.when(i >= 1)
                def _():
                    scatter_wait(s)

                x = xbufs[s][...].astype(BF16)
                hg = jnp.dot(x, wgb[...], preferred_element_type=F32)
                hu = jnp.dot(x, wub[...], preferred_element_type=F32)
                act = (_silu(hg) * hu).astype(BF16)
                obufs[s][...] = jnp.dot(act, wdb[...], preferred_element_type=F32)
                gather_start(i + 1, 1 - s)
                scatter_start(i - 1, 1 - s)

                @pl.when(i == n_used - 1)
                def _():
                    gather_wait(1 - s)
                    scatter_wait(1 - s)
                    scatter_start(i, s)
                    scatter_wait(s)


def _experts(meta, h, w_gate, w_up, w_down, layer, n_slots):
    t, d = h.shape
    de = w_gate.shape[3]
    nb = meta[0].shape[0]
    any_spec = pl.BlockSpec(memory_space=pl.ANY)
    rowbuf = pltpu.VMEM((MOE_BLOCK, d), F32)
    return pl.pallas_call(
        functools.partial(_experts_kernel, layer=layer, n_slots=n_slots),
        grid_spec=pltpu.PrefetchScalarGridSpec(
            num_scalar_prefetch=len(meta),
            grid=(nb,),
            in_specs=[any_spec, any_spec, any_spec, any_spec],
            out_specs=any_spec,
            scratch_shapes=[rowbuf, rowbuf, rowbuf, rowbuf,
                            pltpu.VMEM((d, de), F32), pltpu.VMEM((d, de), F32), pltpu.VMEM((de, d), F32),
                            pltpu.VMEM((d, de), BF16), pltpu.VMEM((d, de), BF16), pltpu.VMEM((de, d), BF16),
                            pltpu.SemaphoreType.DMA((N_ROW_BUFS,)), pltpu.SemaphoreType.DMA((N_ROW_BUFS,)),
                            pltpu.SemaphoreType.DMA((3,))],
        ),
        out_shape=jax.ShapeDtypeStruct((n_slots + N_SPARE_ROWS, d), F32),
        compiler_params=_params(("arbitrary",), 48),
        name="experts",
    )(*meta, h, w_gate, w_up, w_down)


def _combine_ln_kernel(y0_ref, y1_ref, gate_ref, h_ref, g_ref, b_ref, of_ref, ob_ref, *, alpha):
    gt = gate_ref[...]
    moe = gt[:, 0:1] * y0_ref[...] + gt[:, 1:2] * y1_ref[...]
    out = _layer_norm_rows(alpha * h_ref[...] + moe, g_ref[...], b_ref[...])
    of_ref[...] = out
    ob_ref[...] = out.astype(BF16)


def _combine_ln(ys, gate_t, h, g, b, alpha, tm=256):
    t, d = h.shape
    row = pl.BlockSpec((tm, d), lambda i: (i, 0))
    vec = pl.BlockSpec((1, d), lambda i: (0, 0))
    return pl.pallas_call(
        functools.partial(_combine_ln_kernel, alpha=alpha),
        grid=(t // tm,),
        in_specs=[row, pl.BlockSpec((tm, d), lambda i: (t // tm + i, 0)),
                  pl.BlockSpec((tm, TOP_K), lambda i: (i, 0)), row, vec, vec],
        out_specs=[row, row],
        out_shape=[jax.ShapeDtypeStruct((t, d), F32), jax.ShapeDtypeStruct((t, d), BF16)],
        compiler_params=_params(("parallel",), 40),
        name="combine_ln",
    )(ys, ys, gate_t, h, g.reshape(1, d), b.reshape(1, d))


def _moe_layer(h, router_wt, bias_col, w_gate, w_up, w_down, layer, ln_g, ln_b, alpha):
    t, d = h.shape
    e_idx, gate = _router(h, router_wt, bias_col)
    n_slots = t * TOP_K
    e_flat = e_idx.reshape(n_slots)
    experts = jnp.arange(N_EXPERTS, dtype=jnp.int32)
    onehot = (e_flat[:, None] == experts[None, :]).astype(jnp.int32)
    csum = jnp.cumsum(onehot, axis=0)
    counts = csum[-1]
    padded = ((counts + MOE_BLOCK - 1) // MOE_BLOCK) * MOE_BLOCK
    pad_end = jnp.cumsum(padded)
    pad_start = pad_end - padded
    dest = jnp.sum(onehot * (csum - 1 + pad_start[None, :]), axis=1).astype(jnp.int32)
    n_blocks = n_slots // MOE_BLOCK + N_EXPERTS
    rows = n_blocks * MOE_BLOCK
    row_slot = jnp.full((rows,), -1, jnp.int32).at[dest].set(jnp.arange(n_slots, dtype=jnp.int32),
                                                             unique_indices=True)
    row_id = jnp.arange(-MOE_BLOCK, rows, dtype=jnp.int32)
    row_slot = jnp.concatenate([jnp.full((MOE_BLOCK,), -1, jnp.int32), row_slot])
    valid = row_slot >= 0
    row_src = jnp.where(valid, row_slot % t, 0)
    spare = n_slots + ((row_id // MOE_BLOCK + 1) % N_ROW_BUFS) * MOE_BLOCK + row_id % MOE_BLOCK
    row_dst = jnp.where(valid, row_slot, spare)
    blk_row0 = jnp.arange(n_blocks, dtype=jnp.int32) * MOE_BLOCK
    block_e = jnp.clip(jnp.searchsorted(pad_end, blk_row0, side='right'), 0, N_EXPERTS - 1).astype(jnp.int32)
    first = jnp.concatenate([jnp.ones((1,), jnp.int32), (block_e[1:] != block_e[:-1]).astype(jnp.int32)])
    has = jnp.where(counts > 0, experts, N_EXPERTS)
    suffix_min = lax.cummin(has, axis=0, reverse=True)
    next_nonempty = jnp.concatenate([suffix_min[1:], jnp.full((1,), N_EXPERTS, jnp.int32)])
    nxt = jnp.where(next_nonempty[block_e] < N_EXPERTS, next_nonempty[block_e], -1).astype(jnp.int32)
    n_used = (pad_end[-1] // MOE_BLOCK).astype(jnp.int32).reshape(1)
    meta = (block_e, first, nxt, n_used, row_src.astype(jnp.int32), row_dst.astype(jnp.int32))
    ys = _experts(meta, h, w_gate, w_up, w_down, layer, n_slots)
    return _combine_ln(ys, gate.T, h, ln_g, ln_b, alpha)


def _expand_heads(x, e_bf16):
    hi = x.astype(BF16)
    r1 = x - hi.astype(F32)
    mid = r1.astype(BF16)
    lo = (r1 - mid.astype(F32)).astype(BF16)
    dot = lambda a: jnp.dot(a, e_bf16, preferred_element_type=F32)
    return dot(hi) + dot(mid) + dot(lo)


def _ssd_kernel(z_ref, xbc_ref, prev_ref, dt_ref, cw_ref, cb_ref, dtb_ref, alog_ref, dexp_ref, nw_ref, e_ref,
                o_ref, state_sc, xs_sc, *, d_inner):
    c = pl.program_id(1)
    L = SSM_CHUNK
    gw = d_inner // SSM_GROUPS
    hpg = gw // SSM_HEAD_DIM
    pad = SUBLANES

    @pl.when(c == 0)
    def _():
        state_sc[...] = jnp.zeros_like(state_sc)

    xs_sc[0:pad, :] = jnp.where(c > 0, prev_ref[0], 0.0)
    xs_sc[pad:pad + L, :] = xbc_ref[0]

    def conv_silu(col0, width):
        cols = slice(col0, col0 + width)
        acc = cb_ref[:, cols] + xs_sc[pad:pad + L, cols] * cw_ref[SSM_CONV - 1:SSM_CONV, cols]
        for j in range(1, SSM_CONV):
            acc = acc + xs_sc[pad - j:pad - j + L, cols] * cw_ref[SSM_CONV - 1 - j:SSM_CONV - j, cols]
        return _silu(acc)

    dtv = dt_ref[0] + dtb_ref[...]
    dt = jnp.maximum(dtv, 0.0) + jnp.log1p(jnp.exp(-jnp.abs(dtv)))
    a = dt * (-jnp.exp(alog_ref[...]))
    ri = lax.broadcasted_iota(jnp.int32, (L, L), 0)
    ci = lax.broadcasted_iota(jnp.int32, (L, L), 1)
    causal = ri >= ci
    a_cum = jnp.dot(causal.astype(F32), a, precision=HIGHEST, preferred_element_type=F32)
    a_cum_t = a_cum.T
    e_mat = e_ref[...]
    acum_x = _expand_heads(a_cum, e_mat)
    dt_x = _expand_heads(dt, e_mat)
    lane = lax.broadcasted_iota(jnp.int32, (L, LANES), 1)
    first_head = lane < SSM_HEAD_DIM

    for g in range(SSM_GROUPS):
        cols = slice(g * gw, (g + 1) * gw)
        xg = conv_silu(g * gw, gw)
        bg = conv_silu(d_inner + g * SSM_STATE, SSM_STATE)
        cg = conv_silu(d_inner + (SSM_GROUPS + g) * SSM_STATE, SSM_STATE)
        ac = acum_x[:, cols]
        last = ac[L - 1:L, :]
        xdt = xg * dt_x[:, cols]
        cb16 = cg.astype(BF16)
        bb16 = bg.astype(BF16)
        cbm = lax.dot_general(cb16, bb16, (((1,), (1,)), ((), ())), preferred_element_type=F32)
        st = state_sc[g]
        y_off = jnp.dot(cb16, st.astype(BF16), preferred_element_type=F32) * jnp.exp(ac)
        xd = (xdt * jnp.exp(last - ac)).astype(BF16)
        state_sc[g] = st * jnp.exp(last) + jnp.dot(bg.T.astype(BF16), xd, preferred_element_type=F32)
        xdt16 = xdt.astype(BF16)
        pieces = []
        for jp in range(hpg // 2):
            lhs = []
            for hh in (g * hpg + 2 * jp, g * hpg + 2 * jp + 1):
                seg = jnp.where(causal, a_cum[:, hh:hh + 1] - a_cum_t[hh:hh + 1, :], NEG)
                lhs.append((cbm * jnp.exp(seg)).astype(BF16))
            xp = xdt16[:, jp * LANES:(jp + 1) * LANES]
            zero = jnp.zeros_like(xp)
            rhs = jnp.concatenate([jnp.where(first_head, xp, zero), jnp.where(first_head, zero, xp)], axis=0)
            pieces.append(jnp.dot(jnp.concatenate(lhs, axis=1), rhs, preferred_element_type=F32))
        y = jnp.concatenate(pieces, axis=1) + y_off + dexp_ref[:, cols] * xg
        y = y * _silu(z_ref[0, :, cols])
        y = y * lax.rsqrt(jnp.mean(y * y, axis=1, keepdims=True) + RMS_EPS) * nw_ref[:, cols]
        o_ref[0, :, cols] = y.astype(o_ref.dtype)


def _ssd(z3, xbc3, dt3, conv_w, conv_b, dt_bias_p, a_log_p, d_exp, norm_w, e_mat):
    b, s, d_inner = z3.shape
    cdim = xbc3.shape[2]
    nc = s // SSM_CHUNK
    gw = d_inner // SSM_GROUPS
    rows_per_prev = SSM_CHUNK // SUBLANES
    full = lambda shape: pl.BlockSpec(shape, lambda bi, c: tuple(0 for _ in shape))
    return pl.pallas_call(
        functools.partial(_ssd_kernel, d_inner=d_inner),
        grid=(b, nc),
        in_specs=[pl.BlockSpec((1, SSM_CHUNK, d_inner), lambda bi, c: (bi, c, 0)),
                  pl.BlockSpec((1, SSM_CHUNK, cdim), lambda bi, c: (bi, c, 0)),
                  pl.BlockSpec((1, SUBLANES, cdim), lambda bi, c: (bi, jnp.maximum(c * rows_per_prev - 1, 0), 0)),
                  pl.BlockSpec((1, SSM_CHUNK, LANES), lambda bi, c: (bi, c, 0)),
                  full((SSM_CONV, cdim)), full((1, cdim)), full((1, LANES)), full((1, LANES)),
                  full((1, d_inner)), full((1, d_inner)), full((LANES, d_inner))],
        out_specs=pl.BlockSpec((1, SSM_CHUNK, d_inner), lambda bi, c: (bi, c, 0)),
        out_shape=jax.ShapeDtypeStruct((b, s, d_inner), BF16),
        scratch_shapes=[pltpu.VMEM((SSM_GROUPS, SSM_STATE, gw), F32),
                        pltpu.VMEM((SUBLANES + SSM_CHUNK, cdim), F32)],
        compiler_params=_params(("parallel", "arbitrary"), 56),
        name="ssd",
    )(z3, xbc3, xbc3, dt3, conv_w, conv_b, dt_bias_p, a_log_p, d_exp, norm_w, e_mat)


def kernel(x, positions, attn_w_qkv, attn_w_o, ssm_w_in, ssm_conv_w, ssm_conv_b, ssm_dt_bias, ssm_A_log, ssm_D, ssm_norm_w, ssm_w_out, router_w, router_bias, moe_w_gate, moe_w_up, moe_w_down, ln_mix_g, ln_mix_b, ln_ffn_g, ln_ffn_b):
    bsz, seq, d_model = x.shape
    t = bsz * seq
    depth = moe_w_gate.shape[0]
    alpha = (2 * depth) ** 0.25
    n_heads = d_model // HEAD_DIM
    patterns = ((128, 1), (512, 4), (2048, 16))
    attn_blk = 512

    half = ROT_DIM // 2
    inv_freq = ROPE_THETA ** (-jnp.arange(0, ROT_DIM, 2, dtype=F32) / ROT_DIM)
    ang = positions.astype(F32).reshape(t, 1) * inv_freq
    cos, sin = jnp.cos(ang), jnp.sin(ang)
    zeros_r = jnp.zeros((t, LANES - ROT_DIM), F32)
    cos_t = jnp.concatenate([cos, cos, jnp.ones((t, LANES - ROT_DIM), F32)], axis=1)
    sa_t = jnp.concatenate([-sin, jnp.zeros((t, half), F32), zeros_r], axis=1)
    sb_t = jnp.concatenate([jnp.zeros((t, half), F32), sin, zeros_r], axis=1)
    bias = _branch_count_bias(seq, attn_blk, patterns)

    router_wt = router_w.T
    bias_col = router_bias.reshape(N_EXPERTS, 1).astype(F32)

    h = x.reshape(t, d_model)
    hb = h.astype(BF16)
    for i in range(depth):
        j = i // 2
        if i % 2 == 0:
            qkv = _qkv_rope(hb, attn_w_qkv[j].astype(BF16), cos_t, sa_t, sb_t, d_model)
            att = _attention(qkv.reshape(bsz, seq, 3 * d_model), bias, n_heads, attn_blk)
            h, hb = _mm_res_ln(att.reshape(t, d_model), attn_w_o[j].astype(BF16), h, ln_mix_g[i], ln_mix_b[i], alpha)
        else:
            d_inner = ssm_norm_w.shape[1]
            n_ssm_heads = ssm_dt_bias.shape[1]
            cdim = ssm_conv_w.shape[2]
            w_in = ssm_w_in[j]
            z = _mm(hb, w_in[:, :d_inner].astype(BF16), F32)
            xbc = _mm(hb, w_in[:, d_inner:d_inner + cdim].astype(BF16), F32)
            w_dt = jnp.pad(w_in[:, d_inner + cdim:], ((0, 0), (0, LANES - n_ssm_heads))).astype(BF16)
            dt = _mm(hb, w_dt, F32)
            hpad = (0, LANES - n_ssm_heads)
            head_of_ch = jnp.arange(d_inner, dtype=jnp.int32) // SSM_HEAD_DIM
            e_mat = (jnp.arange(LANES, dtype=jnp.int32)[:, None] == head_of_ch[None, :]).astype(BF16)
            y = _ssd(z.reshape(bsz, seq, d_inner), xbc.reshape(bsz, seq, cdim), dt.reshape(bsz, seq, LANES),
                     ssm_conv_w[j], ssm_conv_b[j].reshape(1, cdim),
                     jnp.pad(ssm_dt_bias[j], hpad).reshape(1, LANES), jnp.pad(ssm_A_log[j], hpad).reshape(1, LANES),
                     jnp.repeat(ssm_D[j], SSM_HEAD_DIM).reshape(1, d_inner), ssm_norm_w[j].reshape(1, d_inner), e_mat)
            h, hb = _mm_res_ln(y.reshape(t, d_inner), ssm_w_out[j].astype(BF16), h, ln_mix_g[i], ln_mix_b[i], alpha)
        h, hb = _moe_layer(h, router_wt, bias_col, moe_w_gate, moe_w_up, moe_w_down, i,
                           ln_ffn_g[i], ln_ffn_b[i], alpha)
    return h.reshape(bsz, seq, d_model)
```

```python
import functools
import math

import jax
import jax.numpy as jnp
from jax import lax
from jax.experimental import pallas as pl
from jax.experimental.pallas import tpu as pltpu

F32 = jnp.float32
BF16 = jnp.bfloat16
HIGHEST = lax.Precision.HIGHEST

LANES = 128
SUBLANES = 8
MXU_COLS = 256
MIB = 1024 * 1024

HEAD_DIM = 128
ROT_DIM = HEAD_DIM // 4
ROPE_THETA = 500000.0
N_EXPERTS = 64
N_GROUPS = 8
EPG = N_EXPERTS // N_GROUPS
TOP_K = 2
MOE_BLOCK = 128
SSM_HEAD_DIM = 64
SSM_GROUPS = 8
SSM_STATE = 128
SSM_CONV = 4
SSM_CHUNK = 128
LN_EPS = 1e-5
RMS_EPS = 1e-5
NEG = -1e30


def _params(sem, vmem_mib):
    return pltpu.CompilerParams(dimension_semantics=sem, vmem_limit_bytes=vmem_mib * MIB)


def _silu(x):
    return x / (1.0 + jnp.exp(-x))


def _layer_norm_rows(y, g, b):
    mu = jnp.mean(y, axis=-1, keepdims=True)
    yc = y - mu
    var = jnp.mean(yc * yc, axis=-1, keepdims=True)
    return yc * lax.rsqrt(var + LN_EPS) * g + b


def _mm_kernel(a_ref, w_ref, o_ref, wb_sc):
    @pl.when(pl.program_id(1) == 0)
    def _():
        wb_sc[...] = w_ref[...].astype(BF16)

    o_ref[...] = jnp.dot(a_ref[...], wb_sc[...], preferred_element_type=F32).astype(o_ref.dtype)


def _mm(a, w, col0, n, out_dtype, tm=1024, tn=512):
    m, k = a.shape
    tn = min(tn, n)
    assert col0 % tn == 0 and n % tn == 0 and m % tm == 0
    c0 = col0 // tn
    return pl.pallas_call(
        _mm_kernel,
        grid=(n // tn, m // tm),
        in_specs=[pl.BlockSpec((tm, k), lambda j, i: (i, 0)),
                  pl.BlockSpec((k, tn), lambda j, i: (0, c0 + j))],
        out_specs=pl.BlockSpec((tm, tn), lambda j, i: (i, j)),
        out_shape=jax.ShapeDtypeStruct((m, n), out_dtype),
        scratch_shapes=[pltpu.VMEM((k, tn), BF16)],
        compiler_params=_params(("arbitrary", "arbitrary"), 48),
        name="mm",
    )(a, w)


def _qkv_rope_kernel(x_ref, w_ref, cos_ref, sa_ref, sb_ref, o_ref, wb_sc, *, n_rope_tiles, n_q_tiles, scale):
    n = pl.program_id(0)

    @pl.when(pl.program_id(1) == 0)
    def _():
        wb_sc[...] = w_ref[...].astype(BF16)

    tn = wb_sc.shape[1]

    @pl.when(n < n_rope_tiles)
    def _():
        c = cos_ref[...]
        sa = sa_ref[...]
        sb = sb_ref[...]
        sc = jnp.where(n < n_q_tiles, scale, 1.0).astype(F32)
        for p in range(tn // MXU_COLS):
            t2 = jnp.dot(x_ref[...], wb_sc[:, p * MXU_COLS:(p + 1) * MXU_COLS], preferred_element_type=F32)
            for j in range(MXU_COLS // LANES):
                t = t2[:, j * LANES:(j + 1) * LANES]
                r = t * c + pltpu.roll(t, LANES - ROT_DIM // 2, 1) * sa + pltpu.roll(t, ROT_DIM // 2, 1) * sb
                col = p * MXU_COLS + j * LANES
                o_ref[:, col:col + LANES] = (r * sc).astype(o_ref.dtype)

    @pl.when(n >= n_rope_tiles)
    def _():
        o_ref[...] = jnp.dot(x_ref[...], wb_sc[...], preferred_element_type=F32).astype(o_ref.dtype)


def _qkv_rope(xb, w, cos_t, sa_t, sb_t, d_model, tm=1024, tn=512):
    m, k = xb.shape
    n = w.shape[1]
    kern = functools.partial(_qkv_rope_kernel, n_rope_tiles=2 * d_model // tn, n_q_tiles=d_model // tn,
                             scale=HEAD_DIM ** -0.5)
    tab = pl.BlockSpec((tm, LANES), lambda j, i: (i, 0))
    return pl.pallas_call(
        kern,
        grid=(n // tn, m // tm),
        in_specs=[pl.BlockSpec((tm, k), lambda j, i: (i, 0)),
                  pl.BlockSpec((k, tn), lambda j, i: (0, j)),
                  tab, tab, tab],
        out_specs=pl.BlockSpec((tm, tn), lambda j, i: (i, j)),
        out_shape=jax.ShapeDtypeStruct((m, n), BF16),
        scratch_shapes=[pltpu.VMEM((k, tn), BF16)],
        compiler_params=_params(("arbitrary", "arbitrary"), 48),
        name="qkv_rope",
    )(xb, w, cos_t, sa_t, sb_t)


def _attn_kernel(qi_ref, ki_ref, q_ref, k_ref, v_ref, bias_ref, o_ref, m_sc, l_sc, acc_sc, *, heads):
    p = pl.program_id(2)
    qi = qi_ref[p]
    ki = ki_ref[p]
    blk = q_ref.shape[1]

    @pl.when(ki == 0)
    def _():
        m_sc[...] = jnp.full_like(m_sc, 0.1 * NEG)
        l_sc[...] = jnp.zeros_like(l_sc)
        acc_sc[...] = jnp.zeros_like(acc_sc)

    bias = bias_ref[qi - ki]
    for j in range(heads):
        cols = slice(j * HEAD_DIM, (j + 1) * HEAD_DIM)
        s = lax.dot_general(q_ref[0, :, cols], k_ref[0, :, cols], (((1,), (1,)), ((), ())),
                            preferred_element_type=F32) + bias
        m_prev = m_sc[j]
        m_new = jnp.maximum(m_prev, jnp.max(s, axis=1, keepdims=True))
        alpha = jnp.exp(m_prev - m_new)
        pr = jnp.exp(s - jnp.concatenate([m_new] * (blk // LANES), axis=1))
        l_sc[j] = alpha * l_sc[j] + jnp.sum(pr, axis=1, keepdims=True)
        acc_sc[j] = alpha * acc_sc[j] + jnp.dot(pr.astype(BF16), v_ref[0, :, cols], preferred_element_type=F32)
        m_sc[j] = m_new

    @pl.when(ki == qi)
    def _():
        for j in range(heads):
            o_ref[0, :, j * HEAD_DIM:(j + 1) * HEAD_DIM] = (acc_sc[j] / l_sc[j]).astype(o_ref.dtype)


def _attention(qkv3, bias, n_heads, blk, heads=4):
    b, s, _ = qkv3.shape
    nq = s // blk
    pairs = [(qi, ki) for qi in range(nq) for ki in range(qi + 1)]
    qi_tab = jnp.asarray([pq for pq, _ in pairs], jnp.int32)
    ki_tab = jnp.asarray([pk for _, pk in pairs], jnp.int32)
    hg = n_heads // heads
    w = heads * HEAD_DIM
    return pl.pallas_call(
        functools.partial(_attn_kernel, heads=heads),
        grid_spec=pltpu.PrefetchScalarGridSpec(
            num_scalar_prefetch=2,
            grid=(b, hg, len(pairs)),
            in_specs=[pl.BlockSpec((1, blk, w), lambda bi, h, p, qt, kt: (bi, qt[p], h)),
                      pl.BlockSpec((1, blk, w), lambda bi, h, p, qt, kt: (bi, kt[p], hg + h)),
                      pl.BlockSpec((1, blk, w), lambda bi, h, p, qt, kt: (bi, kt[p], 2 * hg + h)),
                      pl.BlockSpec((nq, blk, blk), lambda bi, h, p, qt, kt: (0, 0, 0))],
            out_specs=pl.BlockSpec((1, blk, w), lambda bi, h, p, qt, kt: (bi, qt[p], h)),
            scratch_shapes=[pltpu.VMEM((heads, blk, LANES), F32), pltpu.VMEM((heads, blk, LANES), F32),
                            pltpu.VMEM((heads, blk, HEAD_DIM), F32)],
        ),
        out_shape=jax.ShapeDtypeStruct((b, s, n_heads * HEAD_DIM), BF16),
        compiler_params=_params(("parallel", "parallel", "arbitrary"), 48),
        name="attention",
    )(qi_tab, ki_tab, qkv3, qkv3, qkv3, bias)


def _branch_count_bias(s, blk, patterns):
    nq = s // blk
    off = jnp.arange(nq, dtype=jnp.int32)[:, None, None] * blk
    d = off + jnp.arange(blk, dtype=jnp.int32)[None, :, None] - jnp.arange(blk, dtype=jnp.int32)[None, None, :]
    cnt = jnp.zeros(d.shape, F32)
    for window, dilation in patterns:
        cnt = cnt + ((d >= 0) & (d % dilation == 0) & (d <= window)).astype(F32)
    return jnp.where(cnt > 0, jnp.log(jnp.maximum(cnt, 1.0)), NEG)


def _mm_res_ln_kernel(a_ref, w_ref, h_ref, g_ref, b_ref, of_ref, ob_ref, acc_sc, *, nk, alpha):
    k = pl.program_id(1)

    @pl.when(k == 0)
    def _():
        acc_sc[...] = jnp.zeros_like(acc_sc)

    acc_sc[...] += jnp.dot(a_ref[...], w_ref[...], preferred_element_type=F32)

    @pl.when(k == nk - 1)
    def _():
        out = _layer_norm_rows(alpha * h_ref[...] + acc_sc[...], g_ref[...], b_ref[...])
        of_ref[...] = out
        ob_ref[...] = out.astype(BF16)


def _mm_res_ln(a, w, h, g, b, alpha, tm=512, tk=1024):
    m, kdim = a.shape
    n = w.shape[1]
    nk = kdim // tk
    kern = functools.partial(_mm_res_ln_kernel, nk=nk, alpha=alpha)
    row = pl.BlockSpec((tm, n), lambda i, k: (i, 0))
    vec = pl.BlockSpec((1, n), lambda i, k: (0, 0))
    return pl.pallas_call(
        kern,
        grid=(m // tm, nk),
        in_specs=[pl.BlockSpec((tm, tk), lambda i, k: (i, k)),
                  pl.BlockSpec((tk, n), lambda i, k: (k, 0)),
                  row, vec, vec],
        out_specs=[row, row],
        out_shape=[jax.ShapeDtypeStruct((m, n), F32), jax.ShapeDtypeStruct((m, n), BF16)],
        scratch_shapes=[pltpu.VMEM((tm, n), F32)],
        compiler_params=_params(("parallel", "arbitrary"), 48),
        name="mm_res_ln",
    )(a, w, h, g.reshape(1, n), b.reshape(1, n))


def _router_kernel(h_ref, wt_ref, b_ref, e_ref, g_ref):
    logits = lax.dot_general(wt_ref[...], h_ref[...], (((1,), (1,)), ((), ())),
                             precision=HIGHEST, preferred_element_type=F32)
    mx = jnp.max(logits, axis=0, keepdims=True)
    ex = jnp.exp(logits - mx)
    scores = ex / jnp.sum(ex, axis=0, keepdims=True)
    sel = scores + b_ref[...]
    tm = sel.shape[1]
    li = lax.broadcasted_iota(jnp.int32, (EPG, tm), 0)

    def first_argmax(v):
        m = jnp.max(v, axis=0, keepdims=True)
        return m, jnp.min(jnp.where(v == m, li, EPG), axis=0, keepdims=True)

    gs, i1s, i2s = [], [], []
    for g in range(N_GROUPS):
        slab = sel[g * EPG:(g + 1) * EPG, :]
        m1, i1 = first_argmax(slab)
        m2, i2 = first_argmax(jnp.where(li == i1, NEG, slab))
        gs.append(m1 + m2)
        i1s.append(i1)
        i2s.append(i2)
    best, gidx = gs[0], jnp.zeros((1, tm), jnp.int32)
    for g in range(1, N_GROUPS):
        better = gs[g] > best
        best = jnp.where(better, gs[g], best)
        gidx = jnp.where(better, g, gidx)
    l1 = jnp.zeros((1, tm), jnp.int32)
    l2 = jnp.zeros((1, tm), jnp.int32)
    s1 = jnp.zeros((1, tm), F32)
    s2 = jnp.zeros((1, tm), F32)
    for g in range(N_GROUPS):
        here = gidx == g
        slab = scores[g * EPG:(g + 1) * EPG, :]
        l1 = jnp.where(here, i1s[g], l1)
        l2 = jnp.where(here, i2s[g], l2)
        s1 = jnp.where(here, jnp.sum(jnp.where(li == i1s[g], slab, 0.0), axis=0, keepdims=True), s1)
        s2 = jnp.where(here, jnp.sum(jnp.where(li == i2s[g], slab, 0.0), axis=0, keepdims=True), s2)
    tot = s1 + s2
    e_ref[0:1, :] = gidx * EPG + l1
    e_ref[1:2, :] = gidx * EPG + l2
    g_ref[0:1, :] = s1 / tot
    g_ref[1:2, :] = s2 / tot


def _router(h, router_wt, bias_col, tm=512):
    t, d = h.shape
    return pl.pallas_call(
        _router_kernel,
        grid=(t // tm,),
        in_specs=[pl.BlockSpec((tm, d), lambda i: (i, 0)),
                  pl.BlockSpec((N_EXPERTS, d), lambda i: (0, 0)),
                  pl.BlockSpec((N_EXPERTS, 1), lambda i: (0, 0))],
        out_specs=[pl.BlockSpec((TOP_K, tm), lambda i: (0, i)), pl.BlockSpec((TOP_K, tm), lambda i: (0, i))],
        out_shape=[jax.ShapeDtypeStruct((TOP_K, t), jnp.int32), jax.ShapeDtypeStruct((TOP_K, t), F32)],
        compiler_params=_params(("parallel",), 32),
        name="router",
    )(h, router_wt, bias_col)


N_ROW_BUFS = 2
N_SPARE_ROWS = N_ROW_BUFS * MOE_BLOCK
N_WEIGHT_SETS = 2
WEIGHT_DMA_PRIORITY = 1


def _experts_kernel(be_ref, wset_ref, ahead_ref, second_ref, nused_ref, src_ref, dst_ref,
                    h_hbm, wg_hbm, wu_hbm, wd_hbm, ys_hbm,
                    xb0, xb1, ob0, ob1, wgl, wul, wdl, wgb, wub, wdb, gsem, ssem, wsem, *, layer, n_slots):
    i = pl.program_id(0)
    n_used = nused_ref[0]
    xbufs = (xb0, xb1)
    obufs = (ob0, ob1)

    def gather_start(blk, s):
        base = (blk + 1) * MOE_BLOCK
        for r in range(MOE_BLOCK):
            pltpu.make_async_copy(h_hbm.at[pl.ds(src_ref[base + r], 1), :],
                                  xbufs[s].at[pl.ds(r, 1), :], gsem.at[s]).start()

    def scatter_start(blk, s):
        base = (blk + 1) * MOE_BLOCK
        for r in range(MOE_BLOCK):
            pltpu.make_async_copy(obufs[s].at[pl.ds(r, 1), :],
                                  ys_hbm.at[pl.ds(dst_ref[base + r], 1), :], ssem.at[s]).start()

    def gather_wait(s):
        pltpu.make_async_copy(h_hbm.at[pl.ds(0, MOE_BLOCK), :], xbufs[s], gsem.at[s]).wait()

    def scatter_wait(s):
        pltpu.make_async_copy(obufs[s], ys_hbm.at[pl.ds(0, MOE_BLOCK), :], ssem.at[s]).wait()

    w_hbm = (wg_hbm, wu_hbm, wd_hbm)
    w_land = (wgl, wul, wdl)
    w_work = (wgb, wub, wdb)

    def weight_copy(e, p, k):
        return pltpu.make_async_copy(w_hbm[k].at[layer, e], w_land[k].at[p], wsem.at[p, k])

    @pl.when(i == 0)
    def _():
        ob1[...] = jnp.zeros_like(ob1)
        spare = pltpu.make_async_copy(ob1, ys_hbm.at[pl.ds(n_slots + MOE_BLOCK, MOE_BLOCK), :], ssem.at[1])
        spare.start()
        spare.wait()
        gather_start(0, 0)
        for k in range(len(w_hbm)):
            weight_copy(be_ref[0], 0, k).start(priority=WEIGHT_DMA_PRIORITY)

        @pl.when(second_ref[0] >= 0)
        def _():
            for k in range(len(w_hbm)):
                weight_copy(second_ref[0], 1, k).start(priority=WEIGHT_DMA_PRIORITY)

    @pl.when(i < n_used)
    def _():
        for p in range(N_WEIGHT_SETS):
            @pl.when(wset_ref[i] == p)
            def _():
                ahead = ahead_ref[i]
                for k in range(len(w_hbm)):
                    weight_copy(be_ref[i], p, k).wait()
                    w_work[k][...] = w_land[k][p].astype(BF16)

                    @pl.when(ahead >= 0)
                    def _():
                        weight_copy(ahead, p, k).start(priority=WEIGHT_DMA_PRIORITY)

        for s in range(N_ROW_BUFS):
            @pl.when(i % N_ROW_BUFS == s)
            def _():
                gather_wait(s)

                @pl.when(i >= 1)
                def _():
                    scatter_wait(s)

                x = xbufs[s][...].astype(BF16)
                hg = jnp.dot(x, wgb[...], preferred_element_type=F32)
                hu = jnp.dot(x, wub[...], preferred_element_type=F32)
                act = (_silu(hg) * hu).astype(BF16)
                obufs[s][...] = jnp.dot(act, wdb[...], preferred_element_type=F32)
                gather_start(i + 1, 1 - s)
                scatter_start(i - 1, 1 - s)

                @pl.when(i == n_used - 1)
                def _():
                    gather_wait(1 - s)
                    scatter_wait(1 - s)
                    scatter_start(i, s)
                    scatter_wait(s)


def _experts(meta, h, w_gate, w_up, w_down, layer, n_slots):
    t, d = h.shape
    de = w_gate.shape[3]
    nb = meta[0].shape[0]
    any_spec = pl.BlockSpec(memory_space=pl.ANY)
    rowbuf = pltpu.VMEM((MOE_BLOCK, d), F32)
    return pl.pallas_call(
        functools.partial(_experts_kernel, layer=layer, n_slots=n_slots),
        grid_spec=pltpu.PrefetchScalarGridSpec(
            num_scalar_prefetch=len(meta),
            grid=(nb,),
            in_specs=[any_spec, any_spec, any_spec, any_spec],
            out_specs=any_spec,
            scratch_shapes=[rowbuf, rowbuf, rowbuf, rowbuf,
                            pltpu.VMEM((N_WEIGHT_SETS, d, de), F32), pltpu.VMEM((N_WEIGHT_SETS, d, de), F32),
                            pltpu.VMEM((N_WEIGHT_SETS, de, d), F32),
                            pltpu.VMEM((d, de), BF16), pltpu.VMEM((d, de), BF16), pltpu.VMEM((de, d), BF16),
                            pltpu.SemaphoreType.DMA((N_ROW_BUFS,)), pltpu.SemaphoreType.DMA((N_ROW_BUFS,)),
                            pltpu.SemaphoreType.DMA((N_WEIGHT_SETS, 3))],
        ),
        out_shape=jax.ShapeDtypeStruct((n_slots + N_SPARE_ROWS, d), F32),
        compiler_params=_params(("arbitrary",), 56),
        name="experts",
    )(*meta, h, w_gate, w_up, w_down)


def _combine_ln_kernel(y0_ref, y1_ref, gate_ref, h_ref, g_ref, b_ref, of_ref, ob_ref, *, alpha):
    gt = gate_ref[...]
    moe = gt[:, 0:1] * y0_ref[...] + gt[:, 1:2] * y1_ref[...]
    out = _layer_norm_rows(alpha * h_ref[...] + moe, g_ref[...], b_ref[...])
    of_ref[...] = out
    ob_ref[...] = out.astype(BF16)


def _combine_ln(ys, gate_t, h, g, b, alpha, tm=256):
    t, d = h.shape
    row = pl.BlockSpec((tm, d), lambda i: (i, 0))
    vec = pl.BlockSpec((1, d), lambda i: (0, 0))
    return pl.pallas_call(
        functools.partial(_combine_ln_kernel, alpha=alpha),
        grid=(t // tm,),
        in_specs=[row, pl.BlockSpec((tm, d), lambda i: (t // tm + i, 0)),
                  pl.BlockSpec((tm, TOP_K), lambda i: (i, 0)), row, vec, vec],
        out_specs=[row, row],
        out_shape=[jax.ShapeDtypeStruct((t, d), F32), jax.ShapeDtypeStruct((t, d), BF16)],
        compiler_params=_params(("parallel",), 40),
        name="combine_ln",
    )(ys, ys, gate_t, h, g.reshape(1, d), b.reshape(1, d))


def _moe_layer(h, router_wt, bias_col, w_gate, w_up, w_down, layer, ln_g, ln_b, alpha):
    t, d = h.shape
    e_idx, gate = _router(h, router_wt, bias_col)
    n_slots = t * TOP_K
    e_flat = e_idx.reshape(n_slots)
    experts = jnp.arange(N_EXPERTS, dtype=jnp.int32)
    onehot = (e_flat[:, None] == experts[None, :]).astype(jnp.int32)
    csum = jnp.cumsum(onehot, axis=0)
    counts = csum[-1]
    padded = ((counts + MOE_BLOCK - 1) // MOE_BLOCK) * MOE_BLOCK
    pad_end = jnp.cumsum(padded)
    pad_start = pad_end - padded
    dest = jnp.sum(onehot * (csum - 1 + pad_start[None, :]), axis=1).astype(jnp.int32)
    n_blocks = n_slots // MOE_BLOCK + N_EXPERTS
    rows = n_blocks * MOE_BLOCK
    row_slot = jnp.full((rows,), -1, jnp.int32).at[dest].set(jnp.arange(n_slots, dtype=jnp.int32),
                                                             unique_indices=True)
    row_id = jnp.arange(-MOE_BLOCK, rows, dtype=jnp.int32)
    row_slot = jnp.concatenate([jnp.full((MOE_BLOCK,), -1, jnp.int32), row_slot])
    valid = row_slot >= 0
    row_src = jnp.where(valid, row_slot % t, row_id % t)
    spare = n_slots + ((row_id // MOE_BLOCK + 1) % N_ROW_BUFS) * MOE_BLOCK + row_id % MOE_BLOCK
    row_dst = jnp.where(valid, row_slot, spare)
    blk_row0 = jnp.arange(n_blocks, dtype=jnp.int32) * MOE_BLOCK
    block_e = jnp.clip(jnp.searchsorted(pad_end, blk_row0, side='right'), 0, N_EXPERTS - 1).astype(jnp.int32)
    first = jnp.concatenate([jnp.ones((1,), bool), block_e[1:] != block_e[:-1]])
    seq = jnp.concatenate([jnp.sort(jnp.where(counts > 0, experts, N_EXPERTS)),
                           jnp.full((N_WEIGHT_SETS,), N_EXPERTS, jnp.int32)])
    place = jnp.cumsum((counts > 0).astype(jnp.int32)) - 1
    blk_place = place[block_e]
    wset = jnp.where(first, blk_place % N_WEIGHT_SETS, -1).astype(jnp.int32)
    ahead = seq[blk_place + N_WEIGHT_SETS]
    ahead = jnp.where(ahead < N_EXPERTS, ahead, -1).astype(jnp.int32)
    second = jnp.where(seq[1] < N_EXPERTS, seq[1], -1).astype(jnp.int32).reshape(1)
    n_used = (pad_end[-1] // MOE_BLOCK).astype(jnp.int32).reshape(1)
    meta = (block_e, wset, ahead, second, n_used, row_src.astype(jnp.int32), row_dst.astype(jnp.int32))
    ys = _experts(meta, h, w_gate, w_up, w_down, layer, n_slots)
    return _combine_ln(ys, gate.T, h, ln_g, ln_b, alpha)


def _expand_heads(x, e_bf16):
    hi = x.astype(BF16)
    r1 = x - hi.astype(F32)
    mid = r1.astype(BF16)
    lo = (r1 - mid.astype(F32)).astype(BF16)
    dot = lambda a: jnp.dot(a, e_bf16, preferred_element_type=F32)
    return dot(hi) + dot(mid) + dot(lo)


def _ssd_kernel(z_ref, xbc_ref, prev_ref, dt_ref, cw_ref, cb_ref, dtb_ref, alog_ref, dexp_ref, nw_ref, e_ref,
                o_ref, state_sc, xs_sc, *, d_inner):
    c = pl.program_id(1)
    L = SSM_CHUNK
    gw = d_inner // SSM_GROUPS
    hpg = gw // SSM_HEAD_DIM
    pad = SUBLANES

    @pl.when(c == 0)
    def _():
        state_sc[...] = jnp.zeros_like(state_sc)

    xs_sc[0:pad, :] = jnp.where(c > 0, prev_ref[0], 0.0)
    xs_sc[pad:pad + L, :] = xbc_ref[0]

    def conv_silu(col0, width):
        cols = slice(col0, col0 + width)
        acc = cb_ref[:, cols] + xs_sc[pad:pad + L, cols] * cw_ref[SSM_CONV - 1:SSM_CONV, cols]
        for j in range(1, SSM_CONV):
            acc = acc + xs_sc[pad - j:pad - j + L, cols] * cw_ref[SSM_CONV - 1 - j:SSM_CONV - j, cols]
        return _silu(acc)

    dtv = dt_ref[0] + dtb_ref[...]
    dt = jnp.maximum(dtv, 0.0) + jnp.log1p(jnp.exp(-jnp.abs(dtv)))
    a = dt * (-jnp.exp(alog_ref[...]))
    ri = lax.broadcasted_iota(jnp.int32, (L, L), 0)
    ci = lax.broadcasted_iota(jnp.int32, (L, L), 1)
    causal = ri >= ci
    a_cum = jnp.dot(causal.astype(F32), a, precision=HIGHEST, preferred_element_type=F32)
    a_cum_t = a_cum.T
    e_mat = e_ref[...]
    acum_x = _expand_heads(a_cum, e_mat)
    dt_x = _expand_heads(dt, e_mat)
    lane = lax.broadcasted_iota(jnp.int32, (L, LANES), 1)
    first_head = lane < SSM_HEAD_DIM

    for g in range(SSM_GROUPS):
        cols = slice(g * gw, (g + 1) * gw)
        xg = conv_silu(g * gw, gw)
        bg = conv_silu(d_inner + g * SSM_STATE, SSM_STATE)
        cg = conv_silu(d_inner + (SSM_GROUPS + g) * SSM_STATE, SSM_STATE)
        ac = acum_x[:, cols]
        last = ac[L - 1:L, :]
        xdt = xg * dt_x[:, cols]
        cb16 = cg.astype(BF16)
        bb16 = bg.astype(BF16)
        cbm = lax.dot_general(cb16, bb16, (((1,), (1,)), ((), ())), preferred_element_type=F32)
        st = state_sc[g]
        y_off = jnp.dot(cb16, st.astype(BF16), preferred_element_type=F32) * jnp.exp(ac)
        xd = (xdt * jnp.exp(last - ac)).astype(BF16)
        state_sc[g] = st * jnp.exp(last) + jnp.dot(bg.T.astype(BF16), xd, preferred_element_type=F32)
        xdt16 = xdt.astype(BF16)
        pieces = []
        for jp in range(hpg // 2):
            lhs = []
            for hh in (g * hpg + 2 * jp, g * hpg + 2 * jp + 1):
                seg = jnp.where(causal, a_cum[:, hh:hh + 1] - a_cum_t[hh:hh + 1, :], NEG)
                lhs.append((cbm * jnp.exp(seg)).astype(BF16))
            xp = xdt16[:, jp * LANES:(jp + 1) * LANES]
            zero = jnp.zeros_like(xp)
            rhs = jnp.concatenate([jnp.where(first_head, xp, zero), jnp.where(first_head, zero, xp)], axis=0)
            pieces.append(jnp.dot(jnp.concatenate(lhs, axis=1), rhs, preferred_element_type=F32))
        y = jnp.concatenate(pieces, axis=1) + y_off + dexp_ref[:, cols] * xg
        y = y * _silu(z_ref[0, :, cols])
        y = y * lax.rsqrt(jnp.mean(y * y, axis=1, keepdims=True) + RMS_EPS) * nw_ref[:, cols]
        o_ref[0, :, cols] = y.astype(o_ref.dtype)


def _ssd(z3, xbc3, dt3, conv_w, conv_b, dt_bias_p, a_log_p, d_exp, norm_w, e_mat):
    b, s, d_inner = z3.shape
    cdim = xbc3.shape[2]
    nc = s // SSM_CHUNK
    gw = d_inner // SSM_GROUPS
    rows_per_prev = SSM_CHUNK // SUBLANES
    full = lambda shape: pl.BlockSpec(shape, lambda bi, c: tuple(0 for _ in shape))
    return pl.pallas_call(
        functools.partial(_ssd_kernel, d_inner=d_inner),
        grid=(b, nc),
        in_specs=[pl.BlockSpec((1, SSM_CHUNK, d_inner), lambda bi, c: (bi, c, 0)),
                  pl.BlockSpec((1, SSM_CHUNK, cdim), lambda bi, c: (bi, c, 0)),
                  pl.BlockSpec((1, SUBLANES, cdim), lambda bi, c: (bi, jnp.maximum(c * rows_per_prev - 1, 0), 0)),
                  pl.BlockSpec((1, SSM_CHUNK, LANES), lambda bi, c: (bi, c, 0)),
                  full((SSM_CONV, cdim)), full((1, cdim)), full((1, LANES)), full((1, LANES)),
                  full((1, d_inner)), full((1, d_inner)), full((LANES, d_inner))],
        out_specs=pl.BlockSpec((1, SSM_CHUNK, d_inner), lambda bi, c: (bi, c, 0)),
        out_shape=jax.ShapeDtypeStruct((b, s, d_inner), BF16),
        scratch_shapes=[pltpu.VMEM((SSM_GROUPS, SSM_STATE, gw), F32),
                        pltpu.VMEM((SUBLANES + SSM_CHUNK, cdim), F32)],
        compiler_params=_params(("parallel", "arbitrary"), 56),
        name="ssd",
    )(z3, xbc3, xbc3, dt3, conv_w, conv_b, dt_bias_p, a_log_p, d_exp, norm_w, e_mat)


def kernel(x, positions, attn_w_qkv, attn_w_o, ssm_w_in, ssm_conv_w, ssm_conv_b, ssm_dt_bias, ssm_A_log, ssm_D, ssm_norm_w, ssm_w_out, router_w, router_bias, moe_w_gate, moe_w_up, moe_w_down, ln_mix_g, ln_mix_b, ln_ffn_g, ln_ffn_b):
    bsz, seq, d_model = x.shape
    t = bsz * seq
    depth = moe_w_gate.shape[0]
    alpha = (2 * depth) ** 0.25
    n_heads = d_model // HEAD_DIM
    patterns = ((128, 1), (512, 4), (2048, 16))
    attn_blk = 512

    half = ROT_DIM // 2
    inv_freq = ROPE_THETA ** (-jnp.arange(0, ROT_DIM, 2, dtype=F32) / ROT_DIM)
    ang = positions.astype(F32).reshape(t, 1) * inv_freq
    cos, sin = jnp.cos(ang), jnp.sin(ang)
    zeros_r = jnp.zeros((t, LANES - ROT_DIM), F32)
    cos_t = jnp.concatenate([cos, cos, jnp.ones((t, LANES - ROT_DIM), F32)], axis=1)
    sa_t = jnp.concatenate([-sin, jnp.zeros((t, half), F32), zeros_r], axis=1)
    sb_t = jnp.concatenate([jnp.zeros((t, half), F32), sin, zeros_r], axis=1)
    bias = _branch_count_bias(seq, attn_blk, patterns)

    router_wt = router_w.T
    bias_col = router_bias.reshape(N_EXPERTS, 1).astype(F32)

    h = x.reshape(t, d_model)
    hb = h.astype(BF16)
    for i in range(depth):
        j = i // 2
        if i % 2 == 0:
            qkv = _qkv_rope(hb, attn_w_qkv[j], cos_t, sa_t, sb_t, d_model)
            att = _attention(qkv.reshape(bsz, seq, 3 * d_model), bias, n_heads, attn_blk)
            h, hb = _mm_res_ln(att.reshape(t, d_model), attn_w_o[j].astype(BF16), h, ln_mix_g[i], ln_mix_b[i], alpha)
        else:
            d_inner = ssm_norm_w.shape[1]
            n_ssm_heads = ssm_dt_bias.shape[1]
            cdim = ssm_conv_w.shape[2]
            w_in = ssm_w_in[j]
            z = _mm(hb, w_in, 0, d_inner, F32)
            xbc = _mm(hb, w_in, d_inner, cdim, F32)
            w_dt = jnp.pad(w_in[:, d_inner + cdim:], ((0, 0), (0, LANES - n_ssm_heads)))
            dt = _mm(hb, w_dt, 0, LANES, F32)
            hpad = (0, LANES - n_ssm_heads)
            head_of_ch = jnp.arange(d_inner, dtype=jnp.int32) // SSM_HEAD_DIM
            e_mat = (jnp.arange(LANES, dtype=jnp.int32)[:, None] == head_of_ch[None, :]).astype(BF16)
            y = _ssd(z.reshape(bsz, seq, d_inner), xbc.reshape(bsz, seq, cdim), dt.reshape(bsz, seq, LANES),
                     ssm_conv_w[j], ssm_conv_b[j].reshape(1, cdim),
                     jnp.pad(ssm_dt_bias[j], hpad).reshape(1, LANES), jnp.pad(ssm_A_log[j], hpad).reshape(1, LANES),
                     jnp.repeat(ssm_D[j], SSM_HEAD_DIM).reshape(1, d_inner), ssm_norm_w[j].reshape(1, d_inner), e_mat)
            h, hb = _mm_res_ln(y.reshape(t, d_inner), ssm_w_out[j].astype(BF16), h, ln_mix_g[i], ln_mix_b[i], alpha)
        h, hb = _moe_layer(h, router_wt, bias_col, moe_w_gate, moe_w_up, moe_w_down, i,
                           ln_ffn_g[i], ln_ffn_b[i], alpha)
    return h.reshape(bsz, seq, d_model)
```

```python
import functools
import math

import jax
import jax.numpy as jnp
from jax import lax
from jax.experimental import pallas as pl
from jax.experimental.pallas import tpu as pltpu

F32 = jnp.float32
BF16 = jnp.bfloat16
HIGHEST = lax.Precision.HIGHEST

LANES = 128
SUBLANES = 8
MXU_COLS = 256
MIB = 1024 * 1024

HEAD_DIM = 128
ROT_DIM = HEAD_DIM // 4
ROPE_THETA = 500000.0
N_EXPERTS = 64
N_GROUPS = 8
EPG = N_EXPERTS // N_GROUPS
TOP_K = 2
MOE_BLOCK = 128
SSM_HEAD_DIM = 64
SSM_GROUPS = 8
SSM_STATE = 128
SSM_CONV = 4
SSM_CHUNK = 128
LN_EPS = 1e-5
RMS_EPS = 1e-5
NEG = -1e30
LOG2E = math.log2(math.e)


def _params(sem, vmem_mib):
    return pltpu.CompilerParams(dimension_semantics=sem, vmem_limit_bytes=vmem_mib * MIB)


def _silu(x):
    return x / (1.0 + jnp.exp(-x))


def _layer_norm_rows(y, g, b):
    mu = jnp.mean(y, axis=-1, keepdims=True)
    yc = y - mu
    var = jnp.mean(yc * yc, axis=-1, keepdims=True)
    return yc * lax.rsqrt(var + LN_EPS) * g + b


def _mm_kernel(a_ref, w_ref, o_ref):
    o_ref[...] = jnp.dot(a_ref[...], w_ref[...], preferred_element_type=F32).astype(o_ref.dtype)


def _mm(a, w, col0, n, out_dtype, tm=1024, tn=512):
    m, k = a.shape
    tn = min(tn, n)
    assert col0 % tn == 0 and n % tn == 0 and m % tm == 0
    c0 = col0 // tn
    return pl.pallas_call(
        _mm_kernel,
        grid=(m // tm, n // tn),
        in_specs=[pl.BlockSpec((tm, k), lambda i, j: (i, 0)),
                  pl.BlockSpec((k, tn), lambda i, j: (0, c0 + j))],
        out_specs=pl.BlockSpec((tm, tn), lambda i, j: (i, j)),
        out_shape=jax.ShapeDtypeStruct((m, n), out_dtype),
        compiler_params=_params(("parallel", "arbitrary"), 48),
        name="mm",
    )(a, w)


def _qkv_rope_kernel(x_ref, w_ref, cos_ref, sa_ref, sb_ref, o_ref, *, n_rope_tiles, n_q_tiles, scale):
    n = pl.program_id(1)
    tn = w_ref.shape[1]

    @pl.when(n < n_rope_tiles)
    def _():
        c = cos_ref[...]
        sa = sa_ref[...]
        sb = sb_ref[...]
        sc = jnp.where(n < n_q_tiles, scale, 1.0).astype(F32)
        for p in range(tn // MXU_COLS):
            t2 = jnp.dot(x_ref[...], w_ref[:, p * MXU_COLS:(p + 1) * MXU_COLS], preferred_element_type=F32)
            for j in range(MXU_COLS // LANES):
                t = t2[:, j * LANES:(j + 1) * LANES]
                r = t * c + pltpu.roll(t, LANES - ROT_DIM // 2, 1) * sa + pltpu.roll(t, ROT_DIM // 2, 1) * sb
                col = p * MXU_COLS + j * LANES
                o_ref[:, col:col + LANES] = (r * sc).astype(o_ref.dtype)

    @pl.when(n >= n_rope_tiles)
    def _():
        o_ref[...] = jnp.dot(x_ref[...], w_ref[...], preferred_element_type=F32).astype(o_ref.dtype)


def _qkv_rope(xb, w, cos_t, sa_t, sb_t, d_model, tm=1024, tn=512):
    m, k = xb.shape
    n = w.shape[1]
    kern = functools.partial(_qkv_rope_kernel, n_rope_tiles=2 * d_model // tn, n_q_tiles=d_model // tn,
                             scale=HEAD_DIM ** -0.5)
    tab = pl.BlockSpec((tm, LANES), lambda i, j: (i, 0))
    return pl.pallas_call(
        kern,
        grid=(m // tm, n // tn),
        in_specs=[pl.BlockSpec((tm, k), lambda i, j: (i, 0)),
                  pl.BlockSpec((k, tn), lambda i, j: (0, j)),
                  tab, tab, tab],
        out_specs=pl.BlockSpec((tm, tn), lambda i, j: (i, j)),
        out_shape=jax.ShapeDtypeStruct((m, n), BF16),
        compiler_params=_params(("parallel", "arbitrary"), 48),
        name="qkv_rope",
    )(xb, w, cos_t, sa_t, sb_t)


def _attn_kernel(qi_ref, ki_ref, q_ref, k_ref, v_ref, bias_ref, o_ref, m_sc, l_sc, acc_sc, *, heads):
    p = pl.program_id(2)
    qi = qi_ref[p]
    ki = ki_ref[p]
    blk = q_ref.shape[1]

    @pl.when(ki == 0)
    def _():
        m_sc[...] = jnp.full_like(m_sc, 0.1 * NEG)
        l_sc[...] = jnp.zeros_like(l_sc)
        acc_sc[...] = jnp.zeros_like(acc_sc)

    bias = bias_ref[qi - ki]
    for j in range(heads):
        cols = slice(j * HEAD_DIM, (j + 1) * HEAD_DIM)
        s = lax.dot_general(q_ref[0, :, cols], k_ref[0, :, cols], (((1,), (1,)), ((), ())),
                            preferred_element_type=F32) + bias
        m_prev = m_sc[j]
        m_new = jnp.maximum(m_prev, jnp.max(s, axis=1, keepdims=True))
        alpha = jnp.exp(m_prev - m_new)
        pr = jnp.exp(s - jnp.concatenate([m_new] * (blk // LANES), axis=1))
        l_sc[j] = alpha * l_sc[j] + jnp.sum(pr, axis=1, keepdims=True)
        acc_sc[j] = alpha * acc_sc[j] + jnp.dot(pr.astype(BF16), v_ref[0, :, cols], preferred_element_type=F32)
        m_sc[j] = m_new

    @pl.when(ki == qi)
    def _():
        for j in range(heads):
            o_ref[0, :, j * HEAD_DIM:(j + 1) * HEAD_DIM] = (acc_sc[j] / l_sc[j]).astype(o_ref.dtype)


def _attention(qkv3, bias, n_heads, blk, heads=4):
    b, s, _ = qkv3.shape
    nq = s // blk
    pairs = [(qi, ki) for qi in range(nq) for ki in range(qi + 1)]
    qi_tab = jnp.asarray([pq for pq, _ in pairs], jnp.int32)
    ki_tab = jnp.asarray([pk for _, pk in pairs], jnp.int32)
    hg = n_heads // heads
    w = heads * HEAD_DIM
    return pl.pallas_call(
        functools.partial(_attn_kernel, heads=heads),
        grid_spec=pltpu.PrefetchScalarGridSpec(
            num_scalar_prefetch=2,
            grid=(b, hg, len(pairs)),
            in_specs=[pl.BlockSpec((1, blk, w), lambda bi, h, p, qt, kt: (bi, qt[p], h)),
                      pl.BlockSpec((1, blk, w), lambda bi, h, p, qt, kt: (bi, kt[p], hg + h)),
                      pl.BlockSpec((1, blk, w), lambda bi, h, p, qt, kt: (bi, kt[p], 2 * hg + h)),
                      pl.BlockSpec((nq, blk, blk), lambda bi, h, p, qt, kt: (0, 0, 0))],
            out_specs=pl.BlockSpec((1, blk, w), lambda bi, h, p, qt, kt: (bi, qt[p], h)),
            scratch_shapes=[pltpu.VMEM((heads, blk, LANES), F32), pltpu.VMEM((heads, blk, LANES), F32),
                            pltpu.VMEM((heads, blk, HEAD_DIM), F32)],
        ),
        out_shape=jax.ShapeDtypeStruct((b, s, n_heads * HEAD_DIM), BF16),
        compiler_params=_params(("parallel", "parallel", "arbitrary"), 48),
        name="attention",
    )(qi_tab, ki_tab, qkv3, qkv3, qkv3, bias)


def _branch_count_bias(s, blk, patterns):
    nq = s // blk
    off = jnp.arange(nq, dtype=jnp.int32)[:, None, None] * blk
    d = off + jnp.arange(blk, dtype=jnp.int32)[None, :, None] - jnp.arange(blk, dtype=jnp.int32)[None, None, :]
    cnt = jnp.zeros(d.shape, F32)
    for window, dilation in patterns:
        cnt = cnt + ((d >= 0) & (d % dilation == 0) & (d <= window)).astype(F32)
    return jnp.where(cnt > 0, jnp.log(jnp.maximum(cnt, 1.0)), NEG)


def _mm_res_ln_kernel(a_ref, w_ref, h_ref, g_ref, b_ref, of_ref, ob_ref, *, alpha, n_sub):
    sub = a_ref.shape[0] // n_sub
    for r in range(n_sub):
        rows = slice(r * sub, (r + 1) * sub)
        mix = jnp.dot(a_ref[rows, :], w_ref[...], preferred_element_type=F32)
        out = _layer_norm_rows(alpha * h_ref[rows, :] + mix, g_ref[...], b_ref[...])
        of_ref[rows, :] = out
        ob_ref[rows, :] = out.astype(BF16)


def _mm_res_ln(a, w, h, g, b, alpha, tm=512, n_sub=2):
    m, kdim = a.shape
    n = w.shape[1]
    kern = functools.partial(_mm_res_ln_kernel, alpha=alpha, n_sub=n_sub)
    row = pl.BlockSpec((tm, n), lambda i: (i, 0))
    vec = pl.BlockSpec((1, n), lambda i: (0, 0))
    return pl.pallas_call(
        kern,
        grid=(m // tm,),
        in_specs=[pl.BlockSpec((tm, kdim), lambda i: (i, 0)),
                  pl.BlockSpec((kdim, n), lambda i: (0, 0), pipeline_mode=pl.Buffered(1)),
                  row, vec, vec],
        out_specs=[row, row],
        out_shape=[jax.ShapeDtypeStruct((m, n), F32), jax.ShapeDtypeStruct((m, n), BF16)],
        compiler_params=_params(("parallel",), 56),
        name="mm_res_ln",
    )(a, w, h, g.reshape(1, n), b.reshape(1, n))


def _router_kernel(h_ref, wt_ref, b_ref, e_ref, g_ref):
    logits = lax.dot_general(wt_ref[...], h_ref[...], (((1,), (1,)), ((), ())),
                             precision=HIGHEST, preferred_element_type=F32)
    mx = jnp.max(logits, axis=0, keepdims=True)
    ex = jnp.exp(logits - mx)
    scores = ex / jnp.sum(ex, axis=0, keepdims=True)
    sel = scores + b_ref[...]
    tm = sel.shape[1]
    li = lax.broadcasted_iota(jnp.int32, (EPG, tm), 0)

    def first_argmax(v):
        m = jnp.max(v, axis=0, keepdims=True)
        return m, jnp.min(jnp.where(v == m, li, EPG), axis=0, keepdims=True)

    gs, i1s, i2s = [], [], []
    for g in range(N_GROUPS):
        slab = sel[g * EPG:(g + 1) * EPG, :]
        m1, i1 = first_argmax(slab)
        m2, i2 = first_argmax(jnp.where(li == i1, NEG, slab))
        gs.append(m1 + m2)
        i1s.append(i1)
        i2s.append(i2)
    best, gidx = gs[0], jnp.zeros((1, tm), jnp.int32)
    for g in range(1, N_GROUPS):
        better = gs[g] > best
        best = jnp.where(better, gs[g], best)
        gidx = jnp.where(better, g, gidx)
    l1 = jnp.zeros((1, tm), jnp.int32)
    l2 = jnp.zeros((1, tm), jnp.int32)
    s1 = jnp.zeros((1, tm), F32)
    s2 = jnp.zeros((1, tm), F32)
    for g in range(N_GROUPS):
        here = gidx == g
        slab = scores[g * EPG:(g + 1) * EPG, :]
        l1 = jnp.where(here, i1s[g], l1)
        l2 = jnp.where(here, i2s[g], l2)
        s1 = jnp.where(here, jnp.sum(jnp.where(li == i1s[g], slab, 0.0), axis=0, keepdims=True), s1)
        s2 = jnp.where(here, jnp.sum(jnp.where(li == i2s[g], slab, 0.0), axis=0, keepdims=True), s2)
    tot = s1 + s2
    e_ref[0:1, :] = gidx * EPG + l1
    e_ref[1:2, :] = gidx * EPG + l2
    g_ref[0:1, :] = s1 / tot
    g_ref[1:2, :] = s2 / tot


def _router(h, router_wt, bias_col, tm=512):
    t, d = h.shape
    return pl.pallas_call(
        _router_kernel,
        grid=(t // tm,),
        in_specs=[pl.BlockSpec((tm, d), lambda i: (i, 0)),
                  pl.BlockSpec((N_EXPERTS, d), lambda i: (0, 0)),
                  pl.BlockSpec((N_EXPERTS, 1), lambda i: (0, 0))],
        out_specs=[pl.BlockSpec((TOP_K, tm), lambda i: (0, i)), pl.BlockSpec((TOP_K, tm), lambda i: (0, i))],
        out_shape=[jax.ShapeDtypeStruct((TOP_K, t), jnp.int32), jax.ShapeDtypeStruct((TOP_K, t), F32)],
        compiler_params=_params(("parallel",), 32),
        name="router",
    )(h, router_wt, bias_col)


N_ROW_BUFS = 2
N_SPARE_ROWS = N_ROW_BUFS * MOE_BLOCK
N_WEIGHT_SETS = 2
WEIGHT_DMA_PRIORITY = 1


def _experts_kernel(be_ref, wset_ref, ahead_ref, second_ref, nused_ref, src_ref, dst_ref,
                    h_hbm, wg_hbm, wu_hbm, wd_hbm, ys_hbm,
                    xb0, xb1, ob0, ob1, wgl, wul, wdl, wgb, wub, wdb, gsem, ssem, wsem, *, layer, n_slots):
    i = pl.program_id(0)
    n_used = nused_ref[0]
    xbufs = (xb0, xb1)
    obufs = (ob0, ob1)

    def gather_start(blk, s):
        base = (blk + 1) * MOE_BLOCK
        for r in range(MOE_BLOCK):
            pltpu.make_async_copy(h_hbm.at[pl.ds(src_ref[base + r], 1), :],
                                  xbufs[s].at[pl.ds(r, 1), :], gsem.at[s]).start()

    def scatter_start(blk, s):
        base = (blk + 1) * MOE_BLOCK
        for r in range(MOE_BLOCK):
            pltpu.make_async_copy(obufs[s].at[pl.ds(r, 1), :],
                                  ys_hbm.at[pl.ds(dst_ref[base + r], 1), :], ssem.at[s]).start()

    def gather_wait(s):
        pltpu.make_async_copy(h_hbm.at[pl.ds(0, MOE_BLOCK), :], xbufs[s], gsem.at[s]).wait()

    def scatter_wait(s):
        pltpu.make_async_copy(obufs[s], ys_hbm.at[pl.ds(0, MOE_BLOCK), :], ssem.at[s]).wait()

    w_hbm = (wg_hbm, wu_hbm, wd_hbm)
    w_land = (wgl, wul, wdl)
    w_work = (wgb, wub, wdb)

    def weight_copy(e, p, k):
        return pltpu.make_async_copy(w_hbm[k].at[layer, e], w_land[k].at[p], wsem.at[p, k])

    @pl.when(i == 0)
    def _():
        ob1[...] = jnp.zeros_like(ob1)
        spare = pltpu.make_async_copy(ob1, ys_hbm.at[pl.ds(n_slots + MOE_BLOCK, MOE_BLOCK), :], ssem.at[1])
        spare.start()
        spare.wait()
        gather_start(0, 0)
        for k in range(len(w_hbm)):
            weight_copy(be_ref[0], 0, k).start(priority=WEIGHT_DMA_PRIORITY)

        @pl.when(second_ref[0] >= 0)
        def _():
            for k in range(len(w_hbm)):
                weight_copy(second_ref[0], 1, k).start(priority=WEIGHT_DMA_PRIORITY)

    @pl.when(i < n_used)
    def _():
        for p in range(N_WEIGHT_SETS):
            @pl.when(wset_ref[i] == p)
            def _():
                ahead = ahead_ref[i]
                for k in range(len(w_hbm)):
                    weight_copy(be_ref[i], p, k).wait()
                    w_work[k][...] = w_land[k][p].astype(BF16)

                    @pl.when(ahead >= 0)
                    def _():
                        weight_copy(ahead, p, k).start(priority=WEIGHT_DMA_PRIORITY)

        for s in range(N_ROW_BUFS):
            @pl.when(i % N_ROW_BUFS == s)
            def _():
                gather_wait(s)

                @pl.when(i >= 1)
                def _():
                    scatter_wait(s)

                x = xbufs[s][...].astype(BF16)
                hg = jnp.dot(x, wgb[...], preferred_element_type=F32)
                hu = jnp.dot(x, wub[...], preferred_element_type=F32)
                act = (_silu(hg) * hu).astype(BF16)
                obufs[s][...] = jnp.dot(act, wdb[...], preferred_element_type=F32)
                gather_start(i + 1, 1 - s)
                scatter_start(i - 1, 1 - s)

                @pl.when(i == n_used - 1)
                def _():
                    gather_wait(1 - s)
                    scatter_wait(1 - s)
                    scatter_start(i, s)
                    scatter_wait(s)


def _experts(meta, h, w_gate, w_up, w_down, layer, n_slots):
    t, d = h.shape
    de = w_gate.shape[3]
    nb = meta[0].shape[0]
    any_spec = pl.BlockSpec(memory_space=pl.ANY)
    rowbuf = pltpu.VMEM((MOE_BLOCK, d), F32)
    return pl.pallas_call(
        functools.partial(_experts_kernel, layer=layer, n_slots=n_slots),
        grid_spec=pltpu.PrefetchScalarGridSpec(
            num_scalar_prefetch=len(meta),
            grid=(nb,),
            in_specs=[any_spec, any_spec, any_spec, any_spec],
            out_specs=any_spec,
            scratch_shapes=[rowbuf, rowbuf, rowbuf, rowbuf,
                            pltpu.VMEM((N_WEIGHT_SETS, d, de), F32), pltpu.VMEM((N_WEIGHT_SETS, d, de), F32),
                            pltpu.VMEM((N_WEIGHT_SETS, de, d), F32),
                            pltpu.VMEM((d, de), BF16), pltpu.VMEM((d, de), BF16), pltpu.VMEM((de, d), BF16),
                            pltpu.SemaphoreType.DMA((N_ROW_BUFS,)), pltpu.SemaphoreType.DMA((N_ROW_BUFS,)),
                            pltpu.SemaphoreType.DMA((N_WEIGHT_SETS, 3))],
        ),
        out_shape=jax.ShapeDtypeStruct((n_slots + N_SPARE_ROWS, d), F32),
        compiler_params=_params(("arbitrary",), 56),
        name="experts",
    )(*meta, h, w_gate, w_up, w_down)


def _combine_ln_kernel(y0_ref, y1_ref, gate_ref, h_ref, g_ref, b_ref, of_ref, ob_ref, *, alpha):
    gt = gate_ref[...]
    moe = gt[:, 0:1] * y0_ref[...] + gt[:, 1:2] * y1_ref[...]
    out = _layer_norm_rows(alpha * h_ref[...] + moe, g_ref[...], b_ref[...])
    of_ref[...] = out
    ob_ref[...] = out.astype(BF16)


def _combine_ln(ys, gate_t, h, g, b, alpha, tm=256):
    t, d = h.shape
    row = pl.BlockSpec((tm, d), lambda i: (i, 0))
    vec = pl.BlockSpec((1, d), lambda i: (0, 0))
    return pl.pallas_call(
        functools.partial(_combine_ln_kernel, alpha=alpha),
        grid=(t // tm,),
        in_specs=[row, pl.BlockSpec((tm, d), lambda i: (t // tm + i, 0)),
                  pl.BlockSpec((tm, TOP_K), lambda i: (i, 0)), row, vec, vec],
        out_specs=[row, row],
        out_shape=[jax.ShapeDtypeStruct((t, d), F32), jax.ShapeDtypeStruct((t, d), BF16)],
        compiler_params=_params(("parallel",), 40),
        name="combine_ln",
    )(ys, ys, gate_t, h, g.reshape(1, d), b.reshape(1, d))


def _moe_layer(h, router_wt, bias_col, w_gate, w_up, w_down, layer, ln_g, ln_b, alpha):
    t, d = h.shape
    e_idx, gate = _router(h, router_wt, bias_col)
    n_slots = t * TOP_K
    e_flat = e_idx.reshape(n_slots)
    experts = jnp.arange(N_EXPERTS, dtype=jnp.int32)
    onehot = (e_flat[:, None] == experts[None, :]).astype(jnp.int32)
    csum = jnp.cumsum(onehot, axis=0)
    counts = csum[-1]
    padded = ((counts + MOE_BLOCK - 1) // MOE_BLOCK) * MOE_BLOCK
    pad_end = jnp.cumsum(padded)
    pad_start = pad_end - padded
    dest = jnp.sum(onehot * (csum - 1 + pad_start[None, :]), axis=1).astype(jnp.int32)
    n_blocks = n_slots // MOE_BLOCK + N_EXPERTS
    rows = n_blocks * MOE_BLOCK
    row_slot = jnp.full((rows,), -1, jnp.int32).at[dest].set(jnp.arange(n_slots, dtype=jnp.int32),
                                                             unique_indices=True)
    row_id = jnp.arange(-MOE_BLOCK, rows, dtype=jnp.int32)
    row_slot = jnp.concatenate([jnp.full((MOE_BLOCK,), -1, jnp.int32), row_slot])
    valid = row_slot >= 0
    row_src = jnp.where(valid, row_slot % t, row_id % t)
    spare = n_slots + ((row_id // MOE_BLOCK + 1) % N_ROW_BUFS) * MOE_BLOCK + row_id % MOE_BLOCK
    row_dst = jnp.where(valid, row_slot, spare)
    blk_row0 = jnp.arange(n_blocks, dtype=jnp.int32) * MOE_BLOCK
    block_e = jnp.clip(jnp.searchsorted(pad_end, blk_row0, side='right'), 0, N_EXPERTS - 1).astype(jnp.int32)
    first = jnp.concatenate([jnp.ones((1,), bool), block_e[1:] != block_e[:-1]])
    seq = jnp.concatenate([jnp.sort(jnp.where(counts > 0, experts, N_EXPERTS)),
                           jnp.full((N_WEIGHT_SETS,), N_EXPERTS, jnp.int32)])
    place = jnp.cumsum((counts > 0).astype(jnp.int32)) - 1
    blk_place = place[block_e]
    wset = jnp.where(first, blk_place % N_WEIGHT_SETS, -1).astype(jnp.int32)
    ahead = seq[blk_place + N_WEIGHT_SETS]
    ahead = jnp.where(ahead < N_EXPERTS, ahead, -1).astype(jnp.int32)
    second = jnp.where(seq[1] < N_EXPERTS, seq[1], -1).astype(jnp.int32).reshape(1)
    n_used = (pad_end[-1] // MOE_BLOCK).astype(jnp.int32).reshape(1)
    meta = (block_e, wset, ahead, second, n_used, row_src.astype(jnp.int32), row_dst.astype(jnp.int32))
    ys = _experts(meta, h, w_gate, w_up, w_down, layer, n_slots)
    return _combine_ln(ys, gate.T, h, ln_g, ln_b, alpha)


def _expand_heads(x, e_bf16):
    hi = x.astype(BF16)
    r1 = x - hi.astype(F32)
    mid = r1.astype(BF16)
    lo = (r1 - mid.astype(F32)).astype(BF16)
    dot = lambda a: jnp.dot(a, e_bf16, preferred_element_type=F32)
    return dot(hi) + dot(mid) + dot(lo)


def _ssd_kernel(z_ref, xbc_ref, prev_ref, dt_ref, cw_ref, cb_ref, dtb_ref, alog_ref, dexp_ref, nw_ref, e_ref,
                o_ref, state_sc, *, d_inner):
    c = pl.program_id(1)
    L = SSM_CHUNK
    gw = d_inner // SSM_GROUPS
    hpg = gw // SSM_HEAD_DIM
    pad = SUBLANES

    @pl.when(c == 0)
    def _():
        state_sc[...] = jnp.zeros_like(state_sc)

    has_prev = c > 0
    row_in_tile = lax.broadcasted_iota(jnp.int32, (pad, 1), 0)

    def conv_silu(col0, width):
        cols = slice(col0, col0 + width)
        x = xbc_ref[0, :, cols]
        tail = jnp.where(has_prev, prev_ref[0, :, cols], 0.0)
        acc = cb_ref[:, cols] + x * cw_ref[SSM_CONV - 1:SSM_CONV, cols]
        for j in range(1, SSM_CONV):
            xr = pltpu.roll(x, j, 0)
            head = jnp.where(row_in_tile < j, pltpu.roll(tail, j, 0), xr[0:pad, :])
            shifted = jnp.concatenate([head, xr[pad:, :]], axis=0)
            acc = acc + shifted * cw_ref[SSM_CONV - 1 - j:SSM_CONV - j, cols]
        return _silu(acc)

    dtv = dt_ref[0] + dtb_ref[...]
    dt = jnp.maximum(dtv, 0.0) + jnp.log1p(jnp.exp(-jnp.abs(dtv)))
    a = dt * (-jnp.exp(alog_ref[...]) * LOG2E)
    ri = lax.broadcasted_iota(jnp.int32, (L, L), 0)
    ci = lax.broadcasted_iota(jnp.int32, (L, L), 1)
    causal = ri >= ci
    a_cum = jnp.dot(causal.astype(F32), a, precision=HIGHEST, preferred_element_type=F32)
    a_cum_t = a_cum.T
    e_mat = e_ref[...]
    acum_x = _expand_heads(a_cum, e_mat)
    dt_x = _expand_heads(dt, e_mat)
    lane = lax.broadcasted_iota(jnp.int32, (L, LANES), 1)
    first_head = lane < SSM_HEAD_DIM

    for g in range(SSM_GROUPS):
        cols = slice(g * gw, (g + 1) * gw)
        xg = conv_silu(g * gw, gw)
        bg = conv_silu(d_inner + g * SSM_STATE, SSM_STATE)
        cg = conv_silu(d_inner + (SSM_GROUPS + g) * SSM_STATE, SSM_STATE)
        ac = acum_x[:, cols]
        last = ac[L - 1:L, :]
        xdt = xg * dt_x[:, cols]
        cb16 = cg.astype(BF16)
        bb16 = bg.astype(BF16)
        cbm = lax.dot_general(cb16, bb16, (((1,), (1,)), ((), ())), preferred_element_type=F32)
        st = state_sc[g]
        y_off = jnp.dot(cb16, st.astype(BF16), preferred_element_type=F32) * jnp.exp2(ac)
        xd = (xdt * jnp.exp2(last - ac)).astype(BF16)
        state_sc[g] = st * jnp.exp2(last) + jnp.dot(bg.T.astype(BF16), xd, preferred_element_type=F32)
        xdt16 = xdt.astype(BF16)
        pieces = []
        for jp in range(hpg // 2):
            lhs = []
            for hh in (g * hpg + 2 * jp, g * hpg + 2 * jp + 1):
                seg = jnp.where(causal, a_cum[:, hh:hh + 1] - a_cum_t[hh:hh + 1, :], NEG)
                lhs.append((cbm * jnp.exp2(seg)).astype(BF16))
            xp = xdt16[:, jp * LANES:(jp + 1) * LANES]
            zero = jnp.zeros_like(xp)
            rhs = jnp.concatenate([jnp.where(first_head, xp, zero), jnp.where(first_head, zero, xp)], axis=0)
            pieces.append(jnp.dot(jnp.concatenate(lhs, axis=1), rhs, preferred_element_type=F32))
        y = jnp.concatenate(pieces, axis=1) + y_off + dexp_ref[:, cols] * xg
        y = y * _silu(z_ref[0, :, cols])
        y = y * lax.rsqrt(jnp.mean(y * y, axis=1, keepdims=True) + RMS_EPS) * nw_ref[:, cols]
        o_ref[0, :, cols] = y.astype(o_ref.dtype)


def _ssd(z3, xbc3, dt3, conv_w, conv_b, dt_bias_p, a_log_p, d_exp, norm_w, e_mat):
    b, s, d_inner = z3.shape
    cdim = xbc3.shape[2]
    nc = s // SSM_CHUNK
    gw = d_inner // SSM_GROUPS
    rows_per_prev = SSM_CHUNK // SUBLANES
    full = lambda shape: pl.BlockSpec(shape, lambda bi, c: tuple(0 for _ in shape))
    return pl.pallas_call(
        functools.partial(_ssd_kernel, d_inner=d_inner),
        grid=(b, nc),
        in_specs=[pl.BlockSpec((1, SSM_CHUNK, d_inner), lambda bi, c: (bi, c, 0)),
                  pl.BlockSpec((1, SSM_CHUNK, cdim), lambda bi, c: (bi, c, 0)),
                  pl.BlockSpec((1, SUBLANES, cdim), lambda bi, c: (bi, jnp.maximum(c * rows_per_prev - 1, 0), 0)),
                  pl.BlockSpec((1, SSM_CHUNK, LANES), lambda bi, c: (bi, c, 0)),
                  full((SSM_CONV, cdim)), full((1, cdim)), full((1, LANES)), full((1, LANES)),
                  full((1, d_inner)), full((1, d_inner)), full((LANES, d_inner))],
        out_specs=pl.BlockSpec((1, SSM_CHUNK, d_inner), lambda bi, c: (bi, c, 0)),
        out_shape=jax.ShapeDtypeStruct((b, s, d_inner), BF16),
        scratch_shapes=[pltpu.VMEM((SSM_GROUPS, SSM_STATE, gw), F32)],
        compiler_params=_params(("parallel", "arbitrary"), 56),
        name="ssd",
    )(z3, xbc3, xbc3, dt3, conv_w, conv_b, dt_bias_p, a_log_p, d_exp, norm_w, e_mat)


def kernel(x, positions, attn_w_qkv, attn_w_o, ssm_w_in, ssm_conv_w, ssm_conv_b, ssm_dt_bias, ssm_A_log, ssm_D, ssm_norm_w, ssm_w_out, router_w, router_bias, moe_w_gate, moe_w_up, moe_w_down, ln_mix_g, ln_mix_b, ln_ffn_g, ln_ffn_b):
    bsz, seq, d_model = x.shape
    t = bsz * seq
    depth = moe_w_gate.shape[0]
    alpha = (2 * depth) ** 0.25
    n_heads = d_model // HEAD_DIM
    patterns = ((128, 1), (512, 4), (2048, 16))
    attn_blk = 512

    half = ROT_DIM // 2
    inv_freq = ROPE_THETA ** (-jnp.arange(0, ROT_DIM, 2, dtype=F32) / ROT_DIM)
    ang = positions.astype(F32).reshape(t, 1) * inv_freq
    cos, sin = jnp.cos(ang), jnp.sin(ang)
    zeros_r = jnp.zeros((t, LANES - ROT_DIM), F32)
    cos_t = jnp.concatenate([cos, cos, jnp.ones((t, LANES - ROT_DIM), F32)], axis=1)
    sa_t = jnp.concatenate([-sin, jnp.zeros((t, half), F32), zeros_r], axis=1)
    sb_t = jnp.concatenate([jnp.zeros((t, half), F32), sin, zeros_r], axis=1)
    bias = _branch_count_bias(seq, attn_blk, patterns)

    router_wt = router_w.T
    bias_col = router_bias.reshape(N_EXPERTS, 1).astype(F32)

    h = x.reshape(t, d_model)
    hb = h.astype(BF16)
    for i in range(depth):
        j = i // 2
        if i % 2 == 0:
            qkv = _qkv_rope(hb, attn_w_qkv[j].astype(BF16), cos_t, sa_t, sb_t, d_model)
            att = _attention(qkv.reshape(bsz, seq, 3 * d_model), bias, n_heads, attn_blk)
            h, hb = _mm_res_ln(att.reshape(t, d_model), attn_w_o[j].astype(BF16), h, ln_mix_g[i], ln_mix_b[i], alpha)
        else:
            d_inner = ssm_norm_w.shape[1]
            n_ssm_heads = ssm_dt_bias.shape[1]
            cdim = ssm_conv_w.shape[2]
            w_in = ssm_w_in[j].astype(BF16)
            z = _mm(hb, w_in, 0, d_inner, F32)
            xbc = _mm(hb, w_in, d_inner, cdim, F32)
            w_dt = jnp.pad(w_in[:, d_inner + cdim:], ((0, 0), (0, LANES - n_ssm_heads)))
            dt = _mm(hb, w_dt, 0, LANES, F32)
            hpad = (0, LANES - n_ssm_heads)
            head_of_ch = jnp.arange(d_inner, dtype=jnp.int32) // SSM_HEAD_DIM
            e_mat = (jnp.arange(LANES, dtype=jnp.int32)[:, None] == head_of_ch[None, :]).astype(BF16)
            y = _ssd(z.reshape(bsz, seq, d_inner), xbc.reshape(bsz, seq, cdim), dt.reshape(bsz, seq, LANES),
                     ssm_conv_w[j], ssm_conv_b[j].reshape(1, cdim),
                     jnp.pad(ssm_dt_bias[j], hpad).reshape(1, LANES), jnp.pad(ssm_A_log[j], hpad).reshape(1, LANES),
                     jnp.repeat(ssm_D[j], SSM_HEAD_DIM).reshape(1, d_inner), ssm_norm_w[j].reshape(1, d_inner), e_mat)
            h, hb = _mm_res_ln(y.reshape(t, d_inner), ssm_w_out[j].astype(BF16), h, ln_mix_g[i], ln_mix_b[i], alpha)
        h, hb = _moe_layer(h, router_wt, bias_col, moe_w_gate, moe_w_up, moe_w_down, i,
                           ln_ffn_g[i], ln_ffn_b[i], alpha)
    return h.reshape(bsz, seq, d_model)
```

```python
import functools
import math

import jax
import jax.numpy as jnp
from jax import lax
from jax.experimental import pallas as pl
from jax.experimental.pallas import tpu as pltpu

F32 = jnp.float32
BF16 = jnp.bfloat16
HIGHEST = lax.Precision.HIGHEST

LANES = 128
SUBLANES = 8
MXU_COLS = 256
MIB = 1024 * 1024

HEAD_DIM = 128
ROT_DIM = HEAD_DIM // 4
ROPE_THETA = 500000.0
N_EXPERTS = 64
N_GROUPS = 8
EPG = N_EXPERTS // N_GROUPS
TOP_K = 2
MOE_BLOCK = 128
SSM_HEAD_DIM = 64
SSM_GROUPS = 8
SSM_STATE = 128
SSM_CONV = 4
SSM_CHUNK = 128
LN_EPS = 1e-5
RMS_EPS = 1e-5
NEG = -1e30
LOG2E = math.log2(math.e)


def _params(sem, vmem_mib):
    return pltpu.CompilerParams(dimension_semantics=sem, vmem_limit_bytes=vmem_mib * MIB)


def _silu(x):
    return x / (1.0 + jnp.exp(-x))


def _layer_norm_rows(y, g, b):
    mu = jnp.mean(y, axis=-1, keepdims=True)
    yc = y - mu
    var = jnp.mean(yc * yc, axis=-1, keepdims=True)
    return yc * lax.rsqrt(var + LN_EPS) * g + b


def _mm_kernel(a_ref, w_ref, o_ref):
    o_ref[...] = jnp.dot(a_ref[...], w_ref[...], preferred_element_type=F32).astype(o_ref.dtype)


def _mm(a, w, col0, n, out_dtype, tm=1024, tn=512):
    m, k = a.shape
    tn = min(tn, n)
    assert col0 % tn == 0 and n % tn == 0 and m % tm == 0
    c0 = col0 // tn
    return pl.pallas_call(
        _mm_kernel,
        grid=(m // tm, n // tn),
        in_specs=[pl.BlockSpec((tm, k), lambda i, j: (i, 0)),
                  pl.BlockSpec((k, tn), lambda i, j: (0, c0 + j))],
        out_specs=pl.BlockSpec((tm, tn), lambda i, j: (i, j)),
        out_shape=jax.ShapeDtypeStruct((m, n), out_dtype),
        compiler_params=_params(("parallel", "arbitrary"), 48),
        name="mm",
    )(a, w)


def _qkv_rope_kernel(x_ref, w_ref, cos_ref, sa_ref, sb_ref, o_ref, *, n_rope_tiles, n_q_tiles, scale):
    n = pl.program_id(1)
    tn = w_ref.shape[1]

    @pl.when(n < n_rope_tiles)
    def _():
        c = cos_ref[...]
        sa = sa_ref[...]
        sb = sb_ref[...]
        sc = jnp.where(n < n_q_tiles, scale, 1.0).astype(F32)
        for p in range(tn // MXU_COLS):
            t2 = jnp.dot(x_ref[...], w_ref[:, p * MXU_COLS:(p + 1) * MXU_COLS], preferred_element_type=F32)
            for j in range(MXU_COLS // LANES):
                t = t2[:, j * LANES:(j + 1) * LANES]
                r = t * c + pltpu.roll(t, LANES - ROT_DIM // 2, 1) * sa + pltpu.roll(t, ROT_DIM // 2, 1) * sb
                col = p * MXU_COLS + j * LANES
                o_ref[:, col:col + LANES] = (r * sc).astype(o_ref.dtype)

    @pl.when(n >= n_rope_tiles)
    def _():
        o_ref[...] = jnp.dot(x_ref[...], w_ref[...], preferred_element_type=F32).astype(o_ref.dtype)


def _qkv_rope(xb, w, cos_t, sa_t, sb_t, d_model, tm=1024, tn=512):
    m, k = xb.shape
    n = w.shape[1]
    kern = functools.partial(_qkv_rope_kernel, n_rope_tiles=2 * d_model // tn, n_q_tiles=d_model // tn,
                             scale=HEAD_DIM ** -0.5 * LOG2E)
    tab = pl.BlockSpec((tm, LANES), lambda i, j: (i, 0))
    return pl.pallas_call(
        kern,
        grid=(m // tm, n // tn),
        in_specs=[pl.BlockSpec((tm, k), lambda i, j: (i, 0)),
                  pl.BlockSpec((k, tn), lambda i, j: (0, j)),
                  tab, tab, tab],
        out_specs=pl.BlockSpec((tm, tn), lambda i, j: (i, j)),
        out_shape=jax.ShapeDtypeStruct((m, n), BF16),
        compiler_params=_params(("parallel", "arbitrary"), 48),
        name="qkv_rope",
    )(xb, w, cos_t, sa_t, sb_t)


def _attn_kernel(qi_ref, ki_ref, q_ref, k_ref, v_ref, bias_ref, o_ref, m_sc, l_sc, acc_sc, *, heads):
    p = pl.program_id(2)
    qi = qi_ref[p]
    ki = ki_ref[p]
    blk = q_ref.shape[1]

    @pl.when(ki == 0)
    def _():
        m_sc[...] = jnp.full_like(m_sc, 0.1 * NEG)
        l_sc[...] = jnp.zeros_like(l_sc)
        acc_sc[...] = jnp.zeros_like(acc_sc)

    bias = bias_ref[qi - ki]
    for j in range(heads):
        cols = slice(j * HEAD_DIM, (j + 1) * HEAD_DIM)
        s = lax.dot_general(q_ref[0, :, cols], k_ref[0, :, cols], (((1,), (1,)), ((), ())),
                            preferred_element_type=F32) + bias
        m_prev = m_sc[j]
        m_new = jnp.maximum(m_prev, jnp.max(s, axis=1, keepdims=True))
        alpha = jnp.exp2(m_prev - m_new)
        pr = jnp.exp2(s - jnp.concatenate([m_new] * (blk // LANES), axis=1))
        l_sc[j] = alpha * l_sc[j] + jnp.sum(pr, axis=1, keepdims=True)
        acc_sc[j] = alpha * acc_sc[j] + jnp.dot(pr.astype(BF16), v_ref[0, :, cols], preferred_element_type=F32)
        m_sc[j] = m_new

    @pl.when(ki == qi)
    def _():
        for j in range(heads):
            o_ref[0, :, j * HEAD_DIM:(j + 1) * HEAD_DIM] = (acc_sc[j] / l_sc[j]).astype(o_ref.dtype)


def _attention(qkv3, bias, n_heads, blk, heads=4):
    b, s, _ = qkv3.shape
    nq = s // blk
    pairs = [(qi, ki) for qi in range(nq) for ki in range(qi + 1)]
    qi_tab = jnp.asarray([pq for pq, _ in pairs], jnp.int32)
    ki_tab = jnp.asarray([pk for _, pk in pairs], jnp.int32)
    hg = n_heads // heads
    w = heads * HEAD_DIM
    return pl.pallas_call(
        functools.partial(_attn_kernel, heads=heads),
        grid_spec=pltpu.PrefetchScalarGridSpec(
            num_scalar_prefetch=2,
            grid=(b, hg, len(pairs)),
            in_specs=[pl.BlockSpec((1, blk, w), lambda bi, h, p, qt, kt: (bi, qt[p], h)),
                      pl.BlockSpec((1, blk, w), lambda bi, h, p, qt, kt: (bi, kt[p], hg + h)),
                      pl.BlockSpec((1, blk, w), lambda bi, h, p, qt, kt: (bi, kt[p], 2 * hg + h)),
                      pl.BlockSpec((nq, blk, blk), lambda bi, h, p, qt, kt: (0, 0, 0))],
            out_specs=pl.BlockSpec((1, blk, w), lambda bi, h, p, qt, kt: (bi, qt[p], h)),
            scratch_shapes=[pltpu.VMEM((heads, blk, LANES), F32), pltpu.VMEM((heads, blk, LANES), F32),
                            pltpu.VMEM((heads, blk, HEAD_DIM), F32)],
        ),
        out_shape=jax.ShapeDtypeStruct((b, s, n_heads * HEAD_DIM), BF16),
        compiler_params=_params(("parallel", "parallel", "arbitrary"), 48),
        name="attention",
    )(qi_tab, ki_tab, qkv3, qkv3, qkv3, bias)


def _branch_count_bias(s, blk, patterns):
    nq = s // blk
    off = jnp.arange(nq, dtype=jnp.int32)[:, None, None] * blk
    d = off + jnp.arange(blk, dtype=jnp.int32)[None, :, None] - jnp.arange(blk, dtype=jnp.int32)[None, None, :]
    cnt = jnp.zeros(d.shape, F32)
    for window, dilation in patterns:
        cnt = cnt + ((d >= 0) & (d % dilation == 0) & (d <= window)).astype(F32)
    return jnp.where(cnt > 0, jnp.log2(jnp.maximum(cnt, 1.0)), NEG)


def _pack_bf16_pairs(x):
    n = x.shape[1] // 2
    lo = lax.bitcast_convert_type(x[:, :n].astype(BF16).astype(F32), jnp.uint32)
    hi = lax.bitcast_convert_type(x[:, n:].astype(BF16).astype(F32), jnp.uint32)
    return (lo >> 16) | hi


def _unpack_bf16_pairs(w):
    lo = lax.bitcast_convert_type(w << 16, F32)
    hi = lax.bitcast_convert_type(w & jnp.uint32(0xFFFF0000), F32)
    return jnp.concatenate([lo, hi], axis=1)


def _mm_res_ln_route_kernel(a_ref, w_ref, h_ref, g_ref, b_ref, rwh_ref, rwl_ref, rb_ref,
                            of_ref, hp_ref, e_ref, gt_ref, *, alpha, n_sub):
    sub = a_ref.shape[0] // n_sub
    for r in range(n_sub):
        rows = slice(r * sub, (r + 1) * sub)
        mix = jnp.dot(a_ref[rows, :], w_ref[...], preferred_element_type=F32)
        out = _layer_norm_rows(alpha * h_ref[rows, :] + mix, g_ref[...], b_ref[...])
        of_ref[rows, :] = out
        hp_ref[rows, :] = _pack_bf16_pairs(out)
        experts, gates = _route_top2(out, rwh_ref[...], rwl_ref[...], rb_ref[...])
        for k in range(TOP_K):
            e_ref[k:k + 1, rows] = experts[k]
            gt_ref[k:k + 1, rows] = gates[k]


def _mm_res_ln_route(a, w, h, g, b, router_w, bias_col, alpha, tm=512, n_sub=2):
    m, kdim = a.shape
    n = w.shape[1]
    kern = functools.partial(_mm_res_ln_route_kernel, alpha=alpha, n_sub=n_sub)
    row = pl.BlockSpec((tm, n), lambda i: (i, 0))
    vec = pl.BlockSpec((1, n), lambda i: (0, 0))
    sel = pl.BlockSpec((TOP_K, tm), lambda i: (0, i))
    rw = jnp.pad(router_w, ((0, 0), (0, LANES - N_EXPERTS)))
    rw_hi = rw.astype(BF16)
    rw_lo = (rw - rw_hi.astype(F32)).astype(BF16)
    rw_spec = pl.BlockSpec((n, LANES), lambda i: (0, 0))
    return pl.pallas_call(
        kern,
        grid=(m // tm,),
        in_specs=[pl.BlockSpec((tm, kdim), lambda i: (i, 0)),
                  pl.BlockSpec((kdim, n), lambda i: (0, 0), pipeline_mode=pl.Buffered(1)),
                  row, vec, vec,
                  rw_spec, rw_spec, pl.BlockSpec((N_EXPERTS, 1), lambda i: (0, 0))],
        out_specs=[row, pl.BlockSpec((tm, n // 2), lambda i: (i, 0)), sel, sel],
        out_shape=[jax.ShapeDtypeStruct((m, n), F32), jax.ShapeDtypeStruct((m, n // 2), jnp.uint32),
                   jax.ShapeDtypeStruct((TOP_K, m), jnp.int32), jax.ShapeDtypeStruct((TOP_K, m), F32)],
        compiler_params=_params(("parallel",), 56),
        name="mm_res_ln_route",
    )(a, w, h, g.reshape(1, n), b.reshape(1, n), rw_hi, rw_lo, bias_col)


def _route_top2(h, w_hi, w_lo, bias_col):
    h_hi = h.astype(BF16)
    h_lo = (h - h_hi.astype(F32)).astype(BF16)
    dot = lambda a, b: jnp.dot(a, b, preferred_element_type=F32)
    logits = (dot(h_hi, w_hi) + dot(h_lo, w_hi) + dot(h_hi, w_lo)).T[:N_EXPERTS, :]
    mx = jnp.max(logits, axis=0, keepdims=True)
    ex = jnp.exp(logits - mx)
    scores = ex / jnp.sum(ex, axis=0, keepdims=True)
    sel = scores + bias_col
    tm = sel.shape[1]
    li = lax.broadcasted_iota(jnp.int32, (EPG, tm), 0)

    def first_argmax(v):
        m = jnp.max(v, axis=0, keepdims=True)
        return m, jnp.min(jnp.where(v == m, li, EPG), axis=0, keepdims=True)

    gs, i1s, i2s = [], [], []
    for g in range(N_GROUPS):
        slab = sel[g * EPG:(g + 1) * EPG, :]
        m1, i1 = first_argmax(slab)
        m2, i2 = first_argmax(jnp.where(li == i1, NEG, slab))
        gs.append(m1 + m2)
        i1s.append(i1)
        i2s.append(i2)
    best, gidx = gs[0], jnp.zeros((1, tm), jnp.int32)
    for g in range(1, N_GROUPS):
        better = gs[g] > best
        best = jnp.where(better, gs[g], best)
        gidx = jnp.where(better, g, gidx)
    l1 = jnp.zeros((1, tm), jnp.int32)
    l2 = jnp.zeros((1, tm), jnp.int32)
    s1 = jnp.zeros((1, tm), F32)
    s2 = jnp.zeros((1, tm), F32)
    for g in range(N_GROUPS):
        here = gidx == g
        slab = scores[g * EPG:(g + 1) * EPG, :]
        l1 = jnp.where(here, i1s[g], l1)
        l2 = jnp.where(here, i2s[g], l2)
        s1 = jnp.where(here, jnp.sum(jnp.where(li == i1s[g], slab, 0.0), axis=0, keepdims=True), s1)
        s2 = jnp.where(here, jnp.sum(jnp.where(li == i2s[g], slab, 0.0), axis=0, keepdims=True), s2)
    tot = s1 + s2
    return (gidx * EPG + l1, gidx * EPG + l2), (s1 / tot, s2 / tot)


N_ROW_BUFS = 2
N_SPARE_ROWS = N_ROW_BUFS * MOE_BLOCK
N_WEIGHT_SETS = 2
WEIGHT_DMA_PRIORITY = 1


def _experts_kernel(be_ref, wset_ref, ahead_ref, second_ref, nused_ref, src_ref, dst_ref,
                    h_hbm, wg_hbm, wu_hbm, wd_hbm, ys_hbm,
                    xb0, xb1, ob0, ob1, wgl, wul, wdl, wgb, wub, wdb, gsem, ssem, wsem, *, layer, n_slots):
    i = pl.program_id(0)
    n_used = nused_ref[0]
    xbufs = (xb0, xb1)
    obufs = (ob0, ob1)

    def gather_start(blk, s):
        base = (blk + 1) * MOE_BLOCK
        for r in range(MOE_BLOCK):
            pltpu.make_async_copy(h_hbm.at[pl.ds(src_ref[base + r], 1), :],
                                  xbufs[s].at[pl.ds(r, 1), :], gsem.at[s]).start()

    def scatter_start(blk, s):
        base = (blk + 1) * MOE_BLOCK
        for r in range(MOE_BLOCK):
            pltpu.make_async_copy(obufs[s].at[pl.ds(r, 1), :],
                                  ys_hbm.at[pl.ds(dst_ref[base + r], 1), :], ssem.at[s]).start()

    def gather_wait(s):
        pltpu.make_async_copy(h_hbm.at[pl.ds(0, MOE_BLOCK), :], xbufs[s], gsem.at[s]).wait()

    def scatter_wait(s):
        pltpu.make_async_copy(obufs[s], ys_hbm.at[pl.ds(0, MOE_BLOCK), :], ssem.at[s]).wait()

    w_hbm = (wg_hbm, wu_hbm, wd_hbm)
    w_land = (wgl, wul, wdl)
    w_work = (wgb, wub, wdb)

    def weight_copy(e, p, k):
        return pltpu.make_async_copy(w_hbm[k].at[layer, e], w_land[k].at[p], wsem.at[p, k])

    @pl.when(i == 0)
    def _():
        ob1[...] = jnp.zeros_like(ob1)
        spare = pltpu.make_async_copy(ob1, ys_hbm.at[pl.ds(n_slots + MOE_BLOCK, MOE_BLOCK), :], ssem.at[1])
        spare.start()
        spare.wait()
        gather_start(0, 0)
        for k in range(len(w_hbm)):
            weight_copy(be_ref[0], 0, k).start(priority=WEIGHT_DMA_PRIORITY)

        @pl.when(second_ref[0] >= 0)
        def _():
            for k in range(len(w_hbm)):
                weight_copy(second_ref[0], 1, k).start(priority=WEIGHT_DMA_PRIORITY)

    @pl.when(i < n_used)
    def _():
        for p in range(N_WEIGHT_SETS):
            @pl.when(wset_ref[i] == p)
            def _():
                ahead = ahead_ref[i]
                for k in range(len(w_hbm)):
                    weight_copy(be_ref[i], p, k).wait()
                    w_work[k][...] = w_land[k][p].astype(BF16)

                    @pl.when(ahead >= 0)
                    def _():
                        weight_copy(ahead, p, k).start(priority=WEIGHT_DMA_PRIORITY)

        for s in range(N_ROW_BUFS):
            @pl.when(i % N_ROW_BUFS == s)
            def _():
                gather_wait(s)

                @pl.when(i >= 1)
                def _():
                    scatter_wait(s)

                x = _unpack_bf16_pairs(xbufs[s][...]).astype(BF16)
                hg = jnp.dot(x, wgb[...], preferred_element_type=F32)
                hu = jnp.dot(x, wub[...], preferred_element_type=F32)
                act = (_silu(hg) * hu).astype(BF16)
                obufs[s][...] = _pack_bf16_pairs(jnp.dot(act, wdb[...], preferred_element_type=F32))
                gather_start(i + 1, 1 - s)
                scatter_start(i - 1, 1 - s)

                @pl.when(i == n_used - 1)
                def _():
                    gather_wait(1 - s)
                    scatter_wait(1 - s)
                    scatter_start(i, s)
                    scatter_wait(s)


def _experts(meta, hp, w_gate, w_up, w_down, layer, n_slots):
    d, de = w_gate.shape[2], w_gate.shape[3]
    nb = meta[0].shape[0]
    any_spec = pl.BlockSpec(memory_space=pl.ANY)
    rowbuf = pltpu.VMEM((MOE_BLOCK, hp.shape[1]), jnp.uint32)
    return pl.pallas_call(
        functools.partial(_experts_kernel, layer=layer, n_slots=n_slots),
        grid_spec=pltpu.PrefetchScalarGridSpec(
            num_scalar_prefetch=len(meta),
            grid=(nb,),
            in_specs=[any_spec, any_spec, any_spec, any_spec],
            out_specs=any_spec,
            scratch_shapes=[rowbuf, rowbuf, rowbuf, rowbuf,
                            pltpu.VMEM((N_WEIGHT_SETS, d, de), F32), pltpu.VMEM((N_WEIGHT_SETS, d, de), F32),
                            pltpu.VMEM((N_WEIGHT_SETS, de, d), F32),
                            pltpu.VMEM((d, de), BF16), pltpu.VMEM((d, de), BF16), pltpu.VMEM((de, d), BF16),
                            pltpu.SemaphoreType.DMA((N_ROW_BUFS,)), pltpu.SemaphoreType.DMA((N_ROW_BUFS,)),
                            pltpu.SemaphoreType.DMA((N_WEIGHT_SETS, 3))],
        ),
        out_shape=jax.ShapeDtypeStruct((n_slots + N_SPARE_ROWS, hp.shape[1]), jnp.uint32),
        compiler_params=_params(("arbitrary",), 56),
        name="experts",
    )(*meta, hp, w_gate, w_up, w_down)


def _combine_ln_kernel(y0_ref, y1_ref, gate_ref, h_ref, g_ref, b_ref, of_ref, ob_ref, *, alpha):
    gt = gate_ref[...]
    moe = gt[:, 0:1] * _unpack_bf16_pairs(y0_ref[...]) + gt[:, 1:2] * _unpack_bf16_pairs(y1_ref[...])
    out = _layer_norm_rows(alpha * h_ref[...] + moe, g_ref[...], b_ref[...])
    of_ref[...] = out
    ob_ref[...] = out.astype(BF16)


def _combine_ln(ys, gate_t, h, g, b, alpha, tm=256):
    t, d = h.shape
    row = pl.BlockSpec((tm, d), lambda i: (i, 0))
    vec = pl.BlockSpec((1, d), lambda i: (0, 0))
    dp = ys.shape[1]
    return pl.pallas_call(
        functools.partial(_combine_ln_kernel, alpha=alpha),
        grid=(t // tm,),
        in_specs=[pl.BlockSpec((tm, dp), lambda i: (i, 0)), pl.BlockSpec((tm, dp), lambda i: (t // tm + i, 0)),
                  pl.BlockSpec((tm, TOP_K), lambda i: (i, 0)), row, vec, vec],
        out_specs=[row, row],
        out_shape=[jax.ShapeDtypeStruct((t, d), F32), jax.ShapeDtypeStruct((t, d), BF16)],
        compiler_params=_params(("parallel",), 40),
        name="combine_ln",
    )(ys, ys, gate_t, h, g.reshape(1, d), b.reshape(1, d))


def _moe_layer(h, hp, e_idx, gate, w_gate, w_up, w_down, layer, ln_g, ln_b, alpha):
    t, d = h.shape
    n_slots = t * TOP_K
    e_flat = e_idx.reshape(n_slots)
    experts = jnp.arange(N_EXPERTS, dtype=jnp.int32)
    onehot = (e_flat[:, None] == experts[None, :]).astype(jnp.int32)
    csum = jnp.cumsum(onehot, axis=0)
    counts = csum[-1]
    padded = ((counts + MOE_BLOCK - 1) // MOE_BLOCK) * MOE_BLOCK
    pad_end = jnp.cumsum(padded)
    pad_start = pad_end - padded
    dest = jnp.sum(onehot * (csum - 1 + pad_start[None, :]), axis=1).astype(jnp.int32)
    n_blocks = n_slots // MOE_BLOCK + N_EXPERTS
    rows = n_blocks * MOE_BLOCK
    row_slot = jnp.full((rows,), -1, jnp.int32).at[dest].set(jnp.arange(n_slots, dtype=jnp.int32),
                                                             unique_indices=True)
    row_id = jnp.arange(-MOE_BLOCK, rows, dtype=jnp.int32)
    row_slot = jnp.concatenate([jnp.full((MOE_BLOCK,), -1, jnp.int32), row_slot])
    valid = row_slot >= 0
    row_src = jnp.where(valid, row_slot % t, row_id % t)
    spare = n_slots + ((row_id // MOE_BLOCK + 1) % N_ROW_BUFS) * MOE_BLOCK + row_id % MOE_BLOCK
    row_dst = jnp.where(valid, row_slot, spare)
    blk_row0 = jnp.arange(n_blocks, dtype=jnp.int32) * MOE_BLOCK
    block_e = jnp.clip(jnp.searchsorted(pad_end, blk_row0, side='right'), 0, N_EXPERTS - 1).astype(jnp.int32)
    first = jnp.concatenate([jnp.ones((1,), bool), block_e[1:] != block_e[:-1]])
    seq = jnp.concatenate([jnp.sort(jnp.where(counts > 0, experts, N_EXPERTS)),
                           jnp.full((N_WEIGHT_SETS,), N_EXPERTS, jnp.int32)])
    place = jnp.cumsum((counts > 0).astype(jnp.int32)) - 1
    blk_place = place[block_e]
    wset = jnp.where(first, blk_place % N_WEIGHT_SETS, -1).astype(jnp.int32)
    ahead = seq[blk_place + N_WEIGHT_SETS]
    ahead = jnp.where(ahead < N_EXPERTS, ahead, -1).astype(jnp.int32)
    second = jnp.where(seq[1] < N_EXPERTS, seq[1], -1).astype(jnp.int32).reshape(1)
    n_used = (pad_end[-1] // MOE_BLOCK).astype(jnp.int32).reshape(1)
    meta = (block_e, wset, ahead, second, n_used, row_src.astype(jnp.int32), row_dst.astype(jnp.int32))
    ys = _experts(meta, hp, w_gate, w_up, w_down, layer, n_slots)
    return _combine_ln(ys, gate.T, h, ln_g, ln_b, alpha)


def _expand_heads(x, e_bf16):
    hi = x.astype(BF16)
    r1 = x - hi.astype(F32)
    mid = r1.astype(BF16)
    lo = (r1 - mid.astype(F32)).astype(BF16)
    dot = lambda a: jnp.dot(a, e_bf16, preferred_element_type=F32)
    return dot(hi) + dot(mid) + dot(lo)


def _ssd_kernel(z_ref, xbc_ref, prev_ref, dt_ref, cw_ref, cb_ref, dtb_ref, alog_ref, dexp_ref, nw_ref, e_ref,
                o_ref, state_sc, *, d_inner):
    c = pl.program_id(1)
    L = SSM_CHUNK
    gw = d_inner // SSM_GROUPS
    hpg = gw // SSM_HEAD_DIM
    pad = SUBLANES

    @pl.when(c == 0)
    def _():
        state_sc[...] = jnp.zeros_like(state_sc)

    has_prev = c > 0
    row_in_tile = lax.broadcasted_iota(jnp.int32, (pad, 1), 0)

    def conv_silu(col0, width):
        cols = slice(col0, col0 + width)
        x = xbc_ref[0, :, cols]
        tail = jnp.where(has_prev, prev_ref[0, :, cols], 0.0)
        acc = cb_ref[:, cols] + x * cw_ref[SSM_CONV - 1:SSM_CONV, cols]
        for j in range(1, SSM_CONV):
            xr = pltpu.roll(x, j, 0)
            head = jnp.where(row_in_tile < j, pltpu.roll(tail, j, 0), xr[0:pad, :])
            shifted = jnp.concatenate([head, xr[pad:, :]], axis=0)
            acc = acc + shifted * cw_ref[SSM_CONV - 1 - j:SSM_CONV - j, cols]
        return _silu(acc)

    dtv = dt_ref[0] + dtb_ref[...]
    dt = jnp.maximum(dtv, 0.0) + jnp.log1p(jnp.exp(-jnp.abs(dtv)))
    a = dt * (-jnp.exp(alog_ref[...]) * LOG2E)
    ri = lax.broadcasted_iota(jnp.int32, (L, L), 0)
    ci = lax.broadcasted_iota(jnp.int32, (L, L), 1)
    causal = ri >= ci
    a_cum = jnp.dot(causal.astype(F32), a, precision=HIGHEST, preferred_element_type=F32)
    a_cum_t = a_cum.T
    e_mat = e_ref[...]
    acum_x = _expand_heads(a_cum, e_mat)
    dt_x = _expand_heads(dt, e_mat)
    lane = lax.broadcasted_iota(jnp.int32, (L, LANES), 1)
    first_head = lane < SSM_HEAD_DIM

    for g in range(SSM_GROUPS):
        cols = slice(g * gw, (g + 1) * gw)
        xg = conv_silu(g * gw, gw)
        bg = conv_silu(d_inner + g * SSM_STATE, SSM_STATE)
        cg = conv_silu(d_inner + (SSM_GROUPS + g) * SSM_STATE, SSM_STATE)
        ac = acum_x[:, cols]
        last = ac[L - 1:L, :]
        xdt = xg * dt_x[:, cols]
        cb16 = cg.astype(BF16)
        bb16 = bg.astype(BF16)
        cbm = lax.dot_general(cb16, bb16, (((1,), (1,)), ((), ())), preferred_element_type=F32)
        st = state_sc[g]
        y_off = jnp.dot(cb16, st.astype(BF16), preferred_element_type=F32) * jnp.exp2(ac)
        xd = (xdt * jnp.exp2(last - ac)).astype(BF16)
        state_sc[g] = st * jnp.exp2(last) + jnp.dot(bg.T.astype(BF16), xd, preferred_element_type=F32)
        xdt16 = xdt.astype(BF16)
        pieces = []
        for jp in range(hpg // 2):
            lhs = []
            for hh in (g * hpg + 2 * jp, g * hpg + 2 * jp + 1):
                seg = jnp.where(causal, a_cum[:, hh:hh + 1] - a_cum_t[hh:hh + 1, :], NEG)
                lhs.append((cbm * jnp.exp2(seg)).astype(BF16))
            xp = xdt16[:, jp * LANES:(jp + 1) * LANES]
            zero = jnp.zeros_like(xp)
            rhs = jnp.concatenate([jnp.where(first_head, xp, zero), jnp.where(first_head, zero, xp)], axis=0)
            pieces.append(jnp.dot(jnp.concatenate(lhs, axis=1), rhs, preferred_element_type=F32))
        y = jnp.concatenate(pieces, axis=1) + y_off + dexp_ref[:, cols] * xg
        y = y * _silu(z_ref[0, :, cols])
        y = y * lax.rsqrt(jnp.mean(y * y, axis=1, keepdims=True) + RMS_EPS) * nw_ref[:, cols]
        o_ref[0, :, cols] = y.astype(o_ref.dtype)


def _ssd(z3, xbc3, dt3, conv_w, conv_b, dt_bias_p, a_log_p, d_exp, norm_w, e_mat):
    b, s, d_inner = z3.shape
    cdim = xbc3.shape[2]
    nc = s // SSM_CHUNK
    gw = d_inner // SSM_GROUPS
    rows_per_prev = SSM_CHUNK // SUBLANES
    full = lambda shape: pl.BlockSpec(shape, lambda bi, c: tuple(0 for _ in shape))
    return pl.pallas_call(
        functools.partial(_ssd_kernel, d_inner=d_inner),
        grid=(b, nc),
        in_specs=[pl.BlockSpec((1, SSM_CHUNK, d_inner), lambda bi, c: (bi, c, 0)),
                  pl.BlockSpec((1, SSM_CHUNK, cdim), lambda bi, c: (bi, c, 0)),
                  pl.BlockSpec((1, SUBLANES, cdim), lambda bi, c: (bi, jnp.maximum(c * rows_per_prev - 1, 0), 0)),
                  pl.BlockSpec((1, SSM_CHUNK, LANES), lambda bi, c: (bi, c, 0)),
                  full((SSM_CONV, cdim)), full((1, cdim)), full((1, LANES)), full((1, LANES)),
                  full((1, d_inner)), full((1, d_inner)), full((LANES, d_inner))],
        out_specs=pl.BlockSpec((1, SSM_CHUNK, d_inner), lambda bi, c: (bi, c, 0)),
        out_shape=jax.ShapeDtypeStruct((b, s, d_inner), BF16),
        scratch_shapes=[pltpu.VMEM((SSM_GROUPS, SSM_STATE, gw), F32)],
        compiler_params=_params(("parallel", "arbitrary"), 56),
        name="ssd",
    )(z3, xbc3, xbc3, dt3, conv_w, conv_b, dt_bias_p, a_log_p, d_exp, norm_w, e_mat)


def kernel(x, positions, attn_w_qkv, attn_w_o, ssm_w_in, ssm_conv_w, ssm_conv_b, ssm_dt_bias, ssm_A_log, ssm_D, ssm_norm_w, ssm_w_out, router_w, router_bias, moe_w_gate, moe_w_up, moe_w_down, ln_mix_g, ln_mix_b, ln_ffn_g, ln_ffn_b):
    bsz, seq, d_model = x.shape
    t = bsz * seq
    depth = moe_w_gate.shape[0]
    alpha = (2 * depth) ** 0.25
    n_heads = d_model // HEAD_DIM
    patterns = ((128, 1), (512, 4), (2048, 16))
    attn_blk = 512

    half = ROT_DIM // 2
    inv_freq = ROPE_THETA ** (-jnp.arange(0, ROT_DIM, 2, dtype=F32) / ROT_DIM)
    ang = positions.astype(F32).reshape(t, 1) * inv_freq
    cos, sin = jnp.cos(ang), jnp.sin(ang)
    zeros_r = jnp.zeros((t, LANES - ROT_DIM), F32)
    cos_t = jnp.concatenate([cos, cos, jnp.ones((t, LANES - ROT_DIM), F32)], axis=1)
    sa_t = jnp.concatenate([-sin, jnp.zeros((t, half), F32), zeros_r], axis=1)
    sb_t = jnp.concatenate([jnp.zeros((t, half), F32), sin, zeros_r], axis=1)
    bias = _branch_count_bias(seq, attn_blk, patterns)

    bias_col = router_bias.reshape(N_EXPERTS, 1).astype(F32)

    h = x.reshape(t, d_model)
    hb = h.astype(BF16)
    for i in range(depth):
        j = i // 2
        if i % 2 == 0:
            qkv = _qkv_rope(hb, attn_w_qkv[j].astype(BF16), cos_t, sa_t, sb_t, d_model)
            att = _attention(qkv.reshape(bsz, seq, 3 * d_model), bias, n_heads, attn_blk)
            mix, w_mix = att.reshape(t, d_model), attn_w_o[j]
        else:
            d_inner = ssm_norm_w.shape[1]
            n_ssm_heads = ssm_dt_bias.shape[1]
            cdim = ssm_conv_w.shape[2]
            w_in = ssm_w_in[j].astype(BF16)
            z = _mm(hb, w_in, 0, d_inner, F32)
            xbc = _mm(hb, w_in, d_inner, cdim, F32)
            w_dt = jnp.pad(w_in[:, d_inner + cdim:], ((0, 0), (0, LANES - n_ssm_heads)))
            dt = _mm(hb, w_dt, 0, LANES, F32)
            hpad = (0, LANES - n_ssm_heads)
            head_of_ch = jnp.arange(d_inner, dtype=jnp.int32) // SSM_HEAD_DIM
            e_mat = (jnp.arange(LANES, dtype=jnp.int32)[:, None] == head_of_ch[None, :]).astype(BF16)
            y = _ssd(z.reshape(bsz, seq, d_inner), xbc.reshape(bsz, seq, cdim), dt.reshape(bsz, seq, LANES),
                     ssm_conv_w[j], ssm_conv_b[j].reshape(1, cdim),
                     jnp.pad(ssm_dt_bias[j], hpad).reshape(1, LANES), jnp.pad(ssm_A_log[j], hpad).reshape(1, LANES),
                     jnp.repeat(ssm_D[j], SSM_HEAD_DIM).reshape(1, d_inner), ssm_norm_w[j].reshape(1, d_inner), e_mat)
            mix, w_mix = y.reshape(t, d_inner), ssm_w_out[j]
        h, hp, e_idx, gate = _mm_res_ln_route(mix, w_mix.astype(BF16), h, ln_mix_g[i], ln_mix_b[i],
                                              router_w, bias_col, alpha)
        h, hb = _moe_layer(h, hp, e_idx, gate, moe_w_gate, moe_w_up, moe_w_down, i,
                           ln_ffn_g[i], ln_ffn_b[i], alpha)
    return h.reshape(bsz, seq, d_model)
```

```python
import functools
import math

import jax
import jax.numpy as jnp
from jax import lax
from jax.experimental import pallas as pl
from jax.experimental.pallas import tpu as pltpu

F32 = jnp.float32
BF16 = jnp.bfloat16
HIGHEST = lax.Precision.HIGHEST

LANES = 128
SUBLANES = 8
MXU_COLS = 256
MIB = 1024 * 1024

HEAD_DIM = 128
ROT_DIM = HEAD_DIM // 4
ROPE_THETA = 500000.0
N_EXPERTS = 64
N_GROUPS = 8
EPG = N_EXPERTS // N_GROUPS
TOP_K = 2
MOE_BLOCK = 128
SSM_HEAD_DIM = 64
SSM_GROUPS = 8
SSM_STATE = 128
SSM_CONV = 4
SSM_CHUNK = 128
LN_EPS = 1e-5
RMS_EPS = 1e-5
NEG = -1e30
LOG2E = math.log2(math.e)


def _params(sem, vmem_mib):
    return pltpu.CompilerParams(dimension_semantics=sem, vmem_limit_bytes=vmem_mib * MIB)


def _silu(x):
    hx = 0.5 * x
    return hx + hx * jnp.tanh(hx)


def _layer_norm_rows(y, g, b):
    mu = jnp.mean(y, axis=-1, keepdims=True)
    yc = y - mu
    var = jnp.mean(yc * yc, axis=-1, keepdims=True)
    return yc * lax.rsqrt(var + LN_EPS) * g + b


def _mm_kernel(a_ref, w_ref, o_ref):
    o_ref[...] = jnp.dot(a_ref[...], w_ref[...], preferred_element_type=F32).astype(o_ref.dtype)


def _mm(a, w, col0, n, out_dtype, tm=1024, tn=512):
    m, k = a.shape
    tn = min(tn, n)
    assert col0 % tn == 0 and n % tn == 0 and m % tm == 0
    c0 = col0 // tn
    return pl.pallas_call(
        _mm_kernel,
        grid=(m // tm, n // tn),
        in_specs=[pl.BlockSpec((tm, k), lambda i, j: (i, 0)),
                  pl.BlockSpec((k, tn), lambda i, j: (0, c0 + j))],
        out_specs=pl.BlockSpec((tm, tn), lambda i, j: (i, j)),
        out_shape=jax.ShapeDtypeStruct((m, n), out_dtype),
        compiler_params=_params(("parallel", "arbitrary"), 48),
        name="mm",
    )(a, w)


def _qkv_rope_kernel(x_ref, w_ref, cos_ref, sa_ref, sb_ref, o_ref, *, n_rope_tiles, n_q_tiles, scale):
    n = pl.program_id(1)
    tn = w_ref.shape[1]

    @pl.when(n < n_rope_tiles)
    def _():
        c = cos_ref[...]
        sa = sa_ref[...]
        sb = sb_ref[...]
        sc = jnp.where(n < n_q_tiles, scale, 1.0).astype(F32)
        for p in range(tn // MXU_COLS):
            t2 = jnp.dot(x_ref[...], w_ref[:, p * MXU_COLS:(p + 1) * MXU_COLS], preferred_element_type=F32)
            for j in range(MXU_COLS // LANES):
                t = t2[:, j * LANES:(j + 1) * LANES]
                r = t * c + pltpu.roll(t, LANES - ROT_DIM // 2, 1) * sa + pltpu.roll(t, ROT_DIM // 2, 1) * sb
                col = p * MXU_COLS + j * LANES
                o_ref[:, col:col + LANES] = (r * sc).astype(o_ref.dtype)

    @pl.when(n >= n_rope_tiles)
    def _():
        o_ref[...] = jnp.dot(x_ref[...], w_ref[...], preferred_element_type=F32).astype(o_ref.dtype)


def _qkv_rope(xb, w, cos_t, sa_t, sb_t, d_model, tm=1024, tn=512):
    m, k = xb.shape
    n = w.shape[1]
    kern = functools.partial(_qkv_rope_kernel, n_rope_tiles=2 * d_model // tn, n_q_tiles=d_model // tn,
                             scale=HEAD_DIM ** -0.5 * LOG2E)
    tab = pl.BlockSpec((tm, LANES), lambda i, j: (i, 0))
    return pl.pallas_call(
        kern,
        grid=(m // tm, n // tn),
        in_specs=[pl.BlockSpec((tm, k), lambda i, j: (i, 0)),
                  pl.BlockSpec((k, tn), lambda i, j: (0, j)),
                  tab, tab, tab],
        out_specs=pl.BlockSpec((tm, tn), lambda i, j: (i, j)),
        out_shape=jax.ShapeDtypeStruct((m, n), BF16),
        compiler_params=_params(("parallel", "arbitrary"), 48),
        name="qkv_rope",
    )(xb, w, cos_t, sa_t, sb_t)


def _attn_kernel(qi_ref, ki_ref, q_ref, k_ref, v_ref, bias_ref, o_ref, m_sc, l_sc, acc_sc, *, heads):
    p = pl.program_id(2)
    qi = qi_ref[p]
    ki = ki_ref[p]
    blk = q_ref.shape[1]

    @pl.when(ki == 0)
    def _():
        m_sc[...] = jnp.full_like(m_sc, 0.1 * NEG)
        l_sc[...] = jnp.zeros_like(l_sc)
        acc_sc[...] = jnp.zeros_like(acc_sc)

    bias = bias_ref[qi - ki]
    for j in range(heads):
        cols = slice(j * HEAD_DIM, (j + 1) * HEAD_DIM)
        s = lax.dot_general(q_ref[0, :, cols], k_ref[0, :, cols], (((1,), (1,)), ((), ())),
                            preferred_element_type=F32) + bias
        m_prev = m_sc[j]
        m_new = jnp.maximum(m_prev, jnp.max(s, axis=1, keepdims=True))
        alpha = jnp.exp2(m_prev - m_new)
        pr = jnp.exp2(s - jnp.concatenate([m_new] * (blk // LANES), axis=1))
        l_sc[j] = alpha * l_sc[j] + jnp.sum(pr, axis=1, keepdims=True)
        acc_sc[j] = alpha * acc_sc[j] + jnp.dot(pr.astype(BF16), v_ref[0, :, cols], preferred_element_type=F32)
        m_sc[j] = m_new

    @pl.when(ki == qi)
    def _():
        for j in range(heads):
            o_ref[0, :, j * HEAD_DIM:(j + 1) * HEAD_DIM] = (acc_sc[j] / l_sc[j]).astype(o_ref.dtype)


def _attention(qkv3, bias, n_heads, blk, heads=8):
    b, s, _ = qkv3.shape
    nq = s // blk
    pairs = [(qi, ki) for qi in range(nq) for ki in range(qi + 1)]
    qi_tab = jnp.asarray([pq for pq, _ in pairs], jnp.int32)
    ki_tab = jnp.asarray([pk for _, pk in pairs], jnp.int32)
    hg = n_heads // heads
    w = heads * HEAD_DIM
    return pl.pallas_call(
        functools.partial(_attn_kernel, heads=heads),
        grid_spec=pltpu.PrefetchScalarGridSpec(
            num_scalar_prefetch=2,
            grid=(b, hg, len(pairs)),
            in_specs=[pl.BlockSpec((1, blk, w), lambda bi, h, p, qt, kt: (bi, qt[p], h)),
                      pl.BlockSpec((1, blk, w), lambda bi, h, p, qt, kt: (bi, kt[p], hg + h)),
                      pl.BlockSpec((1, blk, w), lambda bi, h, p, qt, kt: (bi, kt[p], 2 * hg + h)),
                      pl.BlockSpec((nq, blk, blk), lambda bi, h, p, qt, kt: (0, 0, 0))],
            out_specs=pl.BlockSpec((1, blk, w), lambda bi, h, p, qt, kt: (bi, qt[p], h)),
            scratch_shapes=[pltpu.VMEM((heads, blk, LANES), F32), pltpu.VMEM((heads, blk, LANES), F32),
                            pltpu.VMEM((heads, blk, HEAD_DIM), F32)],
        ),
        out_shape=jax.ShapeDtypeStruct((b, s, n_heads * HEAD_DIM), BF16),
        compiler_params=_params(("parallel", "parallel", "arbitrary"), 48),
        name="attention",
    )(qi_tab, ki_tab, qkv3, qkv3, qkv3, bias)


def _branch_count_bias(s, blk, patterns):
    nq = s // blk
    off = jnp.arange(nq, dtype=jnp.int32)[:, None, None] * blk
    d = off + jnp.arange(blk, dtype=jnp.int32)[None, :, None] - jnp.arange(blk, dtype=jnp.int32)[None, None, :]
    cnt = jnp.zeros(d.shape, F32)
    for window, dilation in patterns:
        cnt = cnt + ((d >= 0) & (d % dilation == 0) & (d <= window)).astype(F32)
    return jnp.where(cnt > 0, jnp.log2(jnp.maximum(cnt, 1.0)), NEG)


def _pack_bf16_pairs(x):
    n = x.shape[1] // 2
    lo = lax.bitcast_convert_type(x[:, :n].astype(BF16).astype(F32), jnp.uint32)
    hi = lax.bitcast_convert_type(x[:, n:].astype(BF16).astype(F32), jnp.uint32)
    return (lo >> 16) | hi


def _unpack_bf16_pairs(w):
    lo = lax.bitcast_convert_type(w << 16, F32)
    hi = lax.bitcast_convert_type(w & jnp.uint32(0xFFFF0000), F32)
    return jnp.concatenate([lo, hi], axis=1)


def _mm_res_ln_route_kernel(a_ref, w_ref, h_ref, g_ref, b_ref, rwh_ref, rwl_ref, rb_ref,
                            of_ref, hp_ref, e_ref, gt_ref, *, alpha, n_sub):
    sub = a_ref.shape[0] // n_sub
    for r in range(n_sub):
        rows = slice(r * sub, (r + 1) * sub)
        mix = jnp.dot(a_ref[rows, :], w_ref[...], preferred_element_type=F32)
        out = _layer_norm_rows(alpha * h_ref[rows, :] + mix, g_ref[...], b_ref[...])
        of_ref[rows, :] = out
        hp_ref[rows, :] = _pack_bf16_pairs(out)
        experts, gates = _route_top2(out, rwh_ref[...], rwl_ref[...], rb_ref[...])
        for k in range(TOP_K):
            e_ref[k:k + 1, rows] = experts[k]
            gt_ref[k:k + 1, rows] = gates[k]


def _mm_res_ln_route(a, w, h, g, b, router_w, bias_col, alpha, tm=512, n_sub=2):
    m, kdim = a.shape
    n = w.shape[1]
    kern = functools.partial(_mm_res_ln_route_kernel, alpha=alpha, n_sub=n_sub)
    row = pl.BlockSpec((tm, n), lambda i: (i, 0))
    vec = pl.BlockSpec((1, n), lambda i: (0, 0))
    sel = pl.BlockSpec((TOP_K, tm), lambda i: (0, i))
    rw = jnp.pad(router_w, ((0, 0), (0, LANES - N_EXPERTS)))
    rw_hi = rw.astype(BF16)
    rw_lo = (rw - rw_hi.astype(F32)).astype(BF16)
    rw_spec = pl.BlockSpec((n, LANES), lambda i: (0, 0))
    return pl.pallas_call(
        kern,
        grid=(m // tm,),
        in_specs=[pl.BlockSpec((tm, kdim), lambda i: (i, 0)),
                  pl.BlockSpec((kdim, n), lambda i: (0, 0), pipeline_mode=pl.Buffered(1)),
                  row, vec, vec,
                  rw_spec, rw_spec, pl.BlockSpec((N_EXPERTS, 1), lambda i: (0, 0))],
        out_specs=[row, pl.BlockSpec((tm, n // 2), lambda i: (i, 0)), sel, sel],
        out_shape=[jax.ShapeDtypeStruct((m, n), F32), jax.ShapeDtypeStruct((m, n // 2), jnp.uint32),
                   jax.ShapeDtypeStruct((TOP_K, m), jnp.int32), jax.ShapeDtypeStruct((TOP_K, m), F32)],
        compiler_params=_params(("parallel",), 56),
        name="mm_res_ln_route",
    )(a, w, h, g.reshape(1, n), b.reshape(1, n), rw_hi, rw_lo, bias_col)


def _route_top2(h, w_hi, w_lo, bias_col):
    h_hi = h.astype(BF16)
    h_lo = (h - h_hi.astype(F32)).astype(BF16)
    dot = lambda a, b: jnp.dot(a, b, preferred_element_type=F32)
    logits = (dot(h_hi, w_hi) + dot(h_lo, w_hi) + dot(h_hi, w_lo)).T[:N_EXPERTS, :]
    mx = jnp.max(logits, axis=0, keepdims=True)
    ex = jnp.exp(logits - mx)
    scores = ex / jnp.sum(ex, axis=0, keepdims=True)
    sel = scores + bias_col
    tm = sel.shape[1]
    li = lax.broadcasted_iota(jnp.int32, (EPG, tm), 0)

    def first_argmax(v):
        m = jnp.max(v, axis=0, keepdims=True)
        return m, jnp.min(jnp.where(v == m, li, EPG), axis=0, keepdims=True)

    gs, i1s, i2s = [], [], []
    for g in range(N_GROUPS):
        slab = sel[g * EPG:(g + 1) * EPG, :]
        m1, i1 = first_argmax(slab)
        m2, i2 = first_argmax(jnp.where(li == i1, NEG, slab))
        gs.append(m1 + m2)
        i1s.append(i1)
        i2s.append(i2)
    best, gidx = gs[0], jnp.zeros((1, tm), jnp.int32)
    for g in range(1, N_GROUPS):
        better = gs[g] > best
        best = jnp.where(better, gs[g], best)
        gidx = jnp.where(better, g, gidx)
    l1 = jnp.zeros((1, tm), jnp.int32)
    l2 = jnp.zeros((1, tm), jnp.int32)
    s1 = jnp.zeros((1, tm), F32)
    s2 = jnp.zeros((1, tm), F32)
    for g in range(N_GROUPS):
        here = gidx == g
        slab = scores[g * EPG:(g + 1) * EPG, :]
        l1 = jnp.where(here, i1s[g], l1)
        l2 = jnp.where(here, i2s[g], l2)
        s1 = jnp.where(here, jnp.sum(jnp.where(li == i1s[g], slab, 0.0), axis=0, keepdims=True), s1)
        s2 = jnp.where(here, jnp.sum(jnp.where(li == i2s[g], slab, 0.0), axis=0, keepdims=True), s2)
    tot = s1 + s2
    return (gidx * EPG + l1, gidx * EPG + l2), (s1 / tot, s2 / tot)


N_ROW_BUFS = 2
N_SPARE_ROWS = N_ROW_BUFS * MOE_BLOCK
N_WEIGHT_SETS = 2
WEIGHT_DMA_PRIORITY = 1


def _experts_kernel(be_ref, wset_ref, ahead_ref, second_ref, nused_ref, src_ref, dst_ref,
                    h_hbm, wg_hbm, wu_hbm, wd_hbm, ys_hbm,
                    xb0, xb1, ob0, ob1, wgl, wul, wdl, wgb, wub, wdb, gsem, ssem, wsem, *, layer, n_slots):
    i = pl.program_id(0)
    n_used = nused_ref[0]
    xbufs = (xb0, xb1)
    obufs = (ob0, ob1)

    def gather_start(blk, s):
        base = (blk + 1) * MOE_BLOCK
        for r in range(MOE_BLOCK):
            pltpu.make_async_copy(h_hbm.at[pl.ds(src_ref[base + r], 1), :],
                                  xbufs[s].at[pl.ds(r, 1), :], gsem.at[s]).start()

    def scatter_start(blk, s):
        base = (blk + 1) * MOE_BLOCK
        for r in range(MOE_BLOCK):
            pltpu.make_async_copy(obufs[s].at[pl.ds(r, 1), :],
                                  ys_hbm.at[pl.ds(dst_ref[base + r], 1), :], ssem.at[s]).start()

    def gather_wait(s):
        pltpu.make_async_copy(h_hbm.at[pl.ds(0, MOE_BLOCK), :], xbufs[s], gsem.at[s]).wait()

    def scatter_wait(s):
        pltpu.make_async_copy(obufs[s], ys_hbm.at[pl.ds(0, MOE_BLOCK), :], ssem.at[s]).wait()

    w_hbm = (wg_hbm, wu_hbm, wd_hbm)
    w_land = (wgl, wul, wdl)
    w_work = (wgb, wub, wdb)

    def weight_copy(e, p, k):
        return pltpu.make_async_copy(w_hbm[k].at[layer, e], w_land[k].at[p], wsem.at[p, k])

    @pl.when(i == 0)
    def _():
        ob1[...] = jnp.zeros_like(ob1)
        spare = pltpu.make_async_copy(ob1, ys_hbm.at[pl.ds(n_slots + MOE_BLOCK, MOE_BLOCK), :], ssem.at[1])
        spare.start()
        spare.wait()
        gather_start(0, 0)
        for k in range(len(w_hbm)):
            weight_copy(be_ref[0], 0, k).start(priority=WEIGHT_DMA_PRIORITY)

        @pl.when(second_ref[0] >= 0)
        def _():
            for k in range(len(w_hbm)):
                weight_copy(second_ref[0], 1, k).start(priority=WEIGHT_DMA_PRIORITY)

    @pl.when(i < n_used)
    def _():
        for p in range(N_WEIGHT_SETS):
            @pl.when(wset_ref[i] == p)
            def _():
                ahead = ahead_ref[i]
                for k in range(len(w_hbm)):
                    weight_copy(be_ref[i], p, k).wait()
                    w_work[k][...] = w_land[k][p].astype(BF16)

                    @pl.when(ahead >= 0)
                    def _():
                        weight_copy(ahead, p, k).start(priority=WEIGHT_DMA_PRIORITY)

        for s in range(N_ROW_BUFS):
            @pl.when(i % N_ROW_BUFS == s)
            def _():
                gather_wait(s)

                @pl.when(i >= 1)
                def _():
                    scatter_wait(s)

                x = _unpack_bf16_pairs(xbufs[s][...]).astype(BF16)
                hg = jnp.dot(x, wgb[...], preferred_element_type=F32)
                hu = jnp.dot(x, wub[...], preferred_element_type=F32)
                act = (_silu(hg) * hu).astype(BF16)
                obufs[s][...] = _pack_bf16_pairs(jnp.dot(act, wdb[...], preferred_element_type=F32))
                gather_start(i + 1, 1 - s)
                scatter_start(i - 1, 1 - s)

                @pl.when(i == n_used - 1)
                def _():
                    gather_wait(1 - s)
                    scatter_wait(1 - s)
                    scatter_start(i, s)
                    scatter_wait(s)


def _experts(meta, hp, w_gate, w_up, w_down, layer, n_slots):
    d, de = w_gate.shape[2], w_gate.shape[3]
    nb = meta[0].shape[0]
    any_spec = pl.BlockSpec(memory_space=pl.ANY)
    rowbuf = pltpu.VMEM((MOE_BLOCK, hp.shape[1]), jnp.uint32)
    return pl.pallas_call(
        functools.partial(_experts_kernel, layer=layer, n_slots=n_slots),
        grid_spec=pltpu.PrefetchScalarGridSpec(
            num_scalar_prefetch=len(meta),
            grid=(nb,),
            in_specs=[any_spec, any_spec, any_spec, any_spec],
            out_specs=any_spec,
            scratch_shapes=[rowbuf, rowbuf, rowbuf, rowbuf,
                            pltpu.VMEM((N_WEIGHT_SETS, d, de), F32), pltpu.VMEM((N_WEIGHT_SETS, d, de), F32),
                            pltpu.VMEM((N_WEIGHT_SETS, de, d), F32),
                            pltpu.VMEM((d, de), BF16), pltpu.VMEM((d, de), BF16), pltpu.VMEM((de, d), BF16),
                            pltpu.SemaphoreType.DMA((N_ROW_BUFS,)), pltpu.SemaphoreType.DMA((N_ROW_BUFS,)),
                            pltpu.SemaphoreType.DMA((N_WEIGHT_SETS, 3))],
        ),
        out_shape=jax.ShapeDtypeStruct((n_slots + N_SPARE_ROWS, hp.shape[1]), jnp.uint32),
        compiler_params=_params(("arbitrary",), 56),
        name="experts",
    )(*meta, hp, w_gate, w_up, w_down)


def _combine_ln_kernel(y0_ref, y1_ref, gate_ref, h_ref, g_ref, b_ref, of_ref, ob_ref, *, alpha):
    gt = gate_ref[...]
    moe = gt[:, 0:1] * _unpack_bf16_pairs(y0_ref[...]) + gt[:, 1:2] * _unpack_bf16_pairs(y1_ref[...])
    out = _layer_norm_rows(alpha * h_ref[...] + moe, g_ref[...], b_ref[...])
    of_ref[...] = out
    ob_ref[...] = out.astype(BF16)


def _combine_ln(ys, gate_t, h, g, b, alpha, tm=256):
    t, d = h.shape
    row = pl.BlockSpec((tm, d), lambda i: (i, 0))
    vec = pl.BlockSpec((1, d), lambda i: (0, 0))
    dp = ys.shape[1]
    return pl.pallas_call(
        functools.partial(_combine_ln_kernel, alpha=alpha),
        grid=(t // tm,),
        in_specs=[pl.BlockSpec((tm, dp), lambda i: (i, 0)), pl.BlockSpec((tm, dp), lambda i: (t // tm + i, 0)),
                  pl.BlockSpec((tm, TOP_K), lambda i: (i, 0)), row, vec, vec],
        out_specs=[row, row],
        out_shape=[jax.ShapeDtypeStruct((t, d), F32), jax.ShapeDtypeStruct((t, d), BF16)],
        compiler_params=_params(("parallel",), 40),
        name="combine_ln",
    )(ys, ys, gate_t, h, g.reshape(1, d), b.reshape(1, d))


def _moe_layer(h, hp, e_idx, gate, w_gate, w_up, w_down, layer, ln_g, ln_b, alpha):
    t, d = h.shape
    n_slots = t * TOP_K
    e_flat = e_idx.reshape(n_slots)
    experts = jnp.arange(N_EXPERTS, dtype=jnp.int32)
    onehot = (e_flat[:, None] == experts[None, :]).astype(jnp.int32)
    csum = jnp.cumsum(onehot, axis=0)
    counts = csum[-1]
    padded = ((counts + MOE_BLOCK - 1) // MOE_BLOCK) * MOE_BLOCK
    pad_end = jnp.cumsum(padded)
    pad_start = pad_end - padded
    dest = jnp.sum(onehot * (csum - 1 + pad_start[None, :]), axis=1).astype(jnp.int32)
    n_blocks = n_slots // MOE_BLOCK + N_EXPERTS
    rows = n_blocks * MOE_BLOCK
    row_slot = jnp.full((rows,), -1, jnp.int32).at[dest].set(jnp.arange(n_slots, dtype=jnp.int32),
                                                             unique_indices=True)
    row_id = jnp.arange(-MOE_BLOCK, rows, dtype=jnp.int32)
    row_slot = jnp.concatenate([jnp.full((MOE_BLOCK,), -1, jnp.int32), row_slot])
    valid = row_slot >= 0
    row_src = jnp.where(valid, row_slot % t, row_id % t)
    spare = n_slots + ((row_id // MOE_BLOCK + 1) % N_ROW_BUFS) * MOE_BLOCK + row_id % MOE_BLOCK
    row_dst = jnp.where(valid, row_slot, spare)
    blk_row0 = jnp.arange(n_blocks, dtype=jnp.int32) * MOE_BLOCK
    block_e = jnp.clip(jnp.searchsorted(pad_end, blk_row0, side='right'), 0, N_EXPERTS - 1).astype(jnp.int32)
    first = jnp.concatenate([jnp.ones((1,), bool), block_e[1:] != block_e[:-1]])
    seq = jnp.concatenate([jnp.sort(jnp.where(counts > 0, experts, N_EXPERTS)),
                           jnp.full((N_WEIGHT_SETS,), N_EXPERTS, jnp.int32)])
    place = jnp.cumsum((counts > 0).astype(jnp.int32)) - 1
    blk_place = place[block_e]
    wset = jnp.where(first, blk_place % N_WEIGHT_SETS, -1).astype(jnp.int32)
    ahead = seq[blk_place + N_WEIGHT_SETS]
    ahead = jnp.where(ahead < N_EXPERTS, ahead, -1).astype(jnp.int32)
    second = jnp.where(seq[1] < N_EXPERTS, seq[1], -1).astype(jnp.int32).reshape(1)
    n_used = (pad_end[-1] // MOE_BLOCK).astype(jnp.int32).reshape(1)
    meta = (block_e, wset, ahead, second, n_used, row_src.astype(jnp.int32), row_dst.astype(jnp.int32))
    ys = _experts(meta, hp, w_gate, w_up, w_down, layer, n_slots)
    return _combine_ln(ys, gate.T, h, ln_g, ln_b, alpha)


def _expand_heads(x, e_bf16):
    hi = x.astype(BF16)
    r1 = x - hi.astype(F32)
    mid = r1.astype(BF16)
    lo = (r1 - mid.astype(F32)).astype(BF16)
    dot = lambda a: jnp.dot(a, e_bf16, preferred_element_type=F32)
    return dot(hi) + dot(mid) + dot(lo)


def _ssd_kernel(z_ref, xbc_ref, prev_ref, dt_ref, cw_ref, cb_ref, dtb_ref, alog_ref, dexp_ref, nw_ref, e_ref,
                o_ref, state_sc, *, d_inner):
    c = pl.program_id(1)
    L = SSM_CHUNK
    gw = d_inner // SSM_GROUPS
    hpg = gw // SSM_HEAD_DIM

    @pl.when(c == 0)
    def _():
        state_sc[...] = jnp.zeros_like(state_sc)

    has_prev = c > 0
    tile = SUBLANES
    row_in_tile = lax.broadcasted_iota(jnp.int32, (tile, 1), 0)

    def conv_silu(col0, width):
        cols = slice(col0, col0 + width)
        x = xbc_ref[0, :, cols].astype(F32)
        tail = jnp.where(has_prev, prev_ref[0, prev_ref.shape[1] - tile:, cols].astype(F32), 0.0)
        acc = cb_ref[:, cols] + x * cw_ref[SSM_CONV - 1:SSM_CONV, cols]
        for j in range(1, SSM_CONV):
            xr = pltpu.roll(x, j, 0)
            head = jnp.where(row_in_tile < j, pltpu.roll(tail, j, 0), xr[0:tile, :])
            shifted = jnp.concatenate([head, xr[tile:, :]], axis=0)
            acc = acc + shifted * cw_ref[SSM_CONV - 1 - j:SSM_CONV - j, cols]
        return _silu(acc)

    dtv = dt_ref[0] + dtb_ref[...]
    dt = jnp.maximum(dtv, 0.0) + jnp.log1p(jnp.exp(-jnp.abs(dtv)))
    a = dt * (-jnp.exp(alog_ref[...]) * LOG2E)
    ri = lax.broadcasted_iota(jnp.int32, (L, L), 0)
    ci = lax.broadcasted_iota(jnp.int32, (L, L), 1)
    causal = ri >= ci
    a_cum = jnp.dot(causal.astype(F32), a, precision=HIGHEST, preferred_element_type=F32)
    a_cum_t = a_cum.T
    e_mat = e_ref[...]
    acum_x = _expand_heads(a_cum, e_mat)
    dt_x = _expand_heads(dt, e_mat)
    lane = lax.broadcasted_iota(jnp.int32, (L, LANES), 1)
    first_head = lane < SSM_HEAD_DIM

    for g in range(SSM_GROUPS):
        cols = slice(g * gw, (g + 1) * gw)
        xg = conv_silu(g * gw, gw)
        bg = conv_silu(d_inner + g * SSM_STATE, SSM_STATE)
        cg = conv_silu(d_inner + (SSM_GROUPS + g) * SSM_STATE, SSM_STATE)
        ac = acum_x[:, cols]
        last = ac[L - 1:L, :]
        xdt = xg * dt_x[:, cols]
        cb16 = cg.astype(BF16)
        bb16 = bg.astype(BF16)
        cbm = lax.dot_general(cb16, bb16, (((1,), (1,)), ((), ())), preferred_element_type=F32)
        st = state_sc[g]
        y_off = jnp.dot(cb16, st.astype(BF16), preferred_element_type=F32) * jnp.exp2(ac)
        xd = (xdt * jnp.exp2(last - ac)).astype(BF16)
        state_sc[g] = st * jnp.exp2(last) + jnp.dot(bg.T.astype(BF16), xd, preferred_element_type=F32)
        xdt16 = xdt.astype(BF16)
        pieces = []
        for jp in range(hpg // 2):
            lhs = []
            for hh in (g * hpg + 2 * jp, g * hpg + 2 * jp + 1):
                seg = jnp.where(causal, a_cum[:, hh:hh + 1] - a_cum_t[hh:hh + 1, :], NEG)
                lhs.append((cbm * jnp.exp2(seg)).astype(BF16))
            xp = xdt16[:, jp * LANES:(jp + 1) * LANES]
            zero = jnp.zeros_like(xp)
            rhs = jnp.concatenate([jnp.where(first_head, xp, zero), jnp.where(first_head, zero, xp)], axis=0)
            pieces.append(jnp.dot(jnp.concatenate(lhs, axis=1), rhs, preferred_element_type=F32))
        y = jnp.concatenate(pieces, axis=1) + y_off + dexp_ref[:, cols] * xg
        y = y * _silu(z_ref[0, :, cols].astype(F32))
        y = y * lax.rsqrt(jnp.mean(y * y, axis=1, keepdims=True) + RMS_EPS) * nw_ref[:, cols]
        o_ref[0, :, cols] = y.astype(o_ref.dtype)


def _ssd(z3, xbc3, dt3, conv_w, conv_b, dt_bias_p, a_log_p, d_exp, norm_w, e_mat):
    b, s, d_inner = z3.shape
    cdim = xbc3.shape[2]
    nc = s // SSM_CHUNK
    gw = d_inner // SSM_GROUPS
    prev_rows = SUBLANES * (4 // xbc3.dtype.itemsize)
    rows_per_prev = SSM_CHUNK // prev_rows
    full = lambda shape: pl.BlockSpec(shape, lambda bi, c: tuple(0 for _ in shape))
    return pl.pallas_call(
        functools.partial(_ssd_kernel, d_inner=d_inner),
        grid=(b, nc),
        in_specs=[pl.BlockSpec((1, SSM_CHUNK, d_inner), lambda bi, c: (bi, c, 0)),
                  pl.BlockSpec((1, SSM_CHUNK, cdim), lambda bi, c: (bi, c, 0)),
                  pl.BlockSpec((1, prev_rows, cdim), lambda bi, c: (bi, jnp.maximum(c * rows_per_prev - 1, 0), 0)),
                  pl.BlockSpec((1, SSM_CHUNK, LANES), lambda bi, c: (bi, c, 0)),
                  full((SSM_CONV, cdim)), full((1, cdim)), full((1, LANES)), full((1, LANES)),
                  full((1, d_inner)), full((1, d_inner)), full((LANES, d_inner))],
        out_specs=pl.BlockSpec((1, SSM_CHUNK, d_inner), lambda bi, c: (bi, c, 0)),
        out_shape=jax.ShapeDtypeStruct((b, s, d_inner), BF16),
        scratch_shapes=[pltpu.VMEM((SSM_GROUPS, SSM_STATE, gw), F32)],
        compiler_params=_params(("parallel", "arbitrary"), 56),
        name="ssd",
    )(z3, xbc3, xbc3, dt3, conv_w, conv_b, dt_bias_p, a_log_p, d_exp, norm_w, e_mat)


def kernel(x, positions, attn_w_qkv, attn_w_o, ssm_w_in, ssm_conv_w, ssm_conv_b, ssm_dt_bias, ssm_A_log, ssm_D, ssm_norm_w, ssm_w_out, router_w, router_bias, moe_w_gate, moe_w_up, moe_w_down, ln_mix_g, ln_mix_b, ln_ffn_g, ln_ffn_b):
    bsz, seq, d_model = x.shape
    t = bsz * seq
    depth = moe_w_gate.shape[0]
    alpha = (2 * depth) ** 0.25
    n_heads = d_model // HEAD_DIM
    patterns = ((128, 1), (512, 4), (2048, 16))
    attn_blk = 512

    half = ROT_DIM // 2
    inv_freq = ROPE_THETA ** (-jnp.arange(0, ROT_DIM, 2, dtype=F32) / ROT_DIM)
    ang = positions.astype(F32).reshape(t, 1) * inv_freq
    cos, sin = jnp.cos(ang), jnp.sin(ang)
    zeros_r = jnp.zeros((t, LANES - ROT_DIM), F32)
    cos_t = jnp.concatenate([cos, cos, jnp.ones((t, LANES - ROT_DIM), F32)], axis=1)
    sa_t = jnp.concatenate([-sin, jnp.zeros((t, half), F32), zeros_r], axis=1)
    sb_t = jnp.concatenate([jnp.zeros((t, half), F32), sin, zeros_r], axis=1)
    bias = _branch_count_bias(seq, attn_blk, patterns)

    bias_col = router_bias.reshape(N_EXPERTS, 1).astype(F32)

    h = x.reshape(t, d_model)
    hb = h.astype(BF16)
    for i in range(depth):
        j = i // 2
        if i % 2 == 0:
            qkv = _qkv_rope(hb, attn_w_qkv[j].astype(BF16), cos_t, sa_t, sb_t, d_model)
            att = _attention(qkv.reshape(bsz, seq, 3 * d_model), bias, n_heads, attn_blk)
            mix, w_mix = att.reshape(t, d_model), attn_w_o[j]
        else:
            d_inner = ssm_norm_w.shape[1]
            n_ssm_heads = ssm_dt_bias.shape[1]
            cdim = ssm_conv_w.shape[2]
            w_in = ssm_w_in[j].astype(BF16)
            z = _mm(hb, w_in, 0, d_inner, BF16, tn=1024)
            xbc = _mm(hb, w_in, d_inner, cdim, BF16, tn=1024)
            w_dt = jnp.pad(w_in[:, d_inner + cdim:], ((0, 0), (0, LANES - n_ssm_heads)))
            dt = _mm(hb, w_dt, 0, LANES, F32)
            hpad = (0, LANES - n_ssm_heads)
            head_of_ch = jnp.arange(d_inner, dtype=jnp.int32) // SSM_HEAD_DIM
            e_mat = (jnp.arange(LANES, dtype=jnp.int32)[:, None] == head_of_ch[None, :]).astype(BF16)
            y = _ssd(z.reshape(bsz, seq, d_inner), xbc.reshape(bsz, seq, cdim), dt.reshape(bsz, seq, LANES),
                     ssm_conv_w[j], ssm_conv_b[j].reshape(1, cdim),
                     jnp.pad(ssm_dt_bias[j], hpad).reshape(1, LANES), jnp.pad(ssm_A_log[j], hpad).reshape(1, LANES),
                     jnp.repeat(ssm_D[j], SSM_HEAD_DIM).reshape(1, d_inner), ssm_norm_w[j].reshape(1, d_inner), e_mat)
            mix, w_mix = y.reshape(t, d_inner), ssm_w_out[j]
        h, hp, e_idx, gate = _mm_res_ln_route(mix, w_mix.astype(BF16), h, ln_mix_g[i], ln_mix_b[i],
                                              router_w, bias_col, alpha)
        h, hb = _moe_layer(h, hp, e_idx, gate, moe_w_gate, moe_w_up, moe_w_down, i,
                           ln_ffn_g[i], ln_ffn_b[i], alpha)
    return h.reshape(bsz, seq, d_model)
```

```python
import functools
import math

import jax
import jax.numpy as jnp
from jax import lax
from jax.experimental import pallas as pl
from jax.experimental.pallas import tpu as pltpu

F32 = jnp.float32
BF16 = jnp.bfloat16
HIGHEST = lax.Precision.HIGHEST

LANES = 128
SUBLANES = 8
MXU_COLS = 256
MIB = 1024 * 1024

HEAD_DIM = 128
ROT_DIM = HEAD_DIM // 4
ROPE_THETA = 500000.0
N_EXPERTS = 64
N_GROUPS = 8
EPG = N_EXPERTS // N_GROUPS
TOP_K = 2
MOE_BLOCK = 128
SSM_HEAD_DIM = 64
SSM_GROUPS = 8
SSM_STATE = 128
SSM_CONV = 4
SSM_CHUNK = 128
LN_EPS = 1e-5
RMS_EPS = 1e-5
NEG = -1e30
LOG2E = math.log2(math.e)


def _params(sem, vmem_mib):
    return pltpu.CompilerParams(dimension_semantics=sem, vmem_limit_bytes=vmem_mib * MIB)


def _silu(x):
    hx = 0.5 * x
    return hx + hx * jnp.tanh(hx)


def _layer_norm_rows(y, g, b):
    mu = jnp.mean(y, axis=-1, keepdims=True)
    yc = y - mu
    var = jnp.mean(yc * yc, axis=-1, keepdims=True)
    return yc * lax.rsqrt(var + LN_EPS) * g + b


def _mm_kernel(a_ref, w_ref, o_ref):
    o_ref[...] = jnp.dot(a_ref[...], w_ref[...], preferred_element_type=F32).astype(o_ref.dtype)


def _mm(a, w, col0, n, out_dtype, tm=1024, tn=512):
    m, k = a.shape
    tn = min(tn, n)
    assert col0 % tn == 0 and n % tn == 0 and m % tm == 0
    c0 = col0 // tn
    return pl.pallas_call(
        _mm_kernel,
        grid=(m // tm, n // tn),
        in_specs=[pl.BlockSpec((tm, k), lambda i, j: (i, 0)),
                  pl.BlockSpec((k, tn), lambda i, j: (0, c0 + j))],
        out_specs=pl.BlockSpec((tm, tn), lambda i, j: (i, j)),
        out_shape=jax.ShapeDtypeStruct((m, n), out_dtype),
        compiler_params=_params(("parallel", "arbitrary"), 48),
        name="mm",
    )(a, w)


def _qkv_rope_kernel(x32_ref, w_ref, cos_ref, sa_ref, sb_ref, o_ref, x_ref, *, n_rope_tiles, n_q_tiles, scale):
    n = pl.program_id(1)
    tn = w_ref.shape[1]

    @pl.when(n == 0)
    def _():
        x_ref[...] = x32_ref[...].astype(BF16)

    @pl.when(n < n_rope_tiles)
    def _():
        c = cos_ref[...]
        sa = sa_ref[...]
        sb = sb_ref[...]
        sc = jnp.where(n < n_q_tiles, scale, 1.0).astype(F32)
        for p in range(tn // MXU_COLS):
            t2 = jnp.dot(x_ref[...], w_ref[:, p * MXU_COLS:(p + 1) * MXU_COLS], preferred_element_type=F32)
            for j in range(MXU_COLS // LANES):
                t = t2[:, j * LANES:(j + 1) * LANES]
                r = t * c + pltpu.roll(t, LANES - ROT_DIM // 2, 1) * sa + pltpu.roll(t, ROT_DIM // 2, 1) * sb
                col = p * MXU_COLS + j * LANES
                o_ref[:, col:col + LANES] = (r * sc).astype(o_ref.dtype)

    @pl.when(n >= n_rope_tiles)
    def _():
        o_ref[...] = jnp.dot(x_ref[...], w_ref[...], preferred_element_type=F32).astype(o_ref.dtype)


def _qkv_rope(x, w, cos_t, sa_t, sb_t, d_model, tm=1024, tn=1024):
    m, k = x.shape
    n = w.shape[1]
    kern = functools.partial(_qkv_rope_kernel, n_rope_tiles=2 * d_model // tn, n_q_tiles=d_model // tn,
                             scale=HEAD_DIM ** -0.5 * LOG2E)
    tab = pl.BlockSpec((tm, LANES), lambda i, j: (i, 0))
    return pl.pallas_call(
        kern,
        grid=(m // tm, n // tn),
        in_specs=[pl.BlockSpec((tm, k), lambda i, j: (i, 0)),
                  pl.BlockSpec((k, tn), lambda i, j: (0, j)),
                  tab, tab, tab],
        out_specs=pl.BlockSpec((tm, tn), lambda i, j: (i, j)),
        out_shape=jax.ShapeDtypeStruct((m, n), BF16),
        scratch_shapes=[pltpu.VMEM((tm, k), BF16)],
        compiler_params=_params(("parallel", "arbitrary"), 56),
        name="qkv_rope",
    )(x, w, cos_t, sa_t, sb_t)


def _attn_kernel(qi_ref, ki_ref, q_ref, k_ref, v_ref, bias_ref, o_ref, m_sc, l_sc, acc_sc, *, heads):
    p = pl.program_id(2)
    qi = qi_ref[p]
    ki = ki_ref[p]
    blk = q_ref.shape[1]

    @pl.when(ki == 0)
    def _():
        m_sc[...] = jnp.full_like(m_sc, 0.1 * NEG)
        l_sc[...] = jnp.zeros_like(l_sc)
        acc_sc[...] = jnp.zeros_like(acc_sc)

    bias = bias_ref[qi - ki]
    for j in range(heads):
        cols = slice(j * HEAD_DIM, (j + 1) * HEAD_DIM)
        s = lax.dot_general(q_ref[0, :, cols], k_ref[0, :, cols], (((1,), (1,)), ((), ())),
                            preferred_element_type=F32) + bias
        m_prev = m_sc[j]
        m_new = jnp.maximum(m_prev, jnp.max(s, axis=1, keepdims=True))
        alpha = jnp.exp2(m_prev - m_new)
        pr = jnp.exp2(s - jnp.concatenate([m_new] * (blk // LANES), axis=1))
        l_sc[j] = alpha * l_sc[j] + jnp.sum(pr, axis=1, keepdims=True)
        acc_sc[j] = alpha * acc_sc[j] + jnp.dot(pr.astype(BF16), v_ref[0, :, cols], preferred_element_type=F32)
        m_sc[j] = m_new

    @pl.when(ki == qi)
    def _():
        for j in range(heads):
            o_ref[0, :, j * HEAD_DIM:(j + 1) * HEAD_DIM] = (acc_sc[j] / l_sc[j]).astype(o_ref.dtype)


def _attention(qkv3, bias, n_heads, blk, heads=8):
    b, s, _ = qkv3.shape
    nq = s // blk
    pairs = [(qi, ki) for qi in range(nq) for ki in range(qi + 1)]
    qi_tab = jnp.asarray([pq for pq, _ in pairs], jnp.int32)
    ki_tab = jnp.asarray([pk for _, pk in pairs], jnp.int32)
    hg = n_heads // heads
    w = heads * HEAD_DIM
    return pl.pallas_call(
        functools.partial(_attn_kernel, heads=heads),
        grid_spec=pltpu.PrefetchScalarGridSpec(
            num_scalar_prefetch=2,
            grid=(b, hg, len(pairs)),
            in_specs=[pl.BlockSpec((1, blk, w), lambda bi, h, p, qt, kt: (bi, qt[p], h)),
                      pl.BlockSpec((1, blk, w), lambda bi, h, p, qt, kt: (bi, kt[p], hg + h)),
                      pl.BlockSpec((1, blk, w), lambda bi, h, p, qt, kt: (bi, kt[p], 2 * hg + h)),
                      pl.BlockSpec((nq, blk, blk), lambda bi, h, p, qt, kt: (0, 0, 0))],
            out_specs=pl.BlockSpec((1, blk, w), lambda bi, h, p, qt, kt: (bi, qt[p], h)),
            scratch_shapes=[pltpu.VMEM((heads, blk, LANES), F32), pltpu.VMEM((heads, blk, LANES), F32),
                            pltpu.VMEM((heads, blk, HEAD_DIM), F32)],
        ),
        out_shape=jax.ShapeDtypeStruct((b, s, n_heads * HEAD_DIM), BF16),
        compiler_params=_params(("parallel", "parallel", "arbitrary"), 48),
        name="attention",
    )(qi_tab, ki_tab, qkv3, qkv3, qkv3, bias)


def _branch_count_bias(s, blk, patterns):
    nq = s // blk
    off = jnp.arange(nq, dtype=jnp.int32)[:, None, None] * blk
    d = off + jnp.arange(blk, dtype=jnp.int32)[None, :, None] - jnp.arange(blk, dtype=jnp.int32)[None, None, :]
    cnt = jnp.zeros(d.shape, F32)
    for window, dilation in patterns:
        cnt = cnt + ((d >= 0) & (d % dilation == 0) & (d <= window)).astype(F32)
    return jnp.where(cnt > 0, jnp.log2(jnp.maximum(cnt, 1.0)), NEG)


def _pack_bf16_pairs(x):
    n = x.shape[1] // 2
    lo = lax.bitcast_convert_type(x[:, :n].astype(BF16).astype(F32), jnp.uint32)
    hi = lax.bitcast_convert_type(x[:, n:].astype(BF16).astype(F32), jnp.uint32)
    return (lo >> 16) | hi


def _unpack_bf16_pairs(w):
    lo = lax.bitcast_convert_type(w << 16, F32)
    hi = lax.bitcast_convert_type(w & jnp.uint32(0xFFFF0000), F32)
    return jnp.concatenate([lo, hi], axis=1)


def _mm_res_ln_route_kernel(a_ref, w_ref, h_ref, g_ref, b_ref, rwh_ref, rwl_ref, rb_ref,
                            of_ref, hp_ref, e_ref, gt_ref, *, alpha, n_sub):
    sub = a_ref.shape[0] // n_sub
    for r in range(n_sub):
        rows = slice(r * sub, (r + 1) * sub)
        mix = jnp.dot(a_ref[rows, :], w_ref[...], preferred_element_type=F32)
        out = _layer_norm_rows(alpha * h_ref[rows, :] + mix, g_ref[...], b_ref[...])
        of_ref[rows, :] = out
        hp_ref[rows, :] = _pack_bf16_pairs(out)
        experts, gates = _route_top2(out, rwh_ref[...], rwl_ref[...], rb_ref[...])
        for k in range(TOP_K):
            e_ref[k:k + 1, rows] = experts[k]
            gt_ref[k:k + 1, rows] = gates[k]


def _mm_res_ln_route(a, w, h, g, b, router_w, bias_col, alpha, tm=512, n_sub=2):
    m, kdim = a.shape
    n = w.shape[1]
    kern = functools.partial(_mm_res_ln_route_kernel, alpha=alpha, n_sub=n_sub)
    row = pl.BlockSpec((tm, n), lambda i: (i, 0))
    vec = pl.BlockSpec((1, n), lambda i: (0, 0))
    sel = pl.BlockSpec((TOP_K, tm), lambda i: (0, i))
    rw = jnp.pad(router_w, ((0, 0), (0, LANES - N_EXPERTS)))
    rw_hi = rw.astype(BF16)
    rw_lo = (rw - rw_hi.astype(F32)).astype(BF16)
    rw_spec = pl.BlockSpec((n, LANES), lambda i: (0, 0))
    return pl.pallas_call(
        kern,
        grid=(m // tm,),
        in_specs=[pl.BlockSpec((tm, kdim), lambda i: (i, 0)),
                  pl.BlockSpec((kdim, n), lambda i: (0, 0), pipeline_mode=pl.Buffered(1)),
                  row, vec, vec,
                  rw_spec, rw_spec, pl.BlockSpec((N_EXPERTS, 1), lambda i: (0, 0))],
        out_specs=[row, pl.BlockSpec((tm, n // 2), lambda i: (i, 0)), sel, sel],
        out_shape=[jax.ShapeDtypeStruct((m, n), F32), jax.ShapeDtypeStruct((m, n // 2), jnp.uint32),
                   jax.ShapeDtypeStruct((TOP_K, m), jnp.int32), jax.ShapeDtypeStruct((TOP_K, m), F32)],
        compiler_params=_params(("parallel",), 56),
        name="mm_res_ln_route",
    )(a, w, h, g.reshape(1, n), b.reshape(1, n), rw_hi, rw_lo, bias_col)


def _route_top2(h, w_hi, w_lo, bias_col):
    h_hi = h.astype(BF16)
    h_lo = (h - h_hi.astype(F32)).astype(BF16)
    dot = lambda a, b: jnp.dot(a, b, preferred_element_type=F32)
    logits = (dot(h_hi, w_hi) + dot(h_lo, w_hi) + dot(h_hi, w_lo)).T[:N_EXPERTS, :]
    mx = jnp.max(logits, axis=0, keepdims=True)
    ex = jnp.exp(logits - mx)
    scores = ex / jnp.sum(ex, axis=0, keepdims=True)
    sel = scores + bias_col
    tm = sel.shape[1]
    li = lax.broadcasted_iota(jnp.int32, (EPG, tm), 0)

    def first_argmax(v):
        m = jnp.max(v, axis=0, keepdims=True)
        return m, jnp.min(jnp.where(v == m, li, EPG), axis=0, keepdims=True)

    gs, i1s, i2s = [], [], []
    for g in range(N_GROUPS):
        slab = sel[g * EPG:(g + 1) * EPG, :]
        m1, i1 = first_argmax(slab)
        m2, i2 = first_argmax(jnp.where(li == i1, NEG, slab))
        gs.append(m1 + m2)
        i1s.append(i1)
        i2s.append(i2)
    best, gidx = gs[0], jnp.zeros((1, tm), jnp.int32)
    for g in range(1, N_GROUPS):
        better = gs[g] > best
        best = jnp.where(better, gs[g], best)
        gidx = jnp.where(better, g, gidx)
    l1 = jnp.zeros((1, tm), jnp.int32)
    l2 = jnp.zeros((1, tm), jnp.int32)
    s1 = jnp.zeros((1, tm), F32)
    s2 = jnp.zeros((1, tm), F32)
    for g in range(N_GROUPS):
        here = gidx == g
        slab = scores[g * EPG:(g + 1) * EPG, :]
        l1 = jnp.where(here, i1s[g], l1)
        l2 = jnp.where(here, i2s[g], l2)
        s1 = jnp.where(here, jnp.sum(jnp.where(li == i1s[g], slab, 0.0), axis=0, keepdims=True), s1)
        s2 = jnp.where(here, jnp.sum(jnp.where(li == i2s[g], slab, 0.0), axis=0, keepdims=True), s2)
    tot = s1 + s2
    return (gidx * EPG + l1, gidx * EPG + l2), (s1 / tot, s2 / tot)


N_ROW_BUFS = 2
N_SPARE_ROWS = N_ROW_BUFS * MOE_BLOCK
N_WEIGHT_SETS = 2
WEIGHT_DMA_PRIORITY = 1


def _experts_kernel(be_ref, wset_ref, ahead_ref, second_ref, nused_ref, src_ref, dst_ref,
                    h_hbm, wg_hbm, wu_hbm, wd_hbm, ys_hbm,
                    xb0, xb1, ob0, ob1, wgl, wul, wdl, wgb, wub, wdb, gsem, ssem, wsem, *, layer, n_slots):
    i = pl.program_id(0)
    n_used = nused_ref[0]
    xbufs = (xb0, xb1)
    obufs = (ob0, ob1)

    def gather_start(blk, s):
        base = (blk + 1) * MOE_BLOCK
        for r in range(MOE_BLOCK):
            pltpu.make_async_copy(h_hbm.at[pl.ds(src_ref[base + r], 1), :],
                                  xbufs[s].at[pl.ds(r, 1), :], gsem.at[s]).start()

    def scatter_start(blk, s):
        base = (blk + 1) * MOE_BLOCK
        for r in range(MOE_BLOCK):
            pltpu.make_async_copy(obufs[s].at[pl.ds(r, 1), :],
                                  ys_hbm.at[pl.ds(dst_ref[base + r], 1), :], ssem.at[s]).start()

    def gather_wait(s):
        pltpu.make_async_copy(h_hbm.at[pl.ds(0, MOE_BLOCK), :], xbufs[s], gsem.at[s]).wait()

    def scatter_wait(s):
        pltpu.make_async_copy(obufs[s], ys_hbm.at[pl.ds(0, MOE_BLOCK), :], ssem.at[s]).wait()

    w_hbm = (wg_hbm, wu_hbm, wd_hbm)
    w_land = (wgl, wul, wdl)
    w_work = (wgb, wub, wdb)

    def weight_copy(e, p, k):
        return pltpu.make_async_copy(w_hbm[k].at[layer, e], w_land[k].at[p], wsem.at[p, k])

    @pl.when(i == 0)
    def _():
        ob1[...] = jnp.zeros_like(ob1)
        spare = pltpu.make_async_copy(ob1, ys_hbm.at[pl.ds(n_slots + MOE_BLOCK, MOE_BLOCK), :], ssem.at[1])
        spare.start()
        spare.wait()
        gather_start(0, 0)
        for k in range(len(w_hbm)):
            weight_copy(be_ref[0], 0, k).start(priority=WEIGHT_DMA_PRIORITY)

        @pl.when(second_ref[0] >= 0)
        def _():
            for k in range(len(w_hbm)):
                weight_copy(second_ref[0], 1, k).start(priority=WEIGHT_DMA_PRIORITY)

    @pl.when(i < n_used)
    def _():
        for p in range(N_WEIGHT_SETS):
            @pl.when(wset_ref[i] == p)
            def _():
                ahead = ahead_ref[i]
                for k in range(len(w_hbm)):
                    weight_copy(be_ref[i], p, k).wait()
                    w_work[k][...] = w_land[k][p].astype(BF16)

                    @pl.when(ahead >= 0)
                    def _():
                        weight_copy(ahead, p, k).start(priority=WEIGHT_DMA_PRIORITY)

        for s in range(N_ROW_BUFS):
            @pl.when(i % N_ROW_BUFS == s)
            def _():
                gather_wait(s)

                @pl.when(i >= 1)
                def _():
                    scatter_wait(s)

                x = _unpack_bf16_pairs(xbufs[s][...]).astype(BF16)
                hg = jnp.dot(x, wgb[...], preferred_element_type=F32)
                hu = jnp.dot(x, wub[...], preferred_element_type=F32)
                act = (_silu(hg) * hu).astype(BF16)
                obufs[s][...] = _pack_bf16_pairs(jnp.dot(act, wdb[...], preferred_element_type=F32))
                gather_start(i + 1, 1 - s)
                scatter_start(i - 1, 1 - s)

                @pl.when(i == n_used - 1)
                def _():
                    gather_wait(1 - s)
                    scatter_wait(1 - s)
                    scatter_start(i, s)
                    scatter_wait(s)


def _experts(meta, hp, w_gate, w_up, w_down, layer, n_slots):
    d, de = w_gate.shape[2], w_gate.shape[3]
    nb = meta[0].shape[0]
    any_spec = pl.BlockSpec(memory_space=pl.ANY)
    rowbuf = pltpu.VMEM((MOE_BLOCK, hp.shape[1]), jnp.uint32)
    return pl.pallas_call(
        functools.partial(_experts_kernel, layer=layer, n_slots=n_slots),
        grid_spec=pltpu.PrefetchScalarGridSpec(
            num_scalar_prefetch=len(meta),
            grid=(nb,),
            in_specs=[any_spec, any_spec, any_spec, any_spec],
            out_specs=any_spec,
            scratch_shapes=[rowbuf, rowbuf, rowbuf, rowbuf,
                            pltpu.VMEM((N_WEIGHT_SETS, d, de), F32), pltpu.VMEM((N_WEIGHT_SETS, d, de), F32),
                            pltpu.VMEM((N_WEIGHT_SETS, de, d), F32),
                            pltpu.VMEM((d, de), BF16), pltpu.VMEM((d, de), BF16), pltpu.VMEM((de, d), BF16),
                            pltpu.SemaphoreType.DMA((N_ROW_BUFS,)), pltpu.SemaphoreType.DMA((N_ROW_BUFS,)),
                            pltpu.SemaphoreType.DMA((N_WEIGHT_SETS, 3))],
        ),
        out_shape=jax.ShapeDtypeStruct((n_slots + N_SPARE_ROWS, hp.shape[1]), jnp.uint32),
        compiler_params=_params(("arbitrary",), 56),
        name="experts",
    )(*meta, hp, w_gate, w_up, w_down)


def _combine_ln_kernel(y0_ref, y1_ref, gate_ref, h_ref, g_ref, b_ref, of_ref, ob_ref, *, alpha):
    gt = gate_ref[...]
    moe = gt[:, 0:1] * _unpack_bf16_pairs(y0_ref[...]) + gt[:, 1:2] * _unpack_bf16_pairs(y1_ref[...])
    out = _layer_norm_rows(alpha * h_ref[...] + moe, g_ref[...], b_ref[...])
    of_ref[...] = out
    ob_ref[...] = out.astype(BF16)


def _combine_ln(ys, gate_t, h, g, b, alpha, tm=256):
    t, d = h.shape
    row = pl.BlockSpec((tm, d), lambda i: (i, 0))
    vec = pl.BlockSpec((1, d), lambda i: (0, 0))
    dp = ys.shape[1]
    return pl.pallas_call(
        functools.partial(_combine_ln_kernel, alpha=alpha),
        grid=(t // tm,),
        in_specs=[pl.BlockSpec((tm, dp), lambda i: (i, 0)), pl.BlockSpec((tm, dp), lambda i: (t // tm + i, 0)),
                  pl.BlockSpec((tm, TOP_K), lambda i: (i, 0)), row, vec, vec],
        out_specs=[row, row],
        out_shape=[jax.ShapeDtypeStruct((t, d), F32), jax.ShapeDtypeStruct((t, d), BF16)],
        compiler_params=_params(("parallel",), 40),
        name="combine_ln",
    )(ys, ys, gate_t, h, g.reshape(1, d), b.reshape(1, d))


def _invert_rows_kernel(dest_ref, out_ref):
    def clear(r, carry):
        out_ref[r] = -1
        return carry

    def place(s, carry):
        out_ref[dest_ref[s]] = s
        return carry

    lax.fori_loop(0, out_ref.shape[0], clear, 0, unroll=16)
    lax.fori_loop(0, dest_ref.shape[0], place, 0, unroll=16)


def _invert_rows(dest, rows):
    smem = pl.BlockSpec(memory_space=pltpu.SMEM)
    return pl.pallas_call(
        _invert_rows_kernel,
        in_specs=[smem],
        out_specs=smem,
        out_shape=jax.ShapeDtypeStruct((rows,), jnp.int32),
        name="invert_rows",
    )(dest)


def _moe_layer(h, hp, e_idx, gate, w_gate, w_up, w_down, layer, ln_g, ln_b, alpha):
    t, d = h.shape
    n_slots = t * TOP_K
    e_flat = e_idx.reshape(n_slots)
    experts = jnp.arange(N_EXPERTS, dtype=jnp.int32)
    onehot = (e_flat[:, None] == experts[None, :]).astype(jnp.int32)
    csum = jnp.cumsum(onehot, axis=0)
    counts = csum[-1]
    padded = ((counts + MOE_BLOCK - 1) // MOE_BLOCK) * MOE_BLOCK
    pad_end = jnp.cumsum(padded)
    pad_start = pad_end - padded
    dest = jnp.sum(onehot * (csum - 1 + pad_start[None, :]), axis=1).astype(jnp.int32)
    n_blocks = n_slots // MOE_BLOCK + N_EXPERTS
    rows = n_blocks * MOE_BLOCK
    row_slot = _invert_rows(dest, rows)
    row_id = jnp.arange(-MOE_BLOCK, rows, dtype=jnp.int32)
    row_slot = jnp.concatenate([jnp.full((MOE_BLOCK,), -1, jnp.int32), row_slot])
    valid = row_slot >= 0
    row_src = jnp.where(valid, row_slot % t, row_id % t)
    spare = n_slots + ((row_id // MOE_BLOCK + 1) % N_ROW_BUFS) * MOE_BLOCK + row_id % MOE_BLOCK
    row_dst = jnp.where(valid, row_slot, spare)
    blk_row0 = jnp.arange(n_blocks, dtype=jnp.int32) * MOE_BLOCK
    block_e = jnp.sum((pad_end[None, :] <= blk_row0[:, None]).astype(jnp.int32), axis=1)
    block_e = jnp.minimum(block_e, N_EXPERTS - 1)
    first = jnp.concatenate([jnp.ones((1,), bool), block_e[1:] != block_e[:-1]])
    nonempty = counts > 0
    place = jnp.cumsum(nonempty.astype(jnp.int32)) - 1
    blk_place = jnp.sum(jnp.where(block_e[:, None] == experts[None, :], place[None, :], 0), axis=1)
    wset = jnp.where(first, blk_place % N_WEIGHT_SETS, -1).astype(jnp.int32)

    def expert_at(p):
        hit = nonempty[None, :] & (place[None, :] == p.reshape(-1, 1))
        return (jnp.sum(jnp.where(hit, experts[None, :] + 1, 0), axis=1) - 1).astype(jnp.int32).reshape(p.shape)

    ahead = expert_at(blk_place + N_WEIGHT_SETS)
    second = expert_at(jnp.ones((1,), jnp.int32))
    n_used = (pad_end[-1] // MOE_BLOCK).astype(jnp.int32).reshape(1)
    meta = (block_e, wset, ahead, second, n_used, row_src.astype(jnp.int32), row_dst.astype(jnp.int32))
    ys = _experts(meta, hp, w_gate, w_up, w_down, layer, n_slots)
    return _combine_ln(ys, gate.T, h, ln_g, ln_b, alpha)


def _expand_heads(x, e_bf16):
    hi = x.astype(BF16)
    r1 = x - hi.astype(F32)
    mid = r1.astype(BF16)
    lo = (r1 - mid.astype(F32)).astype(BF16)
    dot = lambda a: jnp.dot(a, e_bf16, preferred_element_type=F32)
    return dot(hi) + dot(mid) + dot(lo)


def _ssd_kernel(z_ref, xbc_ref, prev_ref, dt_ref, cw_ref, cb_ref, dtb_ref, alog_ref, dexp_ref, nw_ref, e_ref,
                o_ref, state_sc, *, d_inner):
    c = pl.program_id(1)
    L = SSM_CHUNK
    gw = d_inner // SSM_GROUPS
    hpg = gw // SSM_HEAD_DIM

    @pl.when(c == 0)
    def _():
        state_sc[...] = jnp.zeros_like(state_sc)

    has_prev = c > 0
    tile = SUBLANES
    row_in_tile = lax.broadcasted_iota(jnp.int32, (tile, 1), 0)

    def conv_silu(col0, width):
        cols = slice(col0, col0 + width)
        x = xbc_ref[0, :, cols].astype(F32)
        tail = jnp.where(has_prev, prev_ref[0, prev_ref.shape[1] - tile:, cols].astype(F32), 0.0)
        acc = cb_ref[:, cols] + x * cw_ref[SSM_CONV - 1:SSM_CONV, cols]
        for j in range(1, SSM_CONV):
            xr = pltpu.roll(x, j, 0)
            head = jnp.where(row_in_tile < j, pltpu.roll(tail, j, 0), xr[0:tile, :])
            shifted = jnp.concatenate([head, xr[tile:, :]], axis=0)
            acc = acc + shifted * cw_ref[SSM_CONV - 1 - j:SSM_CONV - j, cols]
        return _silu(acc)

    dtv = dt_ref[0] + dtb_ref[...]
    dt = jnp.maximum(dtv, 0.0) + jnp.log1p(jnp.exp(-jnp.abs(dtv)))
    a = dt * (-jnp.exp(alog_ref[...]) * LOG2E)
    ri = lax.broadcasted_iota(jnp.int32, (L, L), 0)
    ci = lax.broadcasted_iota(jnp.int32, (L, L), 1)
    causal = ri >= ci
    a_cum = jnp.dot(causal.astype(F32), a, precision=HIGHEST, preferred_element_type=F32)
    a_cum_t = a_cum.T
    e_mat = e_ref[...]
    acum_x = _expand_heads(a_cum, e_mat)
    dt_x = _expand_heads(dt, e_mat)
    lane = lax.broadcasted_iota(jnp.int32, (L, LANES), 1)
    first_head = lane < SSM_HEAD_DIM

    for g in range(SSM_GROUPS):
        cols = slice(g * gw, (g + 1) * gw)
        xg = conv_silu(g * gw, gw)
        bg = conv_silu(d_inner + g * SSM_STATE, SSM_STATE)
        cg = conv_silu(d_inner + (SSM_GROUPS + g) * SSM_STATE, SSM_STATE)
        ac = acum_x[:, cols]
        last = ac[L - 1:L, :]
        xdt = xg * dt_x[:, cols]
        cb16 = cg.astype(BF16)
        bb16 = bg.astype(BF16)
        cbm = lax.dot_general(cb16, bb16, (((1,), (1,)), ((), ())), preferred_element_type=F32)
        st = state_sc[g]
        y_off = jnp.dot(cb16, st.astype(BF16), preferred_element_type=F32) * jnp.exp2(ac)
        xd = (xdt * jnp.exp2(last - ac)).astype(BF16)
        state_sc[g] = st * jnp.exp2(last) + jnp.dot(bg.T.astype(BF16), xd, preferred_element_type=F32)
        xdt16 = xdt.astype(BF16)
        pieces = []
        for jp in range(hpg // 2):
            lhs = []
            for hh in (g * hpg + 2 * jp, g * hpg + 2 * jp + 1):
                seg = jnp.where(causal, a_cum[:, hh:hh + 1] - a_cum_t[hh:hh + 1, :], NEG)
                lhs.append((cbm * jnp.exp2(seg)).astype(BF16))
            xp = xdt16[:, jp * LANES:(jp + 1) * LANES]
            zero = jnp.zeros_like(xp)
            rhs = jnp.concatenate([jnp.where(first_head, xp, zero), jnp.where(first_head, zero, xp)], axis=0)
            pieces.append(jnp.dot(jnp.concatenate(lhs, axis=1), rhs, preferred_element_type=F32))
        y = jnp.concatenate(pieces, axis=1) + y_off + dexp_ref[:, cols] * xg
        y = y * _silu(z_ref[0, :, cols].astype(F32))
        y = y * lax.rsqrt(jnp.mean(y * y, axis=1, keepdims=True) + RMS_EPS) * nw_ref[:, cols]
        o_ref[0, :, cols] = y.astype(o_ref.dtype)


def _ssd(z3, xbc3, dt3, conv_w, conv_b, dt_bias_p, a_log_p, d_exp, norm_w, e_mat):
    b, s, d_inner = z3.shape
    cdim = xbc3.shape[2]
    nc = s // SSM_CHUNK
    gw = d_inner // SSM_GROUPS
    prev_rows = SUBLANES * (4 // xbc3.dtype.itemsize)
    rows_per_prev = SSM_CHUNK // prev_rows
    full = lambda shape: pl.BlockSpec(shape, lambda bi, c: tuple(0 for _ in shape))
    return pl.pallas_call(
        functools.partial(_ssd_kernel, d_inner=d_inner),
        grid=(b, nc),
        in_specs=[pl.BlockSpec((1, SSM_CHUNK, d_inner), lambda bi, c: (bi, c, 0)),
                  pl.BlockSpec((1, SSM_CHUNK, cdim), lambda bi, c: (bi, c, 0)),
                  pl.BlockSpec((1, prev_rows, cdim), lambda bi, c: (bi, jnp.maximum(c * rows_per_prev - 1, 0), 0)),
                  pl.BlockSpec((1, SSM_CHUNK, LANES), lambda bi, c: (bi, c, 0)),
                  full((SSM_CONV, cdim)), full((1, cdim)), full((1, LANES)), full((1, LANES)),
                  full((1, d_inner)), full((1, d_inner)), full((LANES, d_inner))],
        out_specs=pl.BlockSpec((1, SSM_CHUNK, d_inner), lambda bi, c: (bi, c, 0)),
        out_shape=jax.ShapeDtypeStruct((b, s, d_inner), BF16),
        scratch_shapes=[pltpu.VMEM((SSM_GROUPS, SSM_STATE, gw), F32)],
        compiler_params=_params(("parallel", "arbitrary"), 56),
        name="ssd",
    )(z3, xbc3, xbc3, dt3, conv_w, conv_b, dt_bias_p, a_log_p, d_exp, norm_w, e_mat)


def kernel(x, positions, attn_w_qkv, attn_w_o, ssm_w_in, ssm_conv_w, ssm_conv_b, ssm_dt_bias, ssm_A_log, ssm_D, ssm_norm_w, ssm_w_out, router_w, router_bias, moe_w_gate, moe_w_up, moe_w_down, ln_mix_g, ln_mix_b, ln_ffn_g, ln_ffn_b):
    bsz, seq, d_model = x.shape
    t = bsz * seq
    depth = moe_w_gate.shape[0]
    alpha = (2 * depth) ** 0.25
    n_heads = d_model // HEAD_DIM
    patterns = ((128, 1), (512, 4), (2048, 16))
    attn_blk = 512

    half = ROT_DIM // 2
    inv_freq = ROPE_THETA ** (-jnp.arange(0, ROT_DIM, 2, dtype=F32) / ROT_DIM)
    ang = positions.astype(F32).reshape(t, 1) * inv_freq
    cos, sin = jnp.cos(ang), jnp.sin(ang)
    zeros_r = jnp.zeros((t, LANES - ROT_DIM), F32)
    cos_t = jnp.concatenate([cos, cos, jnp.ones((t, LANES - ROT_DIM), F32)], axis=1)
    sa_t = jnp.concatenate([-sin, jnp.zeros((t, half), F32), zeros_r], axis=1)
    sb_t = jnp.concatenate([jnp.zeros((t, half), F32), sin, zeros_r], axis=1)
    bias = _branch_count_bias(seq, attn_blk, patterns)

    bias_col = router_bias.reshape(N_EXPERTS, 1).astype(F32)

    h = x.reshape(t, d_model)
    hb = h.astype(BF16)
    for i in range(depth):
        j = i // 2
        if i % 2 == 0:
            qkv = _qkv_rope(h, attn_w_qkv[j].astype(BF16), cos_t, sa_t, sb_t, d_model)
            att = _attention(qkv.reshape(bsz, seq, 3 * d_model), bias, n_heads, attn_blk)
            mix, w_mix = att.reshape(t, d_model), attn_w_o[j]
        else:
            d_inner = ssm_norm_w.shape[1]
            n_ssm_heads = ssm_dt_bias.shape[1]
            cdim = ssm_conv_w.shape[2]
            w_in = ssm_w_in[j].astype(BF16)
            z = _mm(hb, w_in, 0, d_inner, BF16, tn=1024)
            xbc = _mm(hb, w_in, d_inner, cdim, BF16, tn=1024)
            w_dt = jnp.pad(w_in[:, d_inner + cdim:], ((0, 0), (0, LANES - n_ssm_heads)))
            dt = _mm(hb, w_dt, 0, LANES, F32)
            hpad = (0, LANES - n_ssm_heads)
            head_of_ch = jnp.arange(d_inner, dtype=jnp.int32) // SSM_HEAD_DIM
            e_mat = (jnp.arange(LANES, dtype=jnp.int32)[:, None] == head_of_ch[None, :]).astype(BF16)
            y = _ssd(z.reshape(bsz, seq, d_inner), xbc.reshape(bsz, seq, cdim), dt.reshape(bsz, seq, LANES),
                     ssm_conv_w[j], ssm_conv_b[j].reshape(1, cdim),
                     jnp.pad(ssm_dt_bias[j], hpad).reshape(1, LANES), jnp.pad(ssm_A_log[j], hpad).reshape(1, LANES),
                     jnp.repeat(ssm_D[j], SSM_HEAD_DIM).reshape(1, d_inner), ssm_norm_w[j].reshape(1, d_inner), e_mat)
            mix, w_mix = y.reshape(t, d_inner), ssm_w_out[j]
        h, hp, e_idx, gate = _mm_res_ln_route(mix, w_mix.astype(BF16), h, ln_mix_g[i], ln_mix_b[i],
                                              router_w, bias_col, alpha)
        h, hb = _moe_layer(h, hp, e_idx, gate, moe_w_gate, moe_w_up, moe_w_down, i,
                           ln_ffn_g[i], ln_ffn_b[i], alpha)
    return h.reshape(bsz, seq, d_model)
```

```python
import functools
import math

import jax
import jax.numpy as jnp
from jax import lax
from jax.experimental import pallas as pl
from jax.experimental.pallas import tpu as pltpu

F32 = jnp.float32
BF16 = jnp.bfloat16
HIGHEST = lax.Precision.HIGHEST

LANES = 128
SUBLANES = 8
MXU_COLS = 256
MIB = 1024 * 1024

HEAD_DIM = 128
ROT_DIM = HEAD_DIM // 4
ROPE_THETA = 500000.0
N_EXPERTS = 64
N_GROUPS = 8
EPG = N_EXPERTS // N_GROUPS
TOP_K = 2
MOE_BLOCK = 256
SSM_HEAD_DIM = 64
SSM_GROUPS = 8
SSM_STATE = 128
SSM_CONV = 4
SSM_CHUNK = 128
LN_EPS = 1e-5
RMS_EPS = 1e-5
NEG = -1e30
LOG2E = math.log2(math.e)


def _params(sem, vmem_mib):
    return pltpu.CompilerParams(dimension_semantics=sem, vmem_limit_bytes=vmem_mib * MIB)


def _silu(x):
    hx = 0.5 * x
    return hx + hx * jnp.tanh(hx)


def _layer_norm_rows(y, g, b):
    mu = jnp.mean(y, axis=-1, keepdims=True)
    yc = y - mu
    var = jnp.mean(yc * yc, axis=-1, keepdims=True)
    return yc * lax.rsqrt(var + LN_EPS) * g + b


def _mm_kernel(a_ref, w_ref, o_ref):
    o_ref[...] = jnp.dot(a_ref[...], w_ref[...], preferred_element_type=F32).astype(o_ref.dtype)


def _mm(a, w, col0, n, out_dtype, tm=1024, tn=512):
    m, k = a.shape
    tn = min(tn, n)
    assert col0 % tn == 0 and n % tn == 0 and m % tm == 0
    c0 = col0 // tn
    return pl.pallas_call(
        _mm_kernel,
        grid=(m // tm, n // tn),
        in_specs=[pl.BlockSpec((tm, k), lambda i, j: (i, 0)),
                  pl.BlockSpec((k, tn), lambda i, j: (0, c0 + j))],
        out_specs=pl.BlockSpec((tm, tn), lambda i, j: (i, j)),
        out_shape=jax.ShapeDtypeStruct((m, n), out_dtype),
        compiler_params=_params(("parallel", "arbitrary"), 48),
        name="mm",
    )(a, w)


def _qkv_rope_kernel(x32_ref, w_ref, cos_ref, sa_ref, sb_ref, o_ref, x_ref, *, n_rope_tiles, n_q_tiles, scale):
    n = pl.program_id(1)
    tn = w_ref.shape[1]

    @pl.when(n == 0)
    def _():
        x_ref[...] = x32_ref[...].astype(BF16)

    @pl.when(n < n_rope_tiles)
    def _():
        c = cos_ref[...]
        sa = sa_ref[...]
        sb = sb_ref[...]
        sc = jnp.where(n < n_q_tiles, scale, 1.0).astype(F32)
        for p in range(tn // MXU_COLS):
            t2 = jnp.dot(x_ref[...], w_ref[:, p * MXU_COLS:(p + 1) * MXU_COLS], preferred_element_type=F32)
            for j in range(MXU_COLS // LANES):
                t = t2[:, j * LANES:(j + 1) * LANES]
                r = t * c + pltpu.roll(t, LANES - ROT_DIM // 2, 1) * sa + pltpu.roll(t, ROT_DIM // 2, 1) * sb
                col = p * MXU_COLS + j * LANES
                o_ref[:, col:col + LANES] = (r * sc).astype(o_ref.dtype)

    @pl.when(n >= n_rope_tiles)
    def _():
        o_ref[...] = jnp.dot(x_ref[...], w_ref[...], preferred_element_type=F32).astype(o_ref.dtype)


def _qkv_rope(x, w, cos_t, sa_t, sb_t, d_model, tm=1024, tn=1024):
    m, k = x.shape
    n = w.shape[1]
    kern = functools.partial(_qkv_rope_kernel, n_rope_tiles=2 * d_model // tn, n_q_tiles=d_model // tn,
                             scale=HEAD_DIM ** -0.5 * LOG2E)
    tab = pl.BlockSpec((tm, LANES), lambda i, j: (i, 0))
    return pl.pallas_call(
        kern,
        grid=(m // tm, n // tn),
        in_specs=[pl.BlockSpec((tm, k), lambda i, j: (i, 0)),
                  pl.BlockSpec((k, tn), lambda i, j: (0, j)),
                  tab, tab, tab],
        out_specs=pl.BlockSpec((tm, tn), lambda i, j: (i, j)),
        out_shape=jax.ShapeDtypeStruct((m, n), BF16),
        scratch_shapes=[pltpu.VMEM((tm, k), BF16)],
        compiler_params=_params(("parallel", "arbitrary"), 56),
        name="qkv_rope",
    )(x, w, cos_t, sa_t, sb_t)


def _attn_kernel(qi_ref, ki_ref, q_ref, k_ref, v_ref, bias_ref, o_ref, m_sc, l_sc, acc_sc, *, heads):
    p = pl.program_id(2)
    qi = qi_ref[p]
    ki = ki_ref[p]
    blk = q_ref.shape[1]

    @pl.when(ki == 0)
    def _():
        m_sc[...] = jnp.full_like(m_sc, 0.1 * NEG)
        l_sc[...] = jnp.zeros_like(l_sc)
        acc_sc[...] = jnp.zeros_like(acc_sc)

    bias = bias_ref[qi - ki]
    for j in range(heads):
        cols = slice(j * HEAD_DIM, (j + 1) * HEAD_DIM)
        s = lax.dot_general(q_ref[0, :, cols], k_ref[0, :, cols], (((1,), (1,)), ((), ())),
                            preferred_element_type=F32) + bias
        m_prev = m_sc[j]
        m_new = jnp.maximum(m_prev, jnp.max(s, axis=1, keepdims=True))
        alpha = jnp.exp2(m_prev - m_new)
        pr = jnp.exp2(s - jnp.concatenate([m_new] * (blk // LANES), axis=1))
        l_sc[j] = alpha * l_sc[j] + jnp.sum(pr, axis=1, keepdims=True)
        acc_sc[j] = alpha * acc_sc[j] + jnp.dot(pr.astype(BF16), v_ref[0, :, cols], preferred_element_type=F32)
        m_sc[j] = m_new

    @pl.when(ki == qi)
    def _():
        for j in range(heads):
            o_ref[0, :, j * HEAD_DIM:(j + 1) * HEAD_DIM] = (acc_sc[j] / l_sc[j]).astype(o_ref.dtype)


def _attention(qkv3, bias, n_heads, blk, heads=8):
    b, s, _ = qkv3.shape
    nq = s // blk
    pairs = [(qi, ki) for qi in range(nq) for ki in range(qi + 1)]
    qi_tab = jnp.asarray([pq for pq, _ in pairs], jnp.int32)
    ki_tab = jnp.asarray([pk for _, pk in pairs], jnp.int32)
    hg = n_heads // heads
    w = heads * HEAD_DIM
    return pl.pallas_call(
        functools.partial(_attn_kernel, heads=heads),
        grid_spec=pltpu.PrefetchScalarGridSpec(
            num_scalar_prefetch=2,
            grid=(b, hg, len(pairs)),
            in_specs=[pl.BlockSpec((1, blk, w), lambda bi, h, p, qt, kt: (bi, qt[p], h)),
                      pl.BlockSpec((1, blk, w), lambda bi, h, p, qt, kt: (bi, kt[p], hg + h)),
                      pl.BlockSpec((1, blk, w), lambda bi, h, p, qt, kt: (bi, kt[p], 2 * hg + h)),
                      pl.BlockSpec((nq, blk, blk), lambda bi, h, p, qt, kt: (0, 0, 0))],
            out_specs=pl.BlockSpec((1, blk, w), lambda bi, h, p, qt, kt: (bi, qt[p], h)),
            scratch_shapes=[pltpu.VMEM((heads, blk, LANES), F32), pltpu.VMEM((heads, blk, LANES), F32),
                            pltpu.VMEM((heads, blk, HEAD_DIM), F32)],
        ),
        out_shape=jax.ShapeDtypeStruct((b, s, n_heads * HEAD_DIM), BF16),
        compiler_params=_params(("parallel", "parallel", "arbitrary"), 48),
        name="attention",
    )(qi_tab, ki_tab, qkv3, qkv3, qkv3, bias)


def _branch_count_bias(s, blk, patterns):
    nq = s // blk
    off = jnp.arange(nq, dtype=jnp.int32)[:, None, None] * blk
    d = off + jnp.arange(blk, dtype=jnp.int32)[None, :, None] - jnp.arange(blk, dtype=jnp.int32)[None, None, :]
    cnt = jnp.zeros(d.shape, F32)
    for window, dilation in patterns:
        cnt = cnt + ((d >= 0) & (d % dilation == 0) & (d <= window)).astype(F32)
    return jnp.where(cnt > 0, jnp.log2(jnp.maximum(cnt, 1.0)), NEG)


def _pack_bf16_pairs(x):
    n = x.shape[1] // 2
    lo = lax.bitcast_convert_type(x[:, :n].astype(BF16).astype(F32), jnp.uint32)
    hi = lax.bitcast_convert_type(x[:, n:].astype(BF16).astype(F32), jnp.uint32)
    return (lo >> 16) | hi


def _unpack_bf16_pairs(w):
    lo = lax.bitcast_convert_type(w << 16, F32)
    hi = lax.bitcast_convert_type(w & jnp.uint32(0xFFFF0000), F32)
    return jnp.concatenate([lo, hi], axis=1)


def _mm_res_ln_route_kernel(a_ref, w_ref, h_ref, g_ref, b_ref, rwh_ref, rwl_ref, rb_ref,
                            of_ref, hp_ref, e_ref, gt_ref, *, alpha, n_sub):
    sub = a_ref.shape[0] // n_sub
    for r in range(n_sub):
        rows = slice(r * sub, (r + 1) * sub)
        mix = jnp.dot(a_ref[rows, :], w_ref[...], preferred_element_type=F32)
        out = _layer_norm_rows(alpha * h_ref[rows, :] + mix, g_ref[...], b_ref[...])
        of_ref[rows, :] = out
        hp_ref[rows, :] = _pack_bf16_pairs(out)
        experts, gates = _route_top2(out, rwh_ref[...], rwl_ref[...], rb_ref[...])
        for k in range(TOP_K):
            e_ref[k:k + 1, rows] = experts[k]
            gt_ref[k:k + 1, rows] = gates[k]


def _mm_res_ln_route(a, w, h, g, b, router_w, bias_col, alpha, tm=512, n_sub=2):
    m, kdim = a.shape
    n = w.shape[1]
    kern = functools.partial(_mm_res_ln_route_kernel, alpha=alpha, n_sub=n_sub)
    row = pl.BlockSpec((tm, n), lambda i: (i, 0))
    vec = pl.BlockSpec((1, n), lambda i: (0, 0))
    sel = pl.BlockSpec((TOP_K, tm), lambda i: (0, i))
    rw = jnp.pad(router_w, ((0, 0), (0, LANES - N_EXPERTS)))
    rw_hi = rw.astype(BF16)
    rw_lo = (rw - rw_hi.astype(F32)).astype(BF16)
    rw_spec = pl.BlockSpec((n, LANES), lambda i: (0, 0))
    return pl.pallas_call(
        kern,
        grid=(m // tm,),
        in_specs=[pl.BlockSpec((tm, kdim), lambda i: (i, 0)),
                  pl.BlockSpec((kdim, n), lambda i: (0, 0), pipeline_mode=pl.Buffered(1)),
                  row, vec, vec,
                  rw_spec, rw_spec, pl.BlockSpec((N_EXPERTS, 1), lambda i: (0, 0))],
        out_specs=[row, pl.BlockSpec((tm, n // 2), lambda i: (i, 0)), sel, sel],
        out_shape=[jax.ShapeDtypeStruct((m, n), F32), jax.ShapeDtypeStruct((m, n // 2), jnp.uint32),
                   jax.ShapeDtypeStruct((TOP_K, m), jnp.int32), jax.ShapeDtypeStruct((TOP_K, m), F32)],
        compiler_params=_params(("parallel",), 56),
        name="mm_res_ln_route",
    )(a, w, h, g.reshape(1, n), b.reshape(1, n), rw_hi, rw_lo, bias_col)


def _route_top2(h, w_hi, w_lo, bias_col):
    h_hi = h.astype(BF16)
    h_lo = (h - h_hi.astype(F32)).astype(BF16)
    dot = lambda a, b: jnp.dot(a, b, preferred_element_type=F32)
    logits = (dot(h_hi, w_hi) + dot(h_lo, w_hi) + dot(h_hi, w_lo)).T[:N_EXPERTS, :]
    mx = jnp.max(logits, axis=0, keepdims=True)
    ex = jnp.exp(logits - mx)
    scores = ex / jnp.sum(ex, axis=0, keepdims=True)
    sel = scores + bias_col
    tm = sel.shape[1]
    li = lax.broadcasted_iota(jnp.int32, (EPG, tm), 0)

    def first_argmax(v):
        m = jnp.max(v, axis=0, keepdims=True)
        return m, jnp.min(jnp.where(v == m, li, EPG), axis=0, keepdims=True)

    gs, i1s, i2s = [], [], []
    for g in range(N_GROUPS):
        slab = sel[g * EPG:(g + 1) * EPG, :]
        m1, i1 = first_argmax(slab)
        m2, i2 = first_argmax(jnp.where(li == i1, NEG, slab))
        gs.append(m1 + m2)
        i1s.append(i1)
        i2s.append(i2)
    best, gidx = gs[0], jnp.zeros((1, tm), jnp.int32)
    for g in range(1, N_GROUPS):
        better = gs[g] > best
        best = jnp.where(better, gs[g], best)
        gidx = jnp.where(better, g, gidx)
    l1 = jnp.zeros((1, tm), jnp.int32)
    l2 = jnp.zeros((1, tm), jnp.int32)
    s1 = jnp.zeros((1, tm), F32)
    s2 = jnp.zeros((1, tm), F32)
    for g in range(N_GROUPS):
        here = gidx == g
        slab = scores[g * EPG:(g + 1) * EPG, :]
        l1 = jnp.where(here, i1s[g], l1)
        l2 = jnp.where(here, i2s[g], l2)
        s1 = jnp.where(here, jnp.sum(jnp.where(li == i1s[g], slab, 0.0), axis=0, keepdims=True), s1)
        s2 = jnp.where(here, jnp.sum(jnp.where(li == i2s[g], slab, 0.0), axis=0, keepdims=True), s2)
    tot = s1 + s2
    return (gidx * EPG + l1, gidx * EPG + l2), (s1 / tot, s2 / tot)


N_ROW_BUFS = 2
N_SPARE_ROWS = N_ROW_BUFS * MOE_BLOCK
N_WEIGHT_SETS = 2
WEIGHT_DMA_PRIORITY = 1


def _experts_kernel(be_ref, wset_ref, ahead_ref, second_ref, nused_ref, src_ref, dst_ref,
                    h_hbm, wg_hbm, wu_hbm, wd_hbm, ys_hbm,
                    xb0, xb1, ob0, ob1, wgl, wul, wdl, wgb, wub, wdb, gsem, ssem, wsem, *, layer, n_slots):
    i = pl.program_id(0)
    n_used = nused_ref[0]
    xbufs = (xb0, xb1)
    obufs = (ob0, ob1)

    def gather_start(blk, s):
        base = (blk + 1) * MOE_BLOCK
        for r in range(MOE_BLOCK):
            pltpu.make_async_copy(h_hbm.at[pl.ds(src_ref[base + r], 1), :],
                                  xbufs[s].at[pl.ds(r, 1), :], gsem.at[s]).start()

    def scatter_start(blk, s):
        base = (blk + 1) * MOE_BLOCK
        for r in range(MOE_BLOCK):
            pltpu.make_async_copy(obufs[s].at[pl.ds(r, 1), :],
                                  ys_hbm.at[pl.ds(dst_ref[base + r], 1), :], ssem.at[s]).start()

    def gather_wait(s):
        pltpu.make_async_copy(h_hbm.at[pl.ds(0, MOE_BLOCK), :], xbufs[s], gsem.at[s]).wait()

    def scatter_wait(s):
        pltpu.make_async_copy(obufs[s], ys_hbm.at[pl.ds(0, MOE_BLOCK), :], ssem.at[s]).wait()

    w_hbm = (wg_hbm, wu_hbm, wd_hbm)
    w_land = (wgl, wul, wdl)
    w_work = (wgb, wub, wdb)

    def weight_copy(e, p, k):
        return pltpu.make_async_copy(w_hbm[k].at[layer, e], w_land[k].at[p], wsem.at[p, k])

    @pl.when(i == 0)
    def _():
        ob1[...] = jnp.zeros_like(ob1)
        spare = pltpu.make_async_copy(ob1, ys_hbm.at[pl.ds(n_slots + MOE_BLOCK, MOE_BLOCK), :], ssem.at[1])
        spare.start()
        spare.wait()
        gather_start(0, 0)
        for k in range(len(w_hbm)):
            weight_copy(be_ref[0], 0, k).start(priority=WEIGHT_DMA_PRIORITY)

        @pl.when(second_ref[0] >= 0)
        def _():
            for k in range(len(w_hbm)):
                weight_copy(second_ref[0], 1, k).start(priority=WEIGHT_DMA_PRIORITY)

    @pl.when(i < n_used)
    def _():
        for p in range(N_WEIGHT_SETS):
            @pl.when(wset_ref[i] == p)
            def _():
                ahead = ahead_ref[i]
                for k in range(len(w_hbm)):
                    weight_copy(be_ref[i], p, k).wait()
                    w_work[k][...] = w_land[k][p].astype(BF16)

                    @pl.when(ahead >= 0)
                    def _():
                        weight_copy(ahead, p, k).start(priority=WEIGHT_DMA_PRIORITY)

        for s in range(N_ROW_BUFS):
            @pl.when(i % N_ROW_BUFS == s)
            def _():
                gather_wait(s)

                @pl.when(i >= 1)
                def _():
                    scatter_wait(s)

                x = _unpack_bf16_pairs(xbufs[s][...]).astype(BF16)
                hg = jnp.dot(x, wgb[...], preferred_element_type=F32)
                hu = jnp.dot(x, wub[...], preferred_element_type=F32)
                act = (_silu(hg) * hu).astype(BF16)
                obufs[s][...] = _pack_bf16_pairs(jnp.dot(act, wdb[...], preferred_element_type=F32))
                gather_start(i + 1, 1 - s)
                scatter_start(i - 1, 1 - s)

                @pl.when(i == n_used - 1)
                def _():
                    gather_wait(1 - s)
                    scatter_wait(1 - s)
                    scatter_start(i, s)
                    scatter_wait(s)


def _experts(meta, hp, w_gate, w_up, w_down, layer, n_slots):
    d, de = w_gate.shape[2], w_gate.shape[3]
    nb = meta[0].shape[0]
    any_spec = pl.BlockSpec(memory_space=pl.ANY)
    rowbuf = pltpu.VMEM((MOE_BLOCK, hp.shape[1]), jnp.uint32)
    return pl.pallas_call(
        functools.partial(_experts_kernel, layer=layer, n_slots=n_slots),
        grid_spec=pltpu.PrefetchScalarGridSpec(
            num_scalar_prefetch=len(meta),
            grid=(nb,),
            in_specs=[any_spec, any_spec, any_spec, any_spec],
            out_specs=any_spec,
            scratch_shapes=[rowbuf, rowbuf, rowbuf, rowbuf,
                            pltpu.VMEM((N_WEIGHT_SETS, d, de), F32), pltpu.VMEM((N_WEIGHT_SETS, d, de), F32),
                            pltpu.VMEM((N_WEIGHT_SETS, de, d), F32),
                            pltpu.VMEM((d, de), BF16), pltpu.VMEM((d, de), BF16), pltpu.VMEM((de, d), BF16),
                            pltpu.SemaphoreType.DMA((N_ROW_BUFS,)), pltpu.SemaphoreType.DMA((N_ROW_BUFS,)),
                            pltpu.SemaphoreType.DMA((N_WEIGHT_SETS, 3))],
        ),
        out_shape=jax.ShapeDtypeStruct((n_slots + N_SPARE_ROWS, hp.shape[1]), jnp.uint32),
        compiler_params=_params(("arbitrary",), 56),
        name="experts",
    )(*meta, hp, w_gate, w_up, w_down)


def _combine_ln_kernel(y0_ref, y1_ref, gate_ref, h_ref, g_ref, b_ref, of_ref, *maybe_ob_ref, alpha):
    gt = gate_ref[...]
    moe = gt[:, 0:1] * _unpack_bf16_pairs(y0_ref[...]) + gt[:, 1:2] * _unpack_bf16_pairs(y1_ref[...])
    out = _layer_norm_rows(alpha * h_ref[...] + moe, g_ref[...], b_ref[...])
    of_ref[...] = out
    for ob_ref in maybe_ob_ref:
        ob_ref[...] = out.astype(BF16)


def _combine_ln(ys, gate_t, h, g, b, alpha, emit_bf16, tm=256):
    t, d = h.shape
    row = pl.BlockSpec((tm, d), lambda i: (i, 0))
    vec = pl.BlockSpec((1, d), lambda i: (0, 0))
    dp = ys.shape[1]
    out_shape = [jax.ShapeDtypeStruct((t, d), F32)] + ([jax.ShapeDtypeStruct((t, d), BF16)] if emit_bf16 else [])
    res = pl.pallas_call(
        functools.partial(_combine_ln_kernel, alpha=alpha),
        grid=(t // tm,),
        in_specs=[pl.BlockSpec((tm, dp), lambda i: (i, 0)), pl.BlockSpec((tm, dp), lambda i: (t // tm + i, 0)),
                  pl.BlockSpec((tm, TOP_K), lambda i: (i, 0)), row, vec, vec],
        out_specs=[row] * len(out_shape),
        out_shape=out_shape,
        compiler_params=_params(("parallel",), 40),
        name="combine_ln",
    )(ys, ys, gate_t, h, g.reshape(1, d), b.reshape(1, d))
    return (res[0], res[1]) if emit_bf16 else (res[0], None)


def _invert_rows_kernel(dest_ref, out_ref):
    def clear(r, carry):
        out_ref[r] = -1
        return carry

    def place(s, carry):
        out_ref[dest_ref[s]] = s
        return carry

    lax.fori_loop(0, out_ref.shape[0], clear, 0, unroll=32)
    lax.fori_loop(0, dest_ref.shape[0], place, 0, unroll=16)


def _invert_rows(dest, rows):
    smem = pl.BlockSpec(memory_space=pltpu.SMEM)
    return pl.pallas_call(
        _invert_rows_kernel,
        in_specs=[smem],
        out_specs=smem,
        out_shape=jax.ShapeDtypeStruct((rows,), jnp.int32),
        name="invert_rows",
    )(dest)


def _moe_layer(h, hp, e_idx, gate, w_gate, w_up, w_down, layer, ln_g, ln_b, alpha, emit_bf16):
    t, d = h.shape
    n_slots = t * TOP_K
    e_flat = e_idx.reshape(n_slots)
    experts = jnp.arange(N_EXPERTS, dtype=jnp.int32)
    onehot = (e_flat[:, None] == experts[None, :]).astype(jnp.int32)
    csum = jnp.cumsum(onehot, axis=0)
    counts = csum[-1]
    padded = ((counts + MOE_BLOCK - 1) // MOE_BLOCK) * MOE_BLOCK
    pad_end = jnp.cumsum(padded)
    pad_start = pad_end - padded
    dest = jnp.sum(onehot * (csum - 1 + pad_start[None, :]), axis=1).astype(jnp.int32)
    n_blocks = n_slots // MOE_BLOCK + N_EXPERTS
    rows = n_blocks * MOE_BLOCK
    row_slot = _invert_rows(dest, rows)
    row_id = jnp.arange(-MOE_BLOCK, rows, dtype=jnp.int32)
    row_slot = jnp.concatenate([jnp.full((MOE_BLOCK,), -1, jnp.int32), row_slot])
    valid = row_slot >= 0
    row_src = jnp.where(valid, row_slot % t, row_id % t)
    spare = n_slots + ((row_id // MOE_BLOCK + 1) % N_ROW_BUFS) * MOE_BLOCK + row_id % MOE_BLOCK
    row_dst = jnp.where(valid, row_slot, spare)
    blk_row0 = jnp.arange(n_blocks, dtype=jnp.int32) * MOE_BLOCK
    block_e = jnp.sum((pad_end[None, :] <= blk_row0[:, None]).astype(jnp.int32), axis=1)
    block_e = jnp.minimum(block_e, N_EXPERTS - 1)
    first = jnp.concatenate([jnp.ones((1,), bool), block_e[1:] != block_e[:-1]])
    nonempty = counts > 0
    place = jnp.cumsum(nonempty.astype(jnp.int32)) - 1
    blk_place = jnp.sum(jnp.where(block_e[:, None] == experts[None, :], place[None, :], 0), axis=1)
    wset = jnp.where(first, blk_place % N_WEIGHT_SETS, -1).astype(jnp.int32)

    def expert_at(p):
        hit = nonempty[None, :] & (place[None, :] == p.reshape(-1, 1))
        return (jnp.sum(jnp.where(hit, experts[None, :] + 1, 0), axis=1) - 1).astype(jnp.int32).reshape(p.shape)

    ahead = expert_at(blk_place + N_WEIGHT_SETS)
    second = expert_at(jnp.ones((1,), jnp.int32))
    n_used = (pad_end[-1] // MOE_BLOCK).astype(jnp.int32).reshape(1)
    meta = (block_e, wset, ahead, second, n_used, row_src.astype(jnp.int32), row_dst.astype(jnp.int32))
    ys = _experts(meta, hp, w_gate, w_up, w_down, layer, n_slots)
    return _combine_ln(ys, gate.T, h, ln_g, ln_b, alpha, emit_bf16)


def _expand_heads(x, e_bf16):
    hi = x.astype(BF16)
    r1 = x - hi.astype(F32)
    mid = r1.astype(BF16)
    lo = (r1 - mid.astype(F32)).astype(BF16)
    dot = lambda a: jnp.dot(a, e_bf16, preferred_element_type=F32)
    return dot(hi) + dot(mid) + dot(lo)


def _ssd_kernel(z_ref, xbc_ref, prev_ref, dt_ref, cw_ref, cb_ref, dtb_ref, alog_ref, dexp_ref, nw_ref, e_ref,
                o_ref, state_sc, *, d_inner):
    c = pl.program_id(1)
    L = SSM_CHUNK
    gw = d_inner // SSM_GROUPS
    hpg = gw // SSM_HEAD_DIM

    @pl.when(c == 0)
    def _():
        state_sc[...] = jnp.zeros_like(state_sc)

    has_prev = c > 0
    tile = SUBLANES
    row_in_tile = lax.broadcasted_iota(jnp.int32, (tile, 1), 0)

    def conv_silu(col0, width):
        cols = slice(col0, col0 + width)
        x = xbc_ref[0, :, cols].astype(F32)
        tail = jnp.where(has_prev, prev_ref[0, prev_ref.shape[1] - tile:, cols].astype(F32), 0.0)
        acc = cb_ref[:, cols] + x * cw_ref[SSM_CONV - 1:SSM_CONV, cols]
        for j in range(1, SSM_CONV):
            xr = pltpu.roll(x, j, 0)
            head = jnp.where(row_in_tile < j, pltpu.roll(tail, j, 0), xr[0:tile, :])
            shifted = jnp.concatenate([head, xr[tile:, :]], axis=0)
            acc = acc + shifted * cw_ref[SSM_CONV - 1 - j:SSM_CONV - j, cols]
        return _silu(acc)

    dtv = dt_ref[0] + dtb_ref[...]
    dt = jnp.maximum(dtv, 0.0) + jnp.log1p(jnp.exp(-jnp.abs(dtv)))
    a = dt * (-jnp.exp(alog_ref[...]) * LOG2E)
    ri = lax.broadcasted_iota(jnp.int32, (L, L), 0)
    ci = lax.broadcasted_iota(jnp.int32, (L, L), 1)
    causal = ri >= ci
    a_cum = jnp.dot(causal.astype(F32), a, precision=HIGHEST, preferred_element_type=F32)
    a_cum_t = a_cum.T
    e_mat = e_ref[...]
    acum_x = _expand_heads(a_cum, e_mat)
    dt_x = _expand_heads(dt, e_mat)
    lane = lax.broadcasted_iota(jnp.int32, (L, LANES), 1)
    first_head = lane < SSM_HEAD_DIM

    for g in range(SSM_GROUPS):
        cols = slice(g * gw, (g + 1) * gw)
        xg = conv_silu(g * gw, gw)
        bg = conv_silu(d_inner + g * SSM_STATE, SSM_STATE)
        cg = conv_silu(d_inner + (SSM_GROUPS + g) * SSM_STATE, SSM_STATE)
        ac = acum_x[:, cols]
        last = ac[L - 1:L, :]
        xdt = xg * dt_x[:, cols]
        cb16 = cg.astype(BF16)
        bb16 = bg.astype(BF16)
        cbm = lax.dot_general(cb16, bb16, (((1,), (1,)), ((), ())), preferred_element_type=F32)
        st = state_sc[g]
        y_off = jnp.dot(cb16, st.astype(BF16), preferred_element_type=F32) * jnp.exp2(ac)
        xd = (xdt * jnp.exp2(last - ac)).astype(BF16)
        state_sc[g] = st * jnp.exp2(last) + jnp.dot(bg.T.astype(BF16), xd, preferred_element_type=F32)
        xdt16 = xdt.astype(BF16)
        pieces = []
        for jp in range(hpg // 2):
            lhs = []
            for hh in (g * hpg + 2 * jp, g * hpg + 2 * jp + 1):
                seg = jnp.where(causal, a_cum[:, hh:hh + 1] - a_cum_t[hh:hh + 1, :], NEG)
                lhs.append((cbm * jnp.exp2(seg)).astype(BF16))
            xp = xdt16[:, jp * LANES:(jp + 1) * LANES]
            zero = jnp.zeros_like(xp)
            rhs = jnp.concatenate([jnp.where(first_head, xp, zero), jnp.where(first_head, zero, xp)], axis=0)
            pieces.append(jnp.dot(jnp.concatenate(lhs, axis=1), rhs, preferred_element_type=F32))
        y = jnp.concatenate(pieces, axis=1) + y_off + dexp_ref[:, cols] * xg
        y = y * _silu(z_ref[0, :, cols].astype(F32))
        y = y * lax.rsqrt(jnp.mean(y * y, axis=1, keepdims=True) + RMS_EPS) * nw_ref[:, cols]
        o_ref[0, :, cols] = y.astype(o_ref.dtype)


def _ssd(z3, xbc3, dt3, conv_w, conv_b, dt_bias_p, a_log_p, d_exp, norm_w, e_mat):
    b, s, d_inner = z3.shape
    cdim = xbc3.shape[2]
    nc = s // SSM_CHUNK
    gw = d_inner // SSM_GROUPS
    prev_rows = SUBLANES * (4 // xbc3.dtype.itemsize)
    rows_per_prev = SSM_CHUNK // prev_rows
    full = lambda shape: pl.BlockSpec(shape, lambda bi, c: tuple(0 for _ in shape))
    return pl.pallas_call(
        functools.partial(_ssd_kernel, d_inner=d_inner),
        grid=(b, nc),
        in_specs=[pl.BlockSpec((1, SSM_CHUNK, d_inner), lambda bi, c: (bi, c, 0)),
                  pl.BlockSpec((1, SSM_CHUNK, cdim), lambda bi, c: (bi, c, 0)),
                  pl.BlockSpec((1, prev_rows, cdim), lambda bi, c: (bi, jnp.maximum(c * rows_per_prev - 1, 0), 0)),
                  pl.BlockSpec((1, SSM_CHUNK, LANES), lambda bi, c: (bi, c, 0)),
                  full((SSM_CONV, cdim)), full((1, cdim)), full((1, LANES)), full((1, LANES)),
                  full((1, d_inner)), full((1, d_inner)), full((LANES, d_inner))],
        out_specs=pl.BlockSpec((1, SSM_CHUNK, d_inner), lambda bi, c: (bi, c, 0)),
        out_shape=jax.ShapeDtypeStruct((b, s, d_inner), BF16),
        scratch_shapes=[pltpu.VMEM((SSM_GROUPS, SSM_STATE, gw), F32)],
        compiler_params=_params(("parallel", "arbitrary"), 56),
        name="ssd",
    )(z3, xbc3, xbc3, dt3, conv_w, conv_b, dt_bias_p, a_log_p, d_exp, norm_w, e_mat)


def kernel(x, positions, attn_w_qkv, attn_w_o, ssm_w_in, ssm_conv_w, ssm_conv_b, ssm_dt_bias, ssm_A_log, ssm_D, ssm_norm_w, ssm_w_out, router_w, router_bias, moe_w_gate, moe_w_up, moe_w_down, ln_mix_g, ln_mix_b, ln_ffn_g, ln_ffn_b):
    bsz, seq, d_model = x.shape
    t = bsz * seq
    depth = moe_w_gate.shape[0]
    alpha = (2 * depth) ** 0.25
    n_heads = d_model // HEAD_DIM
    patterns = ((128, 1), (512, 4), (2048, 16))
    attn_blk = 512

    half = ROT_DIM // 2
    inv_freq = ROPE_THETA ** (-jnp.arange(0, ROT_DIM, 2, dtype=F32) / ROT_DIM)
    ang = positions.astype(F32).reshape(t, 1) * inv_freq
    cos, sin = jnp.cos(ang), jnp.sin(ang)
    zeros_r = jnp.zeros((t, LANES - ROT_DIM), F32)
    cos_t = jnp.concatenate([cos, cos, jnp.ones((t, LANES - ROT_DIM), F32)], axis=1)
    sa_t = jnp.concatenate([-sin, jnp.zeros((t, half), F32), zeros_r], axis=1)
    sb_t = jnp.concatenate([jnp.zeros((t, half), F32), sin, zeros_r], axis=1)
    bias = _branch_count_bias(seq, attn_blk, patterns)

    bias_col = router_bias.reshape(N_EXPERTS, 1).astype(F32)

    h = x.reshape(t, d_model)
    hb = h.astype(BF16)
    for i in range(depth):
        j = i // 2
        if i % 2 == 0:
            qkv = _qkv_rope(h, attn_w_qkv[j].astype(BF16), cos_t, sa_t, sb_t, d_model)
            att = _attention(qkv.reshape(bsz, seq, 3 * d_model), bias, n_heads, attn_blk)
            mix, w_mix = att.reshape(t, d_model), attn_w_o[j]
        else:
            d_inner = ssm_norm_w.shape[1]
            n_ssm_heads = ssm_dt_bias.shape[1]
            cdim = ssm_conv_w.shape[2]
            w_in = ssm_w_in[j].astype(BF16)
            z = _mm(hb, w_in, 0, d_inner, BF16, tn=1024)
            xbc = _mm(hb, w_in, d_inner, cdim, BF16, tn=1024)
            w_dt = jnp.pad(w_in[:, d_inner + cdim:], ((0, 0), (0, LANES - n_ssm_heads)))
            dt = _mm(hb, w_dt, 0, LANES, F32)
            hpad = (0, LANES - n_ssm_heads)
            head_of_ch = jnp.arange(d_inner, dtype=jnp.int32) // SSM_HEAD_DIM
            e_mat = (jnp.arange(LANES, dtype=jnp.int32)[:, None] == head_of_ch[None, :]).astype(BF16)
            y = _ssd(z.reshape(bsz, seq, d_inner), xbc.reshape(bsz, seq, cdim), dt.reshape(bsz, seq, LANES),
                     ssm_conv_w[j], ssm_conv_b[j].reshape(1, cdim),
                     jnp.pad(ssm_dt_bias[j], hpad).reshape(1, LANES), jnp.pad(ssm_A_log[j], hpad).reshape(1, LANES),
                     jnp.repeat(ssm_D[j], SSM_HEAD_DIM).reshape(1, d_inner), ssm_norm_w[j].reshape(1, d_inner), e_mat)
            mix, w_mix = y.reshape(t, d_inner), ssm_w_out[j]
        h, hp, e_idx, gate = _mm_res_ln_route(mix, w_mix.astype(BF16), h, ln_mix_g[i], ln_mix_b[i],
                                              router_w, bias_col, alpha)
        next_is_ssm = i + 1 < depth and (i + 1) % 2 == 1
        h, hb = _moe_layer(h, hp, e_idx, gate, moe_w_gate, moe_w_up, moe_w_down, i,
                           ln_ffn_g[i], ln_ffn_b[i], alpha, next_is_ssm)
    return h.reshape(bsz, seq, d_model)
```

```python
import functools
import math

import jax
import jax.numpy as jnp
from jax import lax
from jax.experimental import pallas as pl
from jax.experimental.pallas import tpu as pltpu

F32 = jnp.float32
BF16 = jnp.bfloat16
HIGHEST = lax.Precision.HIGHEST

LANES = 128
SUBLANES = 8
MXU_COLS = 256
MIB = 1024 * 1024

HEAD_DIM = 128
ROT_DIM = HEAD_DIM // 4
ROPE_THETA = 500000.0
N_EXPERTS = 64
N_GROUPS = 8
EPG = N_EXPERTS // N_GROUPS
TOP_K = 2
MOE_BLOCK = 256
SSM_HEAD_DIM = 64
SSM_GROUPS = 8
SSM_STATE = 128
SSM_CONV = 4
SSM_CHUNK = 128
LN_EPS = 1e-5
RMS_EPS = 1e-5
NEG = -1e30
LOG2E = math.log2(math.e)


def _params(sem, vmem_mib):
    return pltpu.CompilerParams(dimension_semantics=sem, vmem_limit_bytes=vmem_mib * MIB)


def _silu(x):
    hx = 0.5 * x
    return hx + hx * jnp.tanh(hx)


def _layer_norm_rows(y, g, b):
    mu = jnp.mean(y, axis=-1, keepdims=True)
    yc = y - mu
    var = jnp.mean(yc * yc, axis=-1, keepdims=True)
    return yc * lax.rsqrt(var + LN_EPS) * g + b


def _mm_kernel(a_ref, w_ref, o_ref):
    o_ref[...] = jnp.dot(a_ref[...], w_ref[...], preferred_element_type=F32).astype(o_ref.dtype)


def _mm(a, w, col0, n, out_dtype, tm=1024, tn=512):
    m, k = a.shape
    tn = min(tn, n)
    assert col0 % tn == 0 and n % tn == 0 and m % tm == 0
    c0 = col0 // tn
    return pl.pallas_call(
        _mm_kernel,
        grid=(m // tm, n // tn),
        in_specs=[pl.BlockSpec((tm, k), lambda i, j: (i, 0)),
                  pl.BlockSpec((k, tn), lambda i, j: (0, c0 + j))],
        out_specs=pl.BlockSpec((tm, tn), lambda i, j: (i, j)),
        out_shape=jax.ShapeDtypeStruct((m, n), out_dtype),
        compiler_params=_params(("parallel", "arbitrary"), 48),
        name="mm",
    )(a, w)


def _qkv_rope_kernel(x32_ref, w_ref, cos_ref, sa_ref, sb_ref, o_ref, x_ref, *, n_rope_tiles, n_q_tiles, scale):
    n = pl.program_id(1)
    tn = w_ref.shape[1]

    @pl.when(n == 0)
    def _():
        x_ref[...] = x32_ref[...].astype(BF16)

    @pl.when(n < n_rope_tiles)
    def _():
        c = cos_ref[...]
        sa = sa_ref[...]
        sb = sb_ref[...]
        sc = jnp.where(n < n_q_tiles, scale, 1.0).astype(F32)
        for p in range(tn // MXU_COLS):
            t2 = jnp.dot(x_ref[...], w_ref[:, p * MXU_COLS:(p + 1) * MXU_COLS], preferred_element_type=F32)
            for j in range(MXU_COLS // LANES):
                t = t2[:, j * LANES:(j + 1) * LANES]
                r = t * c + pltpu.roll(t, LANES - ROT_DIM // 2, 1) * sa + pltpu.roll(t, ROT_DIM // 2, 1) * sb
                col = p * MXU_COLS + j * LANES
                o_ref[:, col:col + LANES] = (r * sc).astype(o_ref.dtype)

    @pl.when(n >= n_rope_tiles)
    def _():
        o_ref[...] = jnp.dot(x_ref[...], w_ref[...], preferred_element_type=F32).astype(o_ref.dtype)


def _qkv_rope(x, w, cos_t, sa_t, sb_t, d_model, tm=1024, tn=1024):
    m, k = x.shape
    n = w.shape[1]
    kern = functools.partial(_qkv_rope_kernel, n_rope_tiles=2 * d_model // tn, n_q_tiles=d_model // tn,
                             scale=HEAD_DIM ** -0.5 * LOG2E)
    tab = pl.BlockSpec((tm, LANES), lambda i, j: (i, 0))
    return pl.pallas_call(
        kern,
        grid=(m // tm, n // tn),
        in_specs=[pl.BlockSpec((tm, k), lambda i, j: (i, 0)),
                  pl.BlockSpec((k, tn), lambda i, j: (0, j)),
                  tab, tab, tab],
        out_specs=pl.BlockSpec((tm, tn), lambda i, j: (i, j)),
        out_shape=jax.ShapeDtypeStruct((m, n), BF16),
        scratch_shapes=[pltpu.VMEM((tm, k), BF16)],
        compiler_params=_params(("parallel", "arbitrary"), 56),
        name="qkv_rope",
    )(x, w, cos_t, sa_t, sb_t)


def _attn_kernel(qi_ref, ki_ref, q_ref, k_ref, v_ref, bias_ref, o_ref, m_sc, l_sc, acc_sc, *, heads):
    p = pl.program_id(2)
    qi = qi_ref[p]
    ki = ki_ref[p]
    blk = q_ref.shape[1]

    @pl.when(ki == 0)
    def _():
        m_sc[...] = jnp.full_like(m_sc, 0.1 * NEG)
        l_sc[...] = jnp.zeros_like(l_sc)
        acc_sc[...] = jnp.zeros_like(acc_sc)

    bias = bias_ref[qi - ki]
    for j in range(heads):
        cols = slice(j * HEAD_DIM, (j + 1) * HEAD_DIM)
        s = lax.dot_general(q_ref[0, :, cols], k_ref[0, :, cols], (((1,), (1,)), ((), ())),
                            preferred_element_type=F32) + bias
        m_prev = m_sc[j]
        m_new = jnp.maximum(m_prev, jnp.max(s, axis=1, keepdims=True))
        alpha = jnp.exp2(m_prev - m_new)
        pr = jnp.exp2(s - jnp.concatenate([m_new] * (blk // LANES), axis=1))
        l_sc[j] = alpha * l_sc[j] + jnp.sum(pr, axis=1, keepdims=True)
        acc_sc[j] = alpha * acc_sc[j] + jnp.dot(pr.astype(BF16), v_ref[0, :, cols], preferred_element_type=F32)
        m_sc[j] = m_new

    @pl.when(ki == qi)
    def _():
        for j in range(heads):
            o_ref[0, :, j * HEAD_DIM:(j + 1) * HEAD_DIM] = (acc_sc[j] / l_sc[j]).astype(o_ref.dtype)


def _attention(qkv3, bias, n_heads, blk, heads=16):
    b, s, _ = qkv3.shape
    nq = s // blk
    pairs = [(qi, ki) for qi in range(nq) for ki in range(qi + 1)]
    qi_tab = jnp.asarray([pq for pq, _ in pairs], jnp.int32)
    ki_tab = jnp.asarray([pk for _, pk in pairs], jnp.int32)
    hg = n_heads // heads
    w = heads * HEAD_DIM
    return pl.pallas_call(
        functools.partial(_attn_kernel, heads=heads),
        grid_spec=pltpu.PrefetchScalarGridSpec(
            num_scalar_prefetch=2,
            grid=(b, hg, len(pairs)),
            in_specs=[pl.BlockSpec((1, blk, w), lambda bi, h, p, qt, kt: (bi, qt[p], h)),
                      pl.BlockSpec((1, blk, w), lambda bi, h, p, qt, kt: (bi, kt[p], hg + h)),
                      pl.BlockSpec((1, blk, w), lambda bi, h, p, qt, kt: (bi, kt[p], 2 * hg + h)),
                      pl.BlockSpec((nq, blk, blk), lambda bi, h, p, qt, kt: (0, 0, 0))],
            out_specs=pl.BlockSpec((1, blk, w), lambda bi, h, p, qt, kt: (bi, qt[p], h)),
            scratch_shapes=[pltpu.VMEM((heads, blk, LANES), F32), pltpu.VMEM((heads, blk, LANES), F32),
                            pltpu.VMEM((heads, blk, HEAD_DIM), F32)],
        ),
        out_shape=jax.ShapeDtypeStruct((b, s, n_heads * HEAD_DIM), BF16),
        compiler_params=_params(("parallel", "parallel", "arbitrary"), 48),
        name="attention",
    )(qi_tab, ki_tab, qkv3, qkv3, qkv3, bias)


def _branch_count_bias(s, blk, patterns):
    nq = s // blk
    off = jnp.arange(nq, dtype=jnp.int32)[:, None, None] * blk
    d = off + jnp.arange(blk, dtype=jnp.int32)[None, :, None] - jnp.arange(blk, dtype=jnp.int32)[None, None, :]
    cnt = jnp.zeros(d.shape, F32)
    for window, dilation in patterns:
        cnt = cnt + ((d >= 0) & (d % dilation == 0) & (d <= window)).astype(F32)
    return jnp.where(cnt > 0, jnp.log2(jnp.maximum(cnt, 1.0)), NEG)


def _pack_bf16_pairs(x):
    n = x.shape[1] // 2
    lo = lax.bitcast_convert_type(x[:, :n].astype(BF16).astype(F32), jnp.uint32)
    hi = lax.bitcast_convert_type(x[:, n:].astype(BF16).astype(F32), jnp.uint32)
    return (lo >> 16) | hi


def _unpack_bf16_pairs(w):
    lo = lax.bitcast_convert_type(w << 16, F32)
    hi = lax.bitcast_convert_type(w & jnp.uint32(0xFFFF0000), F32)
    return jnp.concatenate([lo, hi], axis=1)


def _mm_res_ln_route_kernel(a_ref, w_ref, h_ref, g_ref, b_ref, rwh_ref, rwl_ref, rb_ref,
                            of_ref, hp_ref, e_ref, gt_ref, *, alpha, n_sub):
    sub = a_ref.shape[0] // n_sub
    for r in range(n_sub):
        rows = slice(r * sub, (r + 1) * sub)
        mix = jnp.dot(a_ref[rows, :], w_ref[...], preferred_element_type=F32)
        out = _layer_norm_rows(alpha * h_ref[rows, :] + mix, g_ref[...], b_ref[...])
        of_ref[rows, :] = out
        hp_ref[rows, :] = _pack_bf16_pairs(out)
        experts, gates = _route_top2(out, rwh_ref[...], rwl_ref[...], rb_ref[...])
        for k in range(TOP_K):
            e_ref[k:k + 1, rows] = experts[k]
            gt_ref[k:k + 1, rows] = gates[k]


def _mm_res_ln_route(a, w, h, g, b, router_w, bias_col, alpha, tm=512, n_sub=2):
    m, kdim = a.shape
    n = w.shape[1]
    kern = functools.partial(_mm_res_ln_route_kernel, alpha=alpha, n_sub=n_sub)
    row = pl.BlockSpec((tm, n), lambda i: (i, 0))
    vec = pl.BlockSpec((1, n), lambda i: (0, 0))
    sel = pl.BlockSpec((TOP_K, tm), lambda i: (0, i))
    rw = jnp.pad(router_w, ((0, 0), (0, LANES - N_EXPERTS)))
    rw_hi = rw.astype(BF16)
    rw_lo = (rw - rw_hi.astype(F32)).astype(BF16)
    rw_spec = pl.BlockSpec((n, LANES), lambda i: (0, 0))
    return pl.pallas_call(
        kern,
        grid=(m // tm,),
        in_specs=[pl.BlockSpec((tm, kdim), lambda i: (i, 0)),
                  pl.BlockSpec((kdim, n), lambda i: (0, 0), pipeline_mode=pl.Buffered(1)),
                  row, vec, vec,
                  rw_spec, rw_spec, pl.BlockSpec((N_EXPERTS, 1), lambda i: (0, 0))],
        out_specs=[row, pl.BlockSpec((tm, n // 2), lambda i: (i, 0)), sel, sel],
        out_shape=[jax.ShapeDtypeStruct((m, n), F32), jax.ShapeDtypeStruct((m, n // 2), jnp.uint32),
                   jax.ShapeDtypeStruct((TOP_K, m), jnp.int32), jax.ShapeDtypeStruct((TOP_K, m), F32)],
        compiler_params=_params(("parallel",), 56),
        name="mm_res_ln_route",
    )(a, w, h, g.reshape(1, n), b.reshape(1, n), rw_hi, rw_lo, bias_col)


def _route_top2(h, w_hi, w_lo, bias_col):
    h_hi = h.astype(BF16)
    h_lo = (h - h_hi.astype(F32)).astype(BF16)
    dot = lambda a, b: jnp.dot(a, b, preferred_element_type=F32)
    logits = (dot(h_hi, w_hi) + dot(h_lo, w_hi) + dot(h_hi, w_lo)).T[:N_EXPERTS, :]
    mx = jnp.max(logits, axis=0, keepdims=True)
    ex = jnp.exp(logits - mx)
    scores = ex / jnp.sum(ex, axis=0, keepdims=True)
    sel = scores + bias_col
    tm = sel.shape[1]
    li = lax.broadcasted_iota(jnp.int32, (EPG, tm), 0)

    def first_argmax(v):
        m = jnp.max(v, axis=0, keepdims=True)
        return m, jnp.min(jnp.where(v == m, li, EPG), axis=0, keepdims=True)

    gs, i1s, i2s = [], [], []
    for g in range(N_GROUPS):
        slab = sel[g * EPG:(g + 1) * EPG, :]
        m1, i1 = first_argmax(slab)
        m2, i2 = first_argmax(jnp.where(li == i1, NEG, slab))
        gs.append(m1 + m2)
        i1s.append(i1)
        i2s.append(i2)
    best, gidx = gs[0], jnp.zeros((1, tm), jnp.int32)
    for g in range(1, N_GROUPS):
        better = gs[g] > best
        best = jnp.where(better, gs[g], best)
        gidx = jnp.where(better, g, gidx)
    l1 = jnp.zeros((1, tm), jnp.int32)
    l2 = jnp.zeros((1, tm), jnp.int32)
    s1 = jnp.zeros((1, tm), F32)
    s2 = jnp.zeros((1, tm), F32)
    for g in range(N_GROUPS):
        here = gidx == g
        slab = scores[g * EPG:(g + 1) * EPG, :]
        l1 = jnp.where(here, i1s[g], l1)
        l2 = jnp.where(here, i2s[g], l2)
        s1 = jnp.where(here, jnp.sum(jnp.where(li == i1s[g], slab, 0.0), axis=0, keepdims=True), s1)
        s2 = jnp.where(here, jnp.sum(jnp.where(li == i2s[g], slab, 0.0), axis=0, keepdims=True), s2)
    tot = s1 + s2
    return (gidx * EPG + l1, gidx * EPG + l2), (s1 / tot, s2 / tot)


N_ROW_BUFS = 2
N_SPARE_ROWS = N_ROW_BUFS * MOE_BLOCK
N_WEIGHT_SETS = 2
WEIGHT_DMA_PRIORITY = 1


def _experts_kernel(be_ref, wset_ref, ahead_ref, lead_ref, nused_ref, src_ref, dst_ref,
                    h_hbm, wg_hbm, wu_hbm, wd_hbm, ys_hbm,
                    xb0, xb1, ob0, ob1, wgl, wul, wdl, wgb, wub, wdb, gsem, ssem, wsem, *, layer, n_slots):
    i = pl.program_id(0)
    n_used = nused_ref[0]
    xbufs = (xb0, xb1)
    obufs = (ob0, ob1)

    def gather_start(blk, s):
        base = (blk + 1) * MOE_BLOCK
        for r in range(MOE_BLOCK):
            pltpu.make_async_copy(h_hbm.at[pl.ds(src_ref[base + r], 1), :],
                                  xbufs[s].at[pl.ds(r, 1), :], gsem.at[s]).start()

    def scatter_start(blk, s):
        base = (blk + 1) * MOE_BLOCK
        for r in range(MOE_BLOCK):
            pltpu.make_async_copy(obufs[s].at[pl.ds(r, 1), :],
                                  ys_hbm.at[pl.ds(dst_ref[base + r], 1), :], ssem.at[s]).start()

    def gather_wait(s):
        pltpu.make_async_copy(h_hbm.at[pl.ds(0, MOE_BLOCK), :], xbufs[s], gsem.at[s]).wait()

    def scatter_wait(s):
        pltpu.make_async_copy(obufs[s], ys_hbm.at[pl.ds(0, MOE_BLOCK), :], ssem.at[s]).wait()

    w_hbm = (wg_hbm, wu_hbm, wd_hbm)
    w_land = (wgl, wul, wdl)
    w_work = (wgb, wub, wdb)

    def weight_copy(e, p, k):
        return pltpu.make_async_copy(w_hbm[k].at[layer, e], w_land[k].at[p], wsem.at[p, k])

    @pl.when(i == 0)
    def _():
        ob1[...] = jnp.zeros_like(ob1)
        spare = pltpu.make_async_copy(ob1, ys_hbm.at[pl.ds(n_slots + MOE_BLOCK, MOE_BLOCK), :], ssem.at[1])
        spare.start()
        spare.wait()
        gather_start(0, 0)
        for p in range(N_WEIGHT_SETS):
            @pl.when(lead_ref[p] >= 0)
            def _():
                for k in range(len(w_hbm)):
                    weight_copy(lead_ref[p], p, k).start(priority=WEIGHT_DMA_PRIORITY)

    @pl.when(i < n_used)
    def _():
        for p in range(N_WEIGHT_SETS):
            @pl.when(wset_ref[i] == p)
            def _():
                ahead = ahead_ref[i]
                for k in range(len(w_hbm)):
                    weight_copy(be_ref[i], p, k).wait()
                    w_work[k][...] = w_land[k][p].astype(BF16)

                    @pl.when(ahead >= 0)
                    def _():
                        weight_copy(ahead, p, k).start(priority=WEIGHT_DMA_PRIORITY)

        for s in range(N_ROW_BUFS):
            @pl.when(i % N_ROW_BUFS == s)
            def _():
                gather_wait(s)

                @pl.when(i >= 1)
                def _():
                    scatter_wait(s)

                x = _unpack_bf16_pairs(xbufs[s][...]).astype(BF16)
                hg = jnp.dot(x, wgb[...], preferred_element_type=F32)
                hu = jnp.dot(x, wub[...], preferred_element_type=F32)
                act = (_silu(hg) * hu).astype(BF16)
                obufs[s][...] = _pack_bf16_pairs(jnp.dot(act, wdb[...], preferred_element_type=F32))
                gather_start(i + 1, 1 - s)
                scatter_start(i - 1, 1 - s)

                @pl.when(i == n_used - 1)
                def _():
                    gather_wait(1 - s)
                    scatter_wait(1 - s)
                    scatter_start(i, s)
                    scatter_wait(s)


def _experts(meta, hp, w_gate, w_up, w_down, layer, n_slots):
    d, de = w_gate.shape[2], w_gate.shape[3]
    nb = meta[0].shape[0]
    any_spec = pl.BlockSpec(memory_space=pl.ANY)
    rowbuf = pltpu.VMEM((MOE_BLOCK, hp.shape[1]), jnp.uint32)
    return pl.pallas_call(
        functools.partial(_experts_kernel, layer=layer, n_slots=n_slots),
        grid_spec=pltpu.PrefetchScalarGridSpec(
            num_scalar_prefetch=len(meta),
            grid=(nb,),
            in_specs=[any_spec, any_spec, any_spec, any_spec],
            out_specs=any_spec,
            scratch_shapes=[rowbuf, rowbuf, rowbuf, rowbuf,
                            pltpu.VMEM((N_WEIGHT_SETS, d, de), F32), pltpu.VMEM((N_WEIGHT_SETS, d, de), F32),
                            pltpu.VMEM((N_WEIGHT_SETS, de, d), F32),
                            pltpu.VMEM((d, de), BF16), pltpu.VMEM((d, de), BF16), pltpu.VMEM((de, d), BF16),
                            pltpu.SemaphoreType.DMA((N_ROW_BUFS,)), pltpu.SemaphoreType.DMA((N_ROW_BUFS,)),
                            pltpu.SemaphoreType.DMA((N_WEIGHT_SETS, 3))],
        ),
        out_shape=jax.ShapeDtypeStruct((n_slots + N_SPARE_ROWS, hp.shape[1]), jnp.uint32),
        compiler_params=_params(("arbitrary",), 56),
        name="experts",
    )(*meta, hp, w_gate, w_up, w_down)


def _combine_ln_kernel(y0_ref, y1_ref, gate_ref, h_ref, g_ref, b_ref, of_ref, *maybe_ob_ref, alpha):
    gt = gate_ref[...]
    moe = gt[:, 0:1] * _unpack_bf16_pairs(y0_ref[...]) + gt[:, 1:2] * _unpack_bf16_pairs(y1_ref[...])
    out = _layer_norm_rows(alpha * h_ref[...] + moe, g_ref[...], b_ref[...])
    of_ref[...] = out
    for ob_ref in maybe_ob_ref:
        ob_ref[...] = out.astype(BF16)


def _combine_ln(ys, gate_t, h, g, b, alpha, emit_bf16, tm=512):
    t, d = h.shape
    row = pl.BlockSpec((tm, d), lambda i: (i, 0))
    vec = pl.BlockSpec((1, d), lambda i: (0, 0))
    dp = ys.shape[1]
    out_shape = [jax.ShapeDtypeStruct((t, d), F32)] + ([jax.ShapeDtypeStruct((t, d), BF16)] if emit_bf16 else [])
    res = pl.pallas_call(
        functools.partial(_combine_ln_kernel, alpha=alpha),
        grid=(t // tm,),
        in_specs=[pl.BlockSpec((tm, dp), lambda i: (i, 0)), pl.BlockSpec((tm, dp), lambda i: (t // tm + i, 0)),
                  pl.BlockSpec((tm, TOP_K), lambda i: (i, 0)), row, vec, vec],
        out_specs=[row] * len(out_shape),
        out_shape=out_shape,
        compiler_params=_params(("parallel",), 40),
        name="combine_ln",
    )(ys, ys, gate_t, h, g.reshape(1, d), b.reshape(1, d))
    return (res[0], res[1]) if emit_bf16 else (res[0], None)


def _invert_rows_kernel(dest_ref, out_ref):
    def clear(r, carry):
        out_ref[r] = -1
        return carry

    def place(s, carry):
        out_ref[dest_ref[s]] = s
        return carry

    lax.fori_loop(0, out_ref.shape[0], clear, 0, unroll=32)
    lax.fori_loop(0, dest_ref.shape[0], place, 0, unroll=16)


def _invert_rows(dest, rows):
    smem = pl.BlockSpec(memory_space=pltpu.SMEM)
    return pl.pallas_call(
        _invert_rows_kernel,
        in_specs=[smem],
        out_specs=smem,
        out_shape=jax.ShapeDtypeStruct((rows,), jnp.int32),
        name="invert_rows",
    )(dest)


def _moe_layer(h, hp, e_idx, gate, w_gate, w_up, w_down, layer, ln_g, ln_b, alpha, emit_bf16):
    t, d = h.shape
    n_slots = t * TOP_K
    e_flat = e_idx.reshape(n_slots)
    experts = jnp.arange(N_EXPERTS, dtype=jnp.int32)
    onehot = (e_flat[:, None] == experts[None, :]).astype(jnp.int32)
    csum = jnp.cumsum(onehot, axis=0)
    counts = csum[-1]
    padded = ((counts + MOE_BLOCK - 1) // MOE_BLOCK) * MOE_BLOCK
    pad_end = jnp.cumsum(padded)
    pad_start = pad_end - padded
    dest = jnp.sum(onehot * (csum - 1 + pad_start[None, :]), axis=1).astype(jnp.int32)
    n_blocks = n_slots // MOE_BLOCK + N_EXPERTS
    rows = n_blocks * MOE_BLOCK
    row_slot = _invert_rows(dest, rows)
    row_id = jnp.arange(-MOE_BLOCK, rows, dtype=jnp.int32)
    row_slot = jnp.concatenate([jnp.full((MOE_BLOCK,), -1, jnp.int32), row_slot])
    valid = row_slot >= 0
    row_src = jnp.where(valid, row_slot % t, row_id % t)
    spare = n_slots + ((row_id // MOE_BLOCK + 1) % N_ROW_BUFS) * MOE_BLOCK + row_id % MOE_BLOCK
    row_dst = jnp.where(valid, row_slot, spare)
    blk_row0 = jnp.arange(n_blocks, dtype=jnp.int32) * MOE_BLOCK
    block_e = jnp.sum((pad_end[None, :] <= blk_row0[:, None]).astype(jnp.int32), axis=1)
    block_e = jnp.minimum(block_e, N_EXPERTS - 1)
    first = jnp.concatenate([jnp.ones((1,), bool), block_e[1:] != block_e[:-1]])
    nonempty = counts > 0
    place = jnp.cumsum(nonempty.astype(jnp.int32)) - 1
    blk_place = jnp.sum(jnp.where(block_e[:, None] == experts[None, :], place[None, :], 0), axis=1)
    wset = jnp.where(first, blk_place % N_WEIGHT_SETS, -1).astype(jnp.int32)

    def expert_at(p):
        hit = nonempty[None, :] & (place[None, :] == p.reshape(-1, 1))
        return (jnp.sum(jnp.where(hit, experts[None, :] + 1, 0), axis=1) - 1).astype(jnp.int32).reshape(p.shape)

    ahead = expert_at(blk_place + N_WEIGHT_SETS)
    lead = expert_at(jnp.arange(N_WEIGHT_SETS, dtype=jnp.int32))
    n_used = (pad_end[-1] // MOE_BLOCK).astype(jnp.int32).reshape(1)
    meta = (block_e, wset, ahead, lead, n_used, row_src.astype(jnp.int32), row_dst.astype(jnp.int32))
    ys = _experts(meta, hp, w_gate, w_up, w_down, layer, n_slots)
    return _combine_ln(ys, gate.T, h, ln_g, ln_b, alpha, emit_bf16)


def _expand_heads(x, e_bf16):
    hi = x.astype(BF16)
    r1 = x - hi.astype(F32)
    mid = r1.astype(BF16)
    lo = (r1 - mid.astype(F32)).astype(BF16)
    dot = lambda a: jnp.dot(a, e_bf16, preferred_element_type=F32)
    return dot(hi) + dot(mid) + dot(lo)


def _ssd_kernel(z_ref, xbc_ref, prev_ref, dt_ref, cw_ref, cb_ref, dtb_ref, alog_ref, dexp_ref, nw_ref, e_ref,
                o_ref, state_sc, *, d_inner):
    c = pl.program_id(1)
    L = SSM_CHUNK
    gw = d_inner // SSM_GROUPS
    hpg = gw // SSM_HEAD_DIM

    @pl.when(c == 0)
    def _():
        state_sc[...] = jnp.zeros_like(state_sc)

    has_prev = c > 0
    tile = SUBLANES
    row_in_tile = lax.broadcasted_iota(jnp.int32, (tile, 1), 0)

    def conv_silu(col0, width):
        cols = slice(col0, col0 + width)
        x = xbc_ref[0, :, cols].astype(F32)
        tail = jnp.where(has_prev, prev_ref[0, prev_ref.shape[1] - tile:, cols].astype(F32), 0.0)
        acc = cb_ref[:, cols] + x * cw_ref[SSM_CONV - 1:SSM_CONV, cols]
        for j in range(1, SSM_CONV):
            xr = pltpu.roll(x, j, 0)
            head = jnp.where(row_in_tile < j, pltpu.roll(tail, j, 0), xr[0:tile, :])
            shifted = jnp.concatenate([head, xr[tile:, :]], axis=0)
            acc = acc + shifted * cw_ref[SSM_CONV - 1 - j:SSM_CONV - j, cols]
        return _silu(acc)

    dtv = dt_ref[0] + dtb_ref[...]
    dt = jnp.maximum(dtv, 0.0) + jnp.log1p(jnp.exp(-jnp.abs(dtv)))
    a = dt * (-jnp.exp(alog_ref[...]) * LOG2E)
    ri = lax.broadcasted_iota(jnp.int32, (L, L), 0)
    ci = lax.broadcasted_iota(jnp.int32, (L, L), 1)
    causal = ri >= ci
    a_cum = jnp.dot(causal.astype(F32), a, precision=HIGHEST, preferred_element_type=F32)
    a_cum_t = a_cum.T
    e_mat = e_ref[...]
    acum_x = _expand_heads(a_cum, e_mat)
    dt_x = _expand_heads(dt, e_mat)
    lane = lax.broadcasted_iota(jnp.int32, (L, LANES), 1)
    first_head = lane < SSM_HEAD_DIM

    for g in range(SSM_GROUPS):
        cols = slice(g * gw, (g + 1) * gw)
        xg = conv_silu(g * gw, gw)
        bg = conv_silu(d_inner + g * SSM_STATE, SSM_STATE)
        cg = conv_silu(d_inner + (SSM_GROUPS + g) * SSM_STATE, SSM_STATE)
        ac = acum_x[:, cols]
        last = ac[L - 1:L, :]
        xdt = xg * dt_x[:, cols]
        cb16 = cg.astype(BF16)
        bb16 = bg.astype(BF16)
        cbm = lax.dot_general(cb16, bb16, (((1,), (1,)), ((), ())), preferred_element_type=F32)
        st = state_sc[g]
        y_off = jnp.dot(cb16, st.astype(BF16), preferred_element_type=F32) * jnp.exp2(ac)
        xd = (xdt * jnp.exp2(last - ac)).astype(BF16)
        state_sc[g] = st * jnp.exp2(last) + jnp.dot(bg.T.astype(BF16), xd, preferred_element_type=F32)
        xdt16 = xdt.astype(BF16)
        pieces = []
        for jp in range(hpg // 2):
            lhs = []
            for hh in (g * hpg + 2 * jp, g * hpg + 2 * jp + 1):
                seg = jnp.where(causal, a_cum[:, hh:hh + 1] - a_cum_t[hh:hh + 1, :], NEG)
                lhs.append((cbm * jnp.exp2(seg)).astype(BF16))
            xp = xdt16[:, jp * LANES:(jp + 1) * LANES]
            zero = jnp.zeros_like(xp)
            rhs = jnp.concatenate([jnp.where(first_head, xp, zero), jnp.where(first_head, zero, xp)], axis=0)
            pieces.append(jnp.dot(jnp.concatenate(lhs, axis=1), rhs, preferred_element_type=F32))
        y = jnp.concatenate(pieces, axis=1) + y_off + dexp_ref[:, cols] * xg
        y = y * _silu(z_ref[0, :, cols].astype(F32))
        y = y * lax.rsqrt(jnp.mean(y * y, axis=1, keepdims=True) + RMS_EPS) * nw_ref[:, cols]
        o_ref[0, :, cols] = y.astype(o_ref.dtype)


def _ssd(z3, xbc3, dt3, conv_w, conv_b, dt_bias_p, a_log_p, d_exp, norm_w, e_mat):
    b, s, d_inner = z3.shape
    cdim = xbc3.shape[2]
    nc = s // SSM_CHUNK
    gw = d_inner // SSM_GROUPS
    prev_rows = SUBLANES * (4 // xbc3.dtype.itemsize)
    rows_per_prev = SSM_CHUNK // prev_rows
    full = lambda shape: pl.BlockSpec(shape, lambda bi, c: tuple(0 for _ in shape))
    return pl.pallas_call(
        functools.partial(_ssd_kernel, d_inner=d_inner),
        grid=(b, nc),
        in_specs=[pl.BlockSpec((1, SSM_CHUNK, d_inner), lambda bi, c: (bi, c, 0)),
                  pl.BlockSpec((1, SSM_CHUNK, cdim), lambda bi, c: (bi, c, 0)),
                  pl.BlockSpec((1, prev_rows, cdim), lambda bi, c: (bi, jnp.maximum(c * rows_per_prev - 1, 0), 0)),
                  pl.BlockSpec((1, SSM_CHUNK, LANES), lambda bi, c: (bi, c, 0)),
                  full((SSM_CONV, cdim)), full((1, cdim)), full((1, LANES)), full((1, LANES)),
                  full((1, d_inner)), full((1, d_inner)), full((LANES, d_inner))],
        out_specs=pl.BlockSpec((1, SSM_CHUNK, d_inner), lambda bi, c: (bi, c, 0)),
        out_shape=jax.ShapeDtypeStruct((b, s, d_inner), BF16),
        scratch_shapes=[pltpu.VMEM((SSM_GROUPS, SSM_STATE, gw), F32)],
        compiler_params=_params(("parallel", "arbitrary"), 56),
        name="ssd",
    )(z3, xbc3, xbc3, dt3, conv_w, conv_b, dt_bias_p, a_log_p, d_exp, norm_w, e_mat)


def kernel(x, positions, attn_w_qkv, attn_w_o, ssm_w_in, ssm_conv_w, ssm_conv_b, ssm_dt_bias, ssm_A_log, ssm_D, ssm_norm_w, ssm_w_out, router_w, router_bias, moe_w_gate, moe_w_up, moe_w_down, ln_mix_g, ln_mix_b, ln_ffn_g, ln_ffn_b):
    bsz, seq, d_model = x.shape
    t = bsz * seq
    depth = moe_w_gate.shape[0]
    alpha = (2 * depth) ** 0.25
    n_heads = d_model // HEAD_DIM
    patterns = ((128, 1), (512, 4), (2048, 16))
    attn_blk = 512

    half = ROT_DIM // 2
    inv_freq = ROPE_THETA ** (-jnp.arange(0, ROT_DIM, 2, dtype=F32) / ROT_DIM)
    ang = positions.astype(F32).reshape(t, 1) * inv_freq
    cos, sin = jnp.cos(ang), jnp.sin(ang)
    zeros_r = jnp.zeros((t, LANES - ROT_DIM), F32)
    cos_t = jnp.concatenate([cos, cos, jnp.ones((t, LANES - ROT_DIM), F32)], axis=1)
    sa_t = jnp.concatenate([-sin, jnp.zeros((t, half), F32), zeros_r], axis=1)
    sb_t = jnp.concatenate([jnp.zeros((t, half), F32), sin, zeros_r], axis=1)
    bias = _branch_count_bias(seq, attn_blk, patterns)

    bias_col = router_bias.reshape(N_EXPERTS, 1).astype(F32)

    h = x.reshape(t, d_model)
    hb = h.astype(BF16)
    for i in range(depth):
        j = i // 2
        if i % 2 == 0:
            qkv = _qkv_rope(h, attn_w_qkv[j].astype(BF16), cos_t, sa_t, sb_t, d_model)
            att = _attention(qkv.reshape(bsz, seq, 3 * d_model), bias, n_heads, attn_blk)
            mix, w_mix = att.reshape(t, d_model), attn_w_o[j]
        else:
            d_inner = ssm_norm_w.shape[1]
            n_ssm_heads = ssm_dt_bias.shape[1]
            cdim = ssm_conv_w.shape[2]
            w_in = ssm_w_in[j].astype(BF16)
            z = _mm(hb, w_in, 0, d_inner, BF16, tn=2048)
            xbc = _mm(hb, w_in, d_inner, cdim, BF16, tn=2048)
            w_dt = jnp.pad(w_in[:, d_inner + cdim:], ((0, 0), (0, LANES - n_ssm_heads)))
            dt = _mm(hb, w_dt, 0, LANES, F32)
            hpad = (0, LANES - n_ssm_heads)
            head_of_ch = jnp.arange(d_inner, dtype=jnp.int32) // SSM_HEAD_DIM
            e_mat = (jnp.arange(LANES, dtype=jnp.int32)[:, None] == head_of_ch[None, :]).astype(BF16)
            y = _ssd(z.reshape(bsz, seq, d_inner), xbc.reshape(bsz, seq, cdim), dt.reshape(bsz, seq, LANES),
                     ssm_conv_w[j], ssm_conv_b[j].reshape(1, cdim),
                     jnp.pad(ssm_dt_bias[j], hpad).reshape(1, LANES), jnp.pad(ssm_A_log[j], hpad).reshape(1, LANES),
                     jnp.repeat(ssm_D[j], SSM_HEAD_DIM).reshape(1, d_inner), ssm_norm_w[j].reshape(1, d_inner), e_mat)
            mix, w_mix = y.reshape(t, d_inner), ssm_w_out[j]
        h, hp, e_idx, gate = _mm_res_ln_route(mix, w_mix.astype(BF16), h, ln_mix_g[i], ln_mix_b[i],
                                              router_w, bias_col, alpha)
        next_is_ssm = i + 1 < depth and (i + 1) % 2 == 1
        h, hb = _moe_layer(h, hp, e_idx, gate, moe_w_gate, moe_w_up, moe_w_down, i,
                           ln_ffn_g[i], ln_ffn_b[i], alpha, next_is_ssm)
    return h.reshape(bsz, seq, d_model)
```

```python
import functools
import math

import jax
import jax.numpy as jnp
from jax import lax
from jax.experimental import pallas as pl
from jax.experimental.pallas import tpu as pltpu

F32 = jnp.float32
BF16 = jnp.bfloat16
HIGHEST = lax.Precision.HIGHEST

LANES = 128
SUBLANES = 8
MXU_COLS = 256
MIB = 1024 * 1024

HEAD_DIM = 128
ROT_DIM = HEAD_DIM // 4
ROPE_THETA = 500000.0
N_EXPERTS = 64
N_GROUPS = 8
EPG = N_EXPERTS // N_GROUPS
TOP_K = 2
MOE_BLOCK = 256
SSM_HEAD_DIM = 64
SSM_GROUPS = 8
SSM_STATE = 128
SSM_CONV = 4
SSM_CHUNK = 128
LN_EPS = 1e-5
RMS_EPS = 1e-5
NEG = -1e30
LOG2E = math.log2(math.e)


def _params(sem, vmem_mib):
    return pltpu.CompilerParams(dimension_semantics=sem, vmem_limit_bytes=vmem_mib * MIB)


def _silu(x):
    hx = 0.5 * x
    return hx + hx * jnp.tanh(hx)


def _layer_norm_rows(y, g, b):
    mu = jnp.mean(y, axis=-1, keepdims=True)
    yc = y - mu
    var = jnp.mean(yc * yc, axis=-1, keepdims=True)
    return yc * lax.rsqrt(var + LN_EPS) * g + b


def _mm_kernel(a_ref, w_ref, o_ref):
    o_ref[...] = jnp.dot(a_ref[...], w_ref[...], preferred_element_type=F32).astype(o_ref.dtype)


def _mm(a, w, col0, n, out_dtype, tm=1024, tn=512):
    m, k = a.shape
    tn = min(tn, n)
    assert col0 % tn == 0 and n % tn == 0 and m % tm == 0
    c0 = col0 // tn
    return pl.pallas_call(
        _mm_kernel,
        grid=(m // tm, n // tn),
        in_specs=[pl.BlockSpec((tm, k), lambda i, j: (i, 0)),
                  pl.BlockSpec((k, tn), lambda i, j: (0, c0 + j))],
        out_specs=pl.BlockSpec((tm, tn), lambda i, j: (i, j)),
        out_shape=jax.ShapeDtypeStruct((m, n), out_dtype),
        compiler_params=_params(("parallel", "arbitrary"), 48),
        name="mm",
    )(a, w)


def _qkv_rope_kernel(x32_ref, w_ref, cos_ref, sa_ref, sb_ref, o_ref, x_ref, *, n_rope_tiles, n_q_tiles, scale):
    n = pl.program_id(1)
    tn = w_ref.shape[1]

    @pl.when(n == 0)
    def _():
        x_ref[...] = x32_ref[...].astype(BF16)

    @pl.when(n < n_rope_tiles)
    def _():
        c = cos_ref[...]
        sa = sa_ref[...]
        sb = sb_ref[...]
        sc = jnp.where(n < n_q_tiles, scale, 1.0).astype(F32)
        for p in range(tn // MXU_COLS):
            t2 = jnp.dot(x_ref[...], w_ref[:, p * MXU_COLS:(p + 1) * MXU_COLS], preferred_element_type=F32)
            for j in range(MXU_COLS // LANES):
                t = t2[:, j * LANES:(j + 1) * LANES]
                r = t * c + pltpu.roll(t, LANES - ROT_DIM // 2, 1) * sa + pltpu.roll(t, ROT_DIM // 2, 1) * sb
                col = p * MXU_COLS + j * LANES
                o_ref[:, col:col + LANES] = (r * sc).astype(o_ref.dtype)

    @pl.when(n >= n_rope_tiles)
    def _():
        o_ref[...] = jnp.dot(x_ref[...], w_ref[...], preferred_element_type=F32).astype(o_ref.dtype)


def _qkv_rope(x, w, cos_t, sa_t, sb_t, d_model, tm=1024, tn=1024):
    m, k = x.shape
    n = w.shape[1]
    kern = functools.partial(_qkv_rope_kernel, n_rope_tiles=2 * d_model // tn, n_q_tiles=d_model // tn,
                             scale=HEAD_DIM ** -0.5 * LOG2E)
    tab = pl.BlockSpec((tm, LANES), lambda i, j: (i, 0))
    return pl.pallas_call(
        kern,
        grid=(m // tm, n // tn),
        in_specs=[pl.BlockSpec((tm, k), lambda i, j: (i, 0)),
                  pl.BlockSpec((k, tn), lambda i, j: (0, j)),
                  tab, tab, tab],
        out_specs=pl.BlockSpec((tm, tn), lambda i, j: (i, j)),
        out_shape=jax.ShapeDtypeStruct((m, n), BF16),
        scratch_shapes=[pltpu.VMEM((tm, k), BF16)],
        compiler_params=_params(("parallel", "arbitrary"), 56),
        name="qkv_rope",
    )(x, w, cos_t, sa_t, sb_t)


def _attn_kernel(qi_ref, ki_ref, q_ref, k_ref, v_ref, bias_ref, o_ref, m_sc, l_sc, acc_sc, *, heads):
    p = pl.program_id(2)
    qi = qi_ref[p]
    ki = ki_ref[p]
    blk = q_ref.shape[1]

    @pl.when(ki == 0)
    def _():
        m_sc[...] = jnp.full_like(m_sc, 0.1 * NEG)
        l_sc[...] = jnp.zeros_like(l_sc)
        acc_sc[...] = jnp.zeros_like(acc_sc)

    bias = bias_ref[qi - ki]
    ones_tile = (lax.broadcasted_iota(jnp.int32, (blk, LANES), 1) == 0).astype(BF16)
    for j in range(heads):
        cols = slice(j * HEAD_DIM, (j + 1) * HEAD_DIM)
        s = lax.dot_general(q_ref[0, :, cols], k_ref[0, :, cols], (((1,), (1,)), ((), ())),
                            preferred_element_type=F32) + bias
        m_prev = m_sc[j]
        m_new = jnp.maximum(m_prev, jnp.max(s, axis=1, keepdims=True))
        alpha = jnp.exp2(m_prev - m_new)
        pr = jnp.exp2(s - jnp.concatenate([m_new] * (blk // LANES), axis=1))
        pv = jnp.dot(pr.astype(BF16), jnp.concatenate([v_ref[0, :, cols], ones_tile], axis=1),
                     preferred_element_type=F32)
        acc_sc[j] = alpha * acc_sc[j] + pv[:, :HEAD_DIM]
        l_sc[j] = alpha * l_sc[j] + pv[:, HEAD_DIM:]
        m_sc[j] = m_new

    @pl.when(ki == qi)
    def _():
        for j in range(heads):
            o_ref[0, :, j * HEAD_DIM:(j + 1) * HEAD_DIM] = (acc_sc[j] / l_sc[j][:, 0:1]).astype(o_ref.dtype)


def _attention(qkv3, bias, n_heads, blk, heads=16):
    b, s, _ = qkv3.shape
    nq = s // blk
    pairs = [(qi, ki) for qi in range(nq) for ki in range(qi + 1)]
    qi_tab = jnp.asarray([pq for pq, _ in pairs], jnp.int32)
    ki_tab = jnp.asarray([pk for _, pk in pairs], jnp.int32)
    hg = n_heads // heads
    w = heads * HEAD_DIM
    return pl.pallas_call(
        functools.partial(_attn_kernel, heads=heads),
        grid_spec=pltpu.PrefetchScalarGridSpec(
            num_scalar_prefetch=2,
            grid=(b, hg, len(pairs)),
            in_specs=[pl.BlockSpec((1, blk, w), lambda bi, h, p, qt, kt: (bi, qt[p], h)),
                      pl.BlockSpec((1, blk, w), lambda bi, h, p, qt, kt: (bi, kt[p], hg + h)),
                      pl.BlockSpec((1, blk, w), lambda bi, h, p, qt, kt: (bi, kt[p], 2 * hg + h)),
                      pl.BlockSpec((nq, blk, blk), lambda bi, h, p, qt, kt: (0, 0, 0))],
            out_specs=pl.BlockSpec((1, blk, w), lambda bi, h, p, qt, kt: (bi, qt[p], h)),
            scratch_shapes=[pltpu.VMEM((heads, blk, LANES), F32), pltpu.VMEM((heads, blk, LANES), F32),
                            pltpu.VMEM((heads, blk, HEAD_DIM), F32)],
        ),
        out_shape=jax.ShapeDtypeStruct((b, s, n_heads * HEAD_DIM), BF16),
        compiler_params=_params(("parallel", "parallel", "arbitrary"), 48),
        name="attention",
    )(qi_tab, ki_tab, qkv3, qkv3, qkv3, bias)


def _branch_count_bias(s, blk, patterns):
    nq = s // blk
    off = jnp.arange(nq, dtype=jnp.int32)[:, None, None] * blk
    d = off + jnp.arange(blk, dtype=jnp.int32)[None, :, None] - jnp.arange(blk, dtype=jnp.int32)[None, None, :]
    cnt = jnp.zeros(d.shape, F32)
    for window, dilation in patterns:
        cnt = cnt + ((d >= 0) & (d % dilation == 0) & (d <= window)).astype(F32)
    return jnp.where(cnt > 0, jnp.log2(jnp.maximum(cnt, 1.0)), NEG)


def _pack_bf16_pairs(x):
    n = x.shape[1] // 2
    lo = lax.bitcast_convert_type(x[:, :n].astype(BF16).astype(F32), jnp.uint32)
    hi = lax.bitcast_convert_type(x[:, n:].astype(BF16).astype(F32), jnp.uint32)
    return (lo >> 16) | hi


def _unpack_bf16_pairs(w):
    lo = lax.bitcast_convert_type(w << 16, F32)
    hi = lax.bitcast_convert_type(w & jnp.uint32(0xFFFF0000), F32)
    return jnp.concatenate([lo, hi], axis=1)


def _mm_res_ln_route_kernel(a_ref, w_ref, h_ref, g_ref, b_ref, rwh_ref, rwl_ref, rb_ref,
                            of_ref, hp_ref, e_ref, gt_ref, *, alpha, n_sub):
    sub = a_ref.shape[0] // n_sub
    for r in range(n_sub):
        rows = slice(r * sub, (r + 1) * sub)
        mix = jnp.dot(a_ref[rows, :], w_ref[...], preferred_element_type=F32)
        out = _layer_norm_rows(alpha * h_ref[rows, :] + mix, g_ref[...], b_ref[...])
        of_ref[rows, :] = out
        hp_ref[rows, :] = _pack_bf16_pairs(out)
        experts, gates = _route_top2(out, rwh_ref[...], rwl_ref[...], rb_ref[...])
        for k in range(TOP_K):
            e_ref[k:k + 1, rows] = experts[k]
            gt_ref[k:k + 1, rows] = gates[k]


def _mm_res_ln_route(a, w, h, g, b, router_w, bias_col, alpha, tm=512, n_sub=2):
    m, kdim = a.shape
    n = w.shape[1]
    kern = functools.partial(_mm_res_ln_route_kernel, alpha=alpha, n_sub=n_sub)
    row = pl.BlockSpec((tm, n), lambda i: (i, 0))
    vec = pl.BlockSpec((1, n), lambda i: (0, 0))
    sel = pl.BlockSpec((TOP_K, tm), lambda i: (0, i))
    rw = jnp.pad(router_w, ((0, 0), (0, LANES - N_EXPERTS)))
    rw_hi = rw.astype(BF16)
    rw_lo = (rw - rw_hi.astype(F32)).astype(BF16)
    rw_spec = pl.BlockSpec((n, LANES), lambda i: (0, 0))
    return pl.pallas_call(
        kern,
        grid=(m // tm,),
        in_specs=[pl.BlockSpec((tm, kdim), lambda i: (i, 0)),
                  pl.BlockSpec((kdim, n), lambda i: (0, 0), pipeline_mode=pl.Buffered(1)),
                  row, vec, vec,
                  rw_spec, rw_spec, pl.BlockSpec((N_EXPERTS, 1), lambda i: (0, 0))],
        out_specs=[row, pl.BlockSpec((tm, n // 2), lambda i: (i, 0)), sel, sel],
        out_shape=[jax.ShapeDtypeStruct((m, n), F32), jax.ShapeDtypeStruct((m, n // 2), jnp.uint32),
                   jax.ShapeDtypeStruct((TOP_K, m), jnp.int32), jax.ShapeDtypeStruct((TOP_K, m), F32)],
        compiler_params=_params(("parallel",), 56),
        name="mm_res_ln_route",
    )(a, w, h, g.reshape(1, n), b.reshape(1, n), rw_hi, rw_lo, bias_col)


def _route_top2(h, w_hi, w_lo, bias_col):
    h_hi = h.astype(BF16)
    h_lo = (h - h_hi.astype(F32)).astype(BF16)
    dot = lambda a, b: jnp.dot(a, b, preferred_element_type=F32)
    logits = (dot(h_hi, w_hi) + dot(h_lo, w_hi) + dot(h_hi, w_lo)).T[:N_EXPERTS, :]
    mx = jnp.max(logits, axis=0, keepdims=True)
    ex = jnp.exp(logits - mx)
    scores = ex / jnp.sum(ex, axis=0, keepdims=True)
    sel = scores + bias_col
    tm = sel.shape[1]
    li = lax.broadcasted_iota(jnp.int32, (EPG, tm), 0)

    def first_argmax(v):
        m = jnp.max(v, axis=0, keepdims=True)
        return m, jnp.min(jnp.where(v == m, li, EPG), axis=0, keepdims=True)

    gs, i1s, i2s = [], [], []
    for g in range(N_GROUPS):
        slab = sel[g * EPG:(g + 1) * EPG, :]
        m1, i1 = first_argmax(slab)
        m2, i2 = first_argmax(jnp.where(li == i1, NEG, slab))
        gs.append(m1 + m2)
        i1s.append(i1)
        i2s.append(i2)
    best, gidx = gs[0], jnp.zeros((1, tm), jnp.int32)
    for g in range(1, N_GROUPS):
        better = gs[g] > best
        best = jnp.where(better, gs[g], best)
        gidx = jnp.where(better, g, gidx)
    l1 = jnp.zeros((1, tm), jnp.int32)
    l2 = jnp.zeros((1, tm), jnp.int32)
    s1 = jnp.zeros((1, tm), F32)
    s2 = jnp.zeros((1, tm), F32)
    for g in range(N_GROUPS):
        here = gidx == g
        slab = scores[g * EPG:(g + 1) * EPG, :]
        l1 = jnp.where(here, i1s[g], l1)
        l2 = jnp.where(here, i2s[g], l2)
        s1 = jnp.where(here, jnp.sum(jnp.where(li == i1s[g], slab, 0.0), axis=0, keepdims=True), s1)
        s2 = jnp.where(here, jnp.sum(jnp.where(li == i2s[g], slab, 0.0), axis=0, keepdims=True), s2)
    tot = s1 + s2
    return (gidx * EPG + l1, gidx * EPG + l2), (s1 / tot, s2 / tot)


N_ROW_BUFS = 2
N_SPARE_ROWS = N_ROW_BUFS * MOE_BLOCK
N_WEIGHT_SETS = 2
WEIGHT_DMA_PRIORITY = 1


def _experts_kernel(be_ref, wset_ref, ahead_ref, lead_ref, nused_ref, src_ref, dst_ref,
                    h_hbm, wg_hbm, wu_hbm, wd_hbm, ys_hbm,
                    xb0, xb1, ob0, ob1, wgl, wul, wdl, wgb, wub, wdb, gsem, ssem, wsem, *, layer, n_slots):
    i = pl.program_id(0)
    n_used = nused_ref[0]
    xbufs = (xb0, xb1)
    obufs = (ob0, ob1)

    def gather_start(blk, s):
        base = (blk + 1) * MOE_BLOCK
        for r in range(MOE_BLOCK):
            pltpu.make_async_copy(h_hbm.at[pl.ds(src_ref[base + r], 1), :],
                                  xbufs[s].at[pl.ds(r, 1), :], gsem.at[s]).start()

    def scatter_start(blk, s):
        base = (blk + 1) * MOE_BLOCK
        for r in range(MOE_BLOCK):
            pltpu.make_async_copy(obufs[s].at[pl.ds(r, 1), :],
                                  ys_hbm.at[pl.ds(dst_ref[base + r], 1), :], ssem.at[s]).start()

    def gather_wait(s):
        pltpu.make_async_copy(h_hbm.at[pl.ds(0, MOE_BLOCK), :], xbufs[s], gsem.at[s]).wait()

    def scatter_wait(s):
        pltpu.make_async_copy(obufs[s], ys_hbm.at[pl.ds(0, MOE_BLOCK), :], ssem.at[s]).wait()

    w_hbm = (wg_hbm, wu_hbm, wd_hbm)
    w_land = (wgl, wul, wdl)
    w_work = (wgb, wub, wdb)

    def weight_copy(e, p, k):
        return pltpu.make_async_copy(w_hbm[k].at[layer, e], w_land[k].at[p], wsem.at[p, k])

    @pl.when(i == 0)
    def _():
        ob1[...] = jnp.zeros_like(ob1)
        spare = pltpu.make_async_copy(ob1, ys_hbm.at[pl.ds(n_slots + MOE_BLOCK, MOE_BLOCK), :], ssem.at[1])
        spare.start()
        spare.wait()
        gather_start(0, 0)
        for p in range(N_WEIGHT_SETS):
            @pl.when(lead_ref[p] >= 0)
            def _():
                for k in range(len(w_hbm)):
                    weight_copy(lead_ref[p], p, k).start(priority=WEIGHT_DMA_PRIORITY)

    @pl.when(i < n_used)
    def _():
        for p in range(N_WEIGHT_SETS):
            @pl.when(wset_ref[i] == p)
            def _():
                ahead = ahead_ref[i]
                for k in range(len(w_hbm)):
                    weight_copy(be_ref[i], p, k).wait()
                    w_work[k][...] = w_land[k][p].astype(BF16)

                    @pl.when(ahead >= 0)
                    def _():
                        weight_copy(ahead, p, k).start(priority=WEIGHT_DMA_PRIORITY)

        for s in range(N_ROW_BUFS):
            @pl.when(i % N_ROW_BUFS == s)
            def _():
                gather_wait(s)

                @pl.when(i >= 1)
                def _():
                    scatter_wait(s)

                x = _unpack_bf16_pairs(xbufs[s][...]).astype(BF16)
                hg = jnp.dot(x, wgb[...], preferred_element_type=F32)
                hu = jnp.dot(x, wub[...], preferred_element_type=F32)
                act = (_silu(hg) * hu).astype(BF16)
                obufs[s][...] = _pack_bf16_pairs(jnp.dot(act, wdb[...], preferred_element_type=F32))
                gather_start(i + 1, 1 - s)
                scatter_start(i - 1, 1 - s)

                @pl.when(i == n_used - 1)
                def _():
                    gather_wait(1 - s)
                    scatter_wait(1 - s)
                    scatter_start(i, s)
                    scatter_wait(s)


def _experts(meta, hp, w_gate, w_up, w_down, layer, n_slots):
    d, de = w_gate.shape[2], w_gate.shape[3]
    nb = meta[0].shape[0]
    any_spec = pl.BlockSpec(memory_space=pl.ANY)
    rowbuf = pltpu.VMEM((MOE_BLOCK, hp.shape[1]), jnp.uint32)
    return pl.pallas_call(
        functools.partial(_experts_kernel, layer=layer, n_slots=n_slots),
        grid_spec=pltpu.PrefetchScalarGridSpec(
            num_scalar_prefetch=len(meta),
            grid=(nb,),
            in_specs=[any_spec, any_spec, any_spec, any_spec],
            out_specs=any_spec,
            scratch_shapes=[rowbuf, rowbuf, rowbuf, rowbuf,
                            pltpu.VMEM((N_WEIGHT_SETS, d, de), F32), pltpu.VMEM((N_WEIGHT_SETS, d, de), F32),
                            pltpu.VMEM((N_WEIGHT_SETS, de, d), F32),
                            pltpu.VMEM((d, de), BF16), pltpu.VMEM((d, de), BF16), pltpu.VMEM((de, d), BF16),
                            pltpu.SemaphoreType.DMA((N_ROW_BUFS,)), pltpu.SemaphoreType.DMA((N_ROW_BUFS,)),
                            pltpu.SemaphoreType.DMA((N_WEIGHT_SETS, 3))],
        ),
        out_shape=jax.ShapeDtypeStruct((n_slots + N_SPARE_ROWS, hp.shape[1]), jnp.uint32),
        compiler_params=_params(("arbitrary",), 56),
        name="experts",
    )(*meta, hp, w_gate, w_up, w_down)


def _combine_ln_kernel(y0_ref, y1_ref, gate_ref, h_ref, g_ref, b_ref, of_ref, *maybe_ob_ref, alpha):
    gt = gate_ref[...]
    moe = gt[:, 0:1] * _unpack_bf16_pairs(y0_ref[...]) + gt[:, 1:2] * _unpack_bf16_pairs(y1_ref[...])
    out = _layer_norm_rows(alpha * h_ref[...] + moe, g_ref[...], b_ref[...])
    of_ref[...] = out
    for ob_ref in maybe_ob_ref:
        ob_ref[...] = out.astype(BF16)


def _combine_ln(ys, gate_t, h, g, b, alpha, emit_bf16, tm=512):
    t, d = h.shape
    row = pl.BlockSpec((tm, d), lambda i: (i, 0))
    vec = pl.BlockSpec((1, d), lambda i: (0, 0))
    dp = ys.shape[1]
    out_shape = [jax.ShapeDtypeStruct((t, d), F32)] + ([jax.ShapeDtypeStruct((t, d), BF16)] if emit_bf16 else [])
    res = pl.pallas_call(
        functools.partial(_combine_ln_kernel, alpha=alpha),
        grid=(t // tm,),
        in_specs=[pl.BlockSpec((tm, dp), lambda i: (i, 0)), pl.BlockSpec((tm, dp), lambda i: (t // tm + i, 0)),
                  pl.BlockSpec((tm, TOP_K), lambda i: (i, 0)), row, vec, vec],
        out_specs=[row] * len(out_shape),
        out_shape=out_shape,
        compiler_params=_params(("parallel",), 40),
        name="combine_ln",
    )(ys, ys, gate_t, h, g.reshape(1, d), b.reshape(1, d))
    return (res[0], res[1]) if emit_bf16 else (res[0], None)


def _invert_rows_kernel(dest_ref, out_ref):
    def clear(r, carry):
        out_ref[r] = -1
        return carry

    def place(s, carry):
        out_ref[dest_ref[s]] = s
        return carry

    lax.fori_loop(0, out_ref.shape[0], clear, 0, unroll=32)
    lax.fori_loop(0, dest_ref.shape[0], place, 0, unroll=16)


def _invert_rows(dest, rows):
    smem = pl.BlockSpec(memory_space=pltpu.SMEM)
    return pl.pallas_call(
        _invert_rows_kernel,
        in_specs=[smem],
        out_specs=smem,
        out_shape=jax.ShapeDtypeStruct((rows,), jnp.int32),
        name="invert_rows",
    )(dest)


def _moe_layer(h, hp, e_idx, gate, w_gate, w_up, w_down, layer, ln_g, ln_b, alpha, emit_bf16):
    t, d = h.shape
    n_slots = t * TOP_K
    e_flat = e_idx.reshape(n_slots)
    experts = jnp.arange(N_EXPERTS, dtype=jnp.int32)
    onehot = (e_flat[:, None] == experts[None, :]).astype(jnp.int32)
    csum = jnp.cumsum(onehot, axis=0)
    counts = csum[-1]
    padded = ((counts + MOE_BLOCK - 1) // MOE_BLOCK) * MOE_BLOCK
    pad_end = jnp.cumsum(padded)
    pad_start = pad_end - padded
    dest = jnp.sum(onehot * (csum - 1 + pad_start[None, :]), axis=1).astype(jnp.int32)
    n_blocks = n_slots // MOE_BLOCK + N_EXPERTS
    rows = n_blocks * MOE_BLOCK
    row_slot = _invert_rows(dest, rows)
    row_id = jnp.arange(-MOE_BLOCK, rows, dtype=jnp.int32)
    row_slot = jnp.concatenate([jnp.full((MOE_BLOCK,), -1, jnp.int32), row_slot])
    valid = row_slot >= 0
    row_src = jnp.where(valid, row_slot % t, row_id % t)
    spare = n_slots + ((row_id // MOE_BLOCK + 1) % N_ROW_BUFS) * MOE_BLOCK + row_id % MOE_BLOCK
    row_dst = jnp.where(valid, row_slot, spare)
    blk_row0 = jnp.arange(n_blocks, dtype=jnp.int32) * MOE_BLOCK
    block_e = jnp.sum((pad_end[None, :] <= blk_row0[:, None]).astype(jnp.int32), axis=1)
    block_e = jnp.minimum(block_e, N_EXPERTS - 1)
    first = jnp.concatenate([jnp.ones((1,), bool), block_e[1:] != block_e[:-1]])
    nonempty = counts > 0
    place = jnp.cumsum(nonempty.astype(jnp.int32)) - 1
    blk_place = jnp.sum(jnp.where(block_e[:, None] == experts[None, :], place[None, :], 0), axis=1)
    wset = jnp.where(first, blk_place % N_WEIGHT_SETS, -1).astype(jnp.int32)

    def expert_at(p):
        hit = nonempty[None, :] & (place[None, :] == p.reshape(-1, 1))
        return (jnp.sum(jnp.where(hit, experts[None, :] + 1, 0), axis=1) - 1).astype(jnp.int32).reshape(p.shape)

    ahead = expert_at(blk_place + N_WEIGHT_SETS)
    lead = expert_at(jnp.arange(N_WEIGHT_SETS, dtype=jnp.int32))
    n_used = (pad_end[-1] // MOE_BLOCK).astype(jnp.int32).reshape(1)
    meta = (block_e, wset, ahead, lead, n_used, row_src.astype(jnp.int32), row_dst.astype(jnp.int32))
    ys = _experts(meta, hp, w_gate, w_up, w_down, layer, n_slots)
    return _combine_ln(ys, gate.T, h, ln_g, ln_b, alpha, emit_bf16)


def _expand_heads(x, e_bf16):
    hi = x.astype(BF16)
    r1 = x - hi.astype(F32)
    mid = r1.astype(BF16)
    lo = (r1 - mid.astype(F32)).astype(BF16)
    dot = lambda a: jnp.dot(a, e_bf16, preferred_element_type=F32)
    return dot(hi) + dot(mid) + dot(lo)


def _ssd_kernel(z_ref, xbc_ref, prev_ref, dt_ref, cw_ref, cb_ref, dtb_ref, alog_ref, dexp_ref, nw_ref, e_ref,
                o_ref, state_sc, *, d_inner):
    c = pl.program_id(1)
    L = SSM_CHUNK
    gw = d_inner // SSM_GROUPS
    hpg = gw // SSM_HEAD_DIM

    @pl.when(c == 0)
    def _():
        state_sc[...] = jnp.zeros_like(state_sc)

    has_prev = c > 0
    tile = SUBLANES
    row_in_tile = lax.broadcasted_iota(jnp.int32, (tile, 1), 0)

    def conv_silu(col0, width):
        cols = slice(col0, col0 + width)
        x = xbc_ref[0, :, cols].astype(F32)
        tail = jnp.where(has_prev, prev_ref[0, prev_ref.shape[1] - tile:, cols].astype(F32), 0.0)
        acc = cb_ref[:, cols] + x * cw_ref[SSM_CONV - 1:SSM_CONV, cols]
        for j in range(1, SSM_CONV):
            xr = pltpu.roll(x, j, 0)
            head = jnp.where(row_in_tile < j, pltpu.roll(tail, j, 0), xr[0:tile, :])
            shifted = jnp.concatenate([head, xr[tile:, :]], axis=0)
            acc = acc + shifted * cw_ref[SSM_CONV - 1 - j:SSM_CONV - j, cols]
        return _silu(acc)

    dtv = dt_ref[0] + dtb_ref[...]
    dt = jnp.maximum(dtv, 0.0) + jnp.log1p(jnp.exp(-jnp.abs(dtv)))
    a = dt * (-jnp.exp(alog_ref[...]) * LOG2E)
    ri = lax.broadcasted_iota(jnp.int32, (L, L), 0)
    ci = lax.broadcasted_iota(jnp.int32, (L, L), 1)
    causal = ri >= ci
    a_cum = jnp.dot(causal.astype(F32), a, precision=HIGHEST, preferred_element_type=F32)
    a_cum_t = a_cum.T
    e_mat = e_ref[...]
    acum_x = _expand_heads(a_cum, e_mat)
    dt_x = _expand_heads(dt, e_mat)
    lane = lax.broadcasted_iota(jnp.int32, (L, LANES), 1)
    first_head = lane < SSM_HEAD_DIM

    for g in range(SSM_GROUPS):
        cols = slice(g * gw, (g + 1) * gw)
        xg = conv_silu(g * gw, gw)
        bg = conv_silu(d_inner + g * SSM_STATE, SSM_STATE)
        cg = conv_silu(d_inner + (SSM_GROUPS + g) * SSM_STATE, SSM_STATE)
        ac = acum_x[:, cols]
        last = ac[L - 1:L, :]
        xdt = xg * dt_x[:, cols]
        cb16 = cg.astype(BF16)
        bb16 = bg.astype(BF16)
        cbm = lax.dot_general(cb16, bb16, (((1,), (1,)), ((), ())), preferred_element_type=F32)
        st = state_sc[g]
        y_off = jnp.dot(cb16, st.astype(BF16), preferred_element_type=F32) * jnp.exp2(ac)
        xd = (xdt * jnp.exp2(last - ac)).astype(BF16)
        state_sc[g] = st * jnp.exp2(last) + jnp.dot(bg.T.astype(BF16), xd, preferred_element_type=F32)
        xdt16 = xdt.astype(BF16)
        pieces = []
        for jp in range(hpg // 2):
            lhs = []
            for hh in (g * hpg + 2 * jp, g * hpg + 2 * jp + 1):
                seg = jnp.where(causal, a_cum[:, hh:hh + 1] - a_cum_t[hh:hh + 1, :], NEG)
                lhs.append((cbm * jnp.exp2(seg)).astype(BF16))
            xp = xdt16[:, jp * LANES:(jp + 1) * LANES]
            zero = jnp.zeros_like(xp)
            rhs = jnp.concatenate([jnp.where(first_head, xp, zero), jnp.where(first_head, zero, xp)], axis=0)
            pieces.append(jnp.dot(jnp.concatenate(lhs, axis=1), rhs, preferred_element_type=F32))
        y = jnp.concatenate(pieces, axis=1) + y_off + dexp_ref[:, cols] * xg
        y = y * _silu(z_ref[0, :, cols].astype(F32))
        y = y * lax.rsqrt(jnp.mean(y * y, axis=1, keepdims=True) + RMS_EPS) * nw_ref[:, cols]
        o_ref[0, :, cols] = y.astype(o_ref.dtype)


def _ssd(z3, xbc3, dt3, conv_w, conv_b, dt_bias_p, a_log_p, d_exp, norm_w, e_mat):
    b, s, d_inner = z3.shape
    cdim = xbc3.shape[2]
    nc = s // SSM_CHUNK
    gw = d_inner // SSM_GROUPS
    prev_rows = SUBLANES * (4 // xbc3.dtype.itemsize)
    rows_per_prev = SSM_CHUNK // prev_rows
    full = lambda shape: pl.BlockSpec(shape, lambda bi, c: tuple(0 for _ in shape))
    return pl.pallas_call(
        functools.partial(_ssd_kernel, d_inner=d_inner),
        grid=(b, nc),
        in_specs=[pl.BlockSpec((1, SSM_CHUNK, d_inner), lambda bi, c: (bi, c, 0)),
                  pl.BlockSpec((1, SSM_CHUNK, cdim), lambda bi, c: (bi, c, 0)),
                  pl.BlockSpec((1, prev_rows, cdim), lambda bi, c: (bi, jnp.maximum(c * rows_per_prev - 1, 0), 0)),
                  pl.BlockSpec((1, SSM_CHUNK, LANES), lambda bi, c: (bi, c, 0)),
                  full((SSM_CONV, cdim)), full((1, cdim)), full((1, LANES)), full((1, LANES)),
                  full((1, d_inner)), full((1, d_inner)), full((LANES, d_inner))],
        out_specs=pl.BlockSpec((1, SSM_CHUNK, d_inner), lambda bi, c: (bi, c, 0)),
        out_shape=jax.ShapeDtypeStruct((b, s, d_inner), BF16),
        scratch_shapes=[pltpu.VMEM((SSM_GROUPS, SSM_STATE, gw), F32)],
        compiler_params=_params(("parallel", "arbitrary"), 56),
        name="ssd",
    )(z3, xbc3, xbc3, dt3, conv_w, conv_b, dt_bias_p, a_log_p, d_exp, norm_w, e_mat)


def kernel(x, positions, attn_w_qkv, attn_w_o, ssm_w_in, ssm_conv_w, ssm_conv_b, ssm_dt_bias, ssm_A_log, ssm_D, ssm_norm_w, ssm_w_out, router_w, router_bias, moe_w_gate, moe_w_up, moe_w_down, ln_mix_g, ln_mix_b, ln_ffn_g, ln_ffn_b):
    bsz, seq, d_model = x.shape
    t = bsz * seq
    depth = moe_w_gate.shape[0]
    alpha = (2 * depth) ** 0.25
    n_heads = d_model // HEAD_DIM
    patterns = ((128, 1), (512, 4), (2048, 16))
    attn_blk = 512

    half = ROT_DIM // 2
    inv_freq = ROPE_THETA ** (-jnp.arange(0, ROT_DIM, 2, dtype=F32) / ROT_DIM)
    ang = positions.astype(F32).reshape(t, 1) * inv_freq
    cos, sin = jnp.cos(ang), jnp.sin(ang)
    zeros_r = jnp.zeros((t, LANES - ROT_DIM), F32)
    cos_t = jnp.concatenate([cos, cos, jnp.ones((t, LANES - ROT_DIM), F32)], axis=1)
    sa_t = jnp.concatenate([-sin, jnp.zeros((t, half), F32), zeros_r], axis=1)
    sb_t = jnp.concatenate([jnp.zeros((t, half), F32), sin, zeros_r], axis=1)
    bias = _branch_count_bias(seq, attn_blk, patterns)

    bias_col = router_bias.reshape(N_EXPERTS, 1).astype(F32)

    h = x.reshape(t, d_model)
    hb = h.astype(BF16)
    for i in range(depth):
        j = i // 2
        if i % 2 == 0:
            qkv = _qkv_rope(h, attn_w_qkv[j].astype(BF16), cos_t, sa_t, sb_t, d_model)
            att = _attention(qkv.reshape(bsz, seq, 3 * d_model), bias, n_heads, attn_blk)
            mix, w_mix = att.reshape(t, d_model), attn_w_o[j]
        else:
            d_inner = ssm_norm_w.shape[1]
            n_ssm_heads = ssm_dt_bias.shape[1]
            cdim = ssm_conv_w.shape[2]
            w_in = ssm_w_in[j].astype(BF16)
            z = _mm(hb, w_in, 0, d_inner, BF16, tn=2048)
            xbc = _mm(hb, w_in, d_inner, cdim, BF16, tn=2048)
            w_dt = jnp.pad(w_in[:, d_inner + cdim:], ((0, 0), (0, LANES - n_ssm_heads)))
            dt = _mm(hb, w_dt, 0, LANES, F32)
            hpad = (0, LANES - n_ssm_heads)
            head_of_ch = jnp.arange(d_inner, dtype=jnp.int32) // SSM_HEAD_DIM
            e_mat = (jnp.arange(LANES, dtype=jnp.int32)[:, None] == head_of_ch[None, :]).astype(BF16)
            y = _ssd(z.reshape(bsz, seq, d_inner), xbc.reshape(bsz, seq, cdim), dt.reshape(bsz, seq, LANES),
                     ssm_conv_w[j], ssm_conv_b[j].reshape(1, cdim),
                     jnp.pad(ssm_dt_bias[j], hpad).reshape(1, LANES), jnp.pad(ssm_A_log[j], hpad).reshape(1, LANES),
                     jnp.repeat(ssm_D[j], SSM_HEAD_DIM).reshape(1, d_inner), ssm_norm_w[j].reshape(1, d_inner), e_mat)
            mix, w_mix = y.reshape(t, d_inner), ssm_w_out[j]
        h, hp, e_idx, gate = _mm_res_ln_route(mix, w_mix.astype(BF16), h, ln_mix_g[i], ln_mix_b[i],
                                              router_w, bias_col, alpha)
        next_is_ssm = i + 1 < depth and (i + 1) % 2 == 1
        h, hb = _moe_layer(h, hp, e_idx, gate, moe_w_gate, moe_w_up, moe_w_down, i,
                           ln_ffn_g[i], ln_ffn_b[i], alpha, next_is_ssm)
    return h.reshape(bsz, seq, d_model)
```

```python
import functools
import math

import jax
import jax.numpy as jnp
from jax import lax
from jax.experimental import pallas as pl
from jax.experimental.pallas import tpu as pltpu

F32 = jnp.float32
BF16 = jnp.bfloat16
HIGHEST = lax.Precision.HIGHEST

LANES = 128
SUBLANES = 8
MXU_COLS = 256
MIB = 1024 * 1024

HEAD_DIM = 128
ROT_DIM = HEAD_DIM // 4
ROPE_THETA = 500000.0
N_EXPERTS = 64
N_GROUPS = 8
EPG = N_EXPERTS // N_GROUPS
TOP_K = 2
MOE_BLOCK = 256
SSM_HEAD_DIM = 64
SSM_GROUPS = 8
SSM_STATE = 128
SSM_CONV = 4
SSM_CHUNK = 128
LN_EPS = 1e-5
RMS_EPS = 1e-5
NEG = -1e30
LOG2E = math.log2(math.e)


def _params(sem, vmem_mib):
    return pltpu.CompilerParams(dimension_semantics=sem, vmem_limit_bytes=vmem_mib * MIB)


def _silu(x):
    hx = 0.5 * x
    return hx + hx * jnp.tanh(hx)


def _layer_norm_rows(y, g, b):
    mu = jnp.mean(y, axis=-1, keepdims=True)
    yc = y - mu
    var = jnp.mean(yc * yc, axis=-1, keepdims=True)
    return yc * lax.rsqrt(var + LN_EPS) * g + b


def _mm_kernel(a_ref, w_ref, o_ref):
    o_ref[...] = jnp.dot(a_ref[...], w_ref[...], preferred_element_type=F32).astype(o_ref.dtype)


def _mm(a, w, col0, n, out_dtype, tm=1024, tn=512):
    m, k = a.shape
    tn = min(tn, n)
    assert col0 % tn == 0 and n % tn == 0 and m % tm == 0
    c0 = col0 // tn
    return pl.pallas_call(
        _mm_kernel,
        grid=(m // tm, n // tn),
        in_specs=[pl.BlockSpec((tm, k), lambda i, j: (i, 0)),
                  pl.BlockSpec((k, tn), lambda i, j: (0, c0 + j))],
        out_specs=pl.BlockSpec((tm, tn), lambda i, j: (i, j)),
        out_shape=jax.ShapeDtypeStruct((m, n), out_dtype),
        compiler_params=_params(("parallel", "arbitrary"), 48),
        name="mm",
    )(a, w)


def _qkv_rope_kernel(x32_ref, w_ref, cos_ref, sa_ref, sb_ref, o_ref, x_ref, *, n_rope_tiles, n_q_tiles, scale):
    n = pl.program_id(1)
    tn = w_ref.shape[1]

    @pl.when(n == 0)
    def _():
        x_ref[...] = x32_ref[...].astype(BF16)

    @pl.when(n < n_rope_tiles)
    def _():
        c = cos_ref[...]
        sa = sa_ref[...]
        sb = sb_ref[...]
        sc = jnp.where(n < n_q_tiles, scale, 1.0).astype(F32)
        for p in range(tn // MXU_COLS):
            t2 = jnp.dot(x_ref[...], w_ref[:, p * MXU_COLS:(p + 1) * MXU_COLS], preferred_element_type=F32)
            for j in range(MXU_COLS // LANES):
                t = t2[:, j * LANES:(j + 1) * LANES]
                r = t * c + pltpu.roll(t, LANES - ROT_DIM // 2, 1) * sa + pltpu.roll(t, ROT_DIM // 2, 1) * sb
                col = p * MXU_COLS + j * LANES
                o_ref[:, col:col + LANES] = (r * sc).astype(o_ref.dtype)

    @pl.when(n >= n_rope_tiles)
    def _():
        o_ref[...] = jnp.dot(x_ref[...], w_ref[...], preferred_element_type=F32).astype(o_ref.dtype)


def _qkv_rope(x, w, cos_t, sa_t, sb_t, d_model, tm=1024, tn=1024):
    m, k = x.shape
    n = w.shape[1]
    kern = functools.partial(_qkv_rope_kernel, n_rope_tiles=2 * d_model // tn, n_q_tiles=d_model // tn,
                             scale=HEAD_DIM ** -0.5 * LOG2E)
    tab = pl.BlockSpec((tm, LANES), lambda i, j: (i, 0))
    return pl.pallas_call(
        kern,
        grid=(m // tm, n // tn),
        in_specs=[pl.BlockSpec((tm, k), lambda i, j: (i, 0)),
                  pl.BlockSpec((k, tn), lambda i, j: (0, j)),
                  tab, tab, tab],
        out_specs=pl.BlockSpec((tm, tn), lambda i, j: (i, j)),
        out_shape=jax.ShapeDtypeStruct((m, n), BF16),
        scratch_shapes=[pltpu.VMEM((tm, k), BF16)],
        compiler_params=_params(("parallel", "arbitrary"), 56),
        name="qkv_rope",
    )(x, w, cos_t, sa_t, sb_t)


def _attn_kernel(qi_ref, ki_ref, q_ref, k_ref, v_ref, bias_ref, o_ref, m_sc, l_sc, acc_sc, *, heads):
    p = pl.program_id(2)
    qi = qi_ref[p]
    ki = ki_ref[p]
    blk = q_ref.shape[1]

    @pl.when(ki == 0)
    def _():
        m_sc[...] = jnp.full_like(m_sc, 0.1 * NEG)
        l_sc[...] = jnp.zeros_like(l_sc)
        acc_sc[...] = jnp.zeros_like(acc_sc)

    bias = bias_ref[qi - ki]
    ones_tile = (lax.broadcasted_iota(jnp.int32, (blk, LANES), 1) == 0).astype(BF16)
    for j in range(heads):
        cols = slice(j * HEAD_DIM, (j + 1) * HEAD_DIM)
        s = lax.dot_general(q_ref[0, :, cols], k_ref[0, :, cols], (((1,), (1,)), ((), ())),
                            preferred_element_type=F32) + bias
        m_prev = m_sc[j]
        m_new = jnp.maximum(m_prev, jnp.max(s, axis=1, keepdims=True))
        alpha = jnp.exp2(m_prev - m_new)
        pr = jnp.exp2(s - jnp.concatenate([m_new] * (blk // LANES), axis=1))
        pv = jnp.dot(pr.astype(BF16), jnp.concatenate([v_ref[0, :, cols], ones_tile], axis=1),
                     preferred_element_type=F32)
        acc_sc[j] = alpha * acc_sc[j] + pv[:, :HEAD_DIM]
        l_sc[j] = alpha * l_sc[j] + pv[:, HEAD_DIM:]
        m_sc[j] = m_new

    @pl.when(ki == qi)
    def _():
        for j in range(heads):
            o_ref[0, :, j * HEAD_DIM:(j + 1) * HEAD_DIM] = (acc_sc[j] / l_sc[j][:, 0:1]).astype(o_ref.dtype)


def _attention(qkv3, bias, n_heads, blk, heads=16):
    b, s, _ = qkv3.shape
    nq = s // blk
    pairs = [(qi, ki) for qi in range(nq) for ki in range(qi + 1)]
    qi_tab = jnp.asarray([pq for pq, _ in pairs], jnp.int32)
    ki_tab = jnp.asarray([pk for _, pk in pairs], jnp.int32)
    hg = n_heads // heads
    w = heads * HEAD_DIM
    return pl.pallas_call(
        functools.partial(_attn_kernel, heads=heads),
        grid_spec=pltpu.PrefetchScalarGridSpec(
            num_scalar_prefetch=2,
            grid=(b, hg, len(pairs)),
            in_specs=[pl.BlockSpec((1, blk, w), lambda bi, h, p, qt, kt: (bi, qt[p], h)),
                      pl.BlockSpec((1, blk, w), lambda bi, h, p, qt, kt: (bi, kt[p], hg + h)),
                      pl.BlockSpec((1, blk, w), lambda bi, h, p, qt, kt: (bi, kt[p], 2 * hg + h)),
                      pl.BlockSpec((nq, blk, blk), lambda bi, h, p, qt, kt: (0, 0, 0))],
            out_specs=pl.BlockSpec((1, blk, w), lambda bi, h, p, qt, kt: (bi, qt[p], h)),
            scratch_shapes=[pltpu.VMEM((heads, blk, LANES), F32), pltpu.VMEM((heads, blk, LANES), F32),
                            pltpu.VMEM((heads, blk, HEAD_DIM), F32)],
        ),
        out_shape=jax.ShapeDtypeStruct((b, s, n_heads * HEAD_DIM), BF16),
        compiler_params=_params(("parallel", "parallel", "arbitrary"), 48),
        name="attention",
    )(qi_tab, ki_tab, qkv3, qkv3, qkv3, bias)


def _branch_count_bias(s, blk, patterns):
    nq = s // blk
    off = jnp.arange(nq, dtype=jnp.int32)[:, None, None] * blk
    d = off + jnp.arange(blk, dtype=jnp.int32)[None, :, None] - jnp.arange(blk, dtype=jnp.int32)[None, None, :]
    cnt = jnp.zeros(d.shape, F32)
    for window, dilation in patterns:
        cnt = cnt + ((d >= 0) & (d % dilation == 0) & (d <= window)).astype(F32)
    return jnp.where(cnt > 0, jnp.log2(jnp.maximum(cnt, 1.0)), NEG)


def _pack_bf16_pairs(x):
    n = x.shape[1] // 2
    lo = lax.bitcast_convert_type(x[:, :n].astype(BF16).astype(F32), jnp.uint32)
    hi = lax.bitcast_convert_type(x[:, n:].astype(BF16).astype(F32), jnp.uint32)
    return (lo >> 16) | hi


def _unpack_bf16_pairs(w):
    lo = lax.bitcast_convert_type(w << 16, F32)
    hi = lax.bitcast_convert_type(w & jnp.uint32(0xFFFF0000), F32)
    return jnp.concatenate([lo, hi], axis=1)


def _mm_res_ln_route_kernel(a_ref, w_ref, h_ref, g_ref, b_ref, rwh_ref, rwl_ref, rb_ref,
                            of_ref, hp_ref, e_ref, gt_ref, *, alpha, n_sub):
    sub = a_ref.shape[0] // n_sub
    for r in range(n_sub):
        rows = slice(r * sub, (r + 1) * sub)
        mix = jnp.dot(a_ref[rows, :], w_ref[...], preferred_element_type=F32)
        out = _layer_norm_rows(alpha * h_ref[rows, :] + mix, g_ref[...], b_ref[...])
        of_ref[rows, :] = out
        hp_ref[rows, :] = _pack_bf16_pairs(out)
        experts, gates = _route_top2(out, rwh_ref[...], rwl_ref[...], rb_ref[...])
        for k in range(TOP_K):
            e_ref[k:k + 1, rows] = experts[k]
            gt_ref[k:k + 1, rows] = gates[k]


def _mm_res_ln_route(a, w, h, g, b, router_w, bias_col, alpha, tm=512, n_sub=2):
    m, kdim = a.shape
    n = w.shape[1]
    kern = functools.partial(_mm_res_ln_route_kernel, alpha=alpha, n_sub=n_sub)
    row = pl.BlockSpec((tm, n), lambda i: (i, 0))
    vec = pl.BlockSpec((1, n), lambda i: (0, 0))
    sel = pl.BlockSpec((TOP_K, tm), lambda i: (0, i))
    rw = jnp.pad(router_w, ((0, 0), (0, LANES - N_EXPERTS)))
    rw_hi = rw.astype(BF16)
    rw_lo = (rw - rw_hi.astype(F32)).astype(BF16)
    rw_spec = pl.BlockSpec((n, LANES), lambda i: (0, 0))
    return pl.pallas_call(
        kern,
        grid=(m // tm,),
        in_specs=[pl.BlockSpec((tm, kdim), lambda i: (i, 0)),
                  pl.BlockSpec((kdim, n), lambda i: (0, 0), pipeline_mode=pl.Buffered(1)),
                  row, vec, vec,
                  rw_spec, rw_spec, pl.BlockSpec((N_EXPERTS, 1), lambda i: (0, 0))],
        out_specs=[row, pl.BlockSpec((tm, n // 2), lambda i: (i, 0)), sel, sel],
        out_shape=[jax.ShapeDtypeStruct((m, n), F32), jax.ShapeDtypeStruct((m, n // 2), jnp.uint32),
                   jax.ShapeDtypeStruct((TOP_K, m), jnp.int32), jax.ShapeDtypeStruct((TOP_K, m), F32)],
        compiler_params=_params(("parallel",), 56),
        name="mm_res_ln_route",
    )(a, w, h, g.reshape(1, n), b.reshape(1, n), rw_hi, rw_lo, bias_col)


def _route_top2(h, w_hi, w_lo, bias_col):
    h_hi = h.astype(BF16)
    h_lo = (h - h_hi.astype(F32)).astype(BF16)
    dot = lambda a, b: jnp.dot(a, b, preferred_element_type=F32)
    logits = (dot(h_hi, w_hi) + dot(h_lo, w_hi) + dot(h_hi, w_lo)).T[:N_EXPERTS, :]
    mx = jnp.max(logits, axis=0, keepdims=True)
    ex = jnp.exp(logits - mx)
    scores = ex / jnp.sum(ex, axis=0, keepdims=True)
    sel = scores + bias_col
    tm = sel.shape[1]
    li = lax.broadcasted_iota(jnp.int32, (EPG, tm), 0)

    def first_argmax(v):
        m = jnp.max(v, axis=0, keepdims=True)
        return m, jnp.min(jnp.where(v == m, li, EPG), axis=0, keepdims=True)

    gs, i1s, i2s = [], [], []
    for g in range(N_GROUPS):
        slab = sel[g * EPG:(g + 1) * EPG, :]
        m1, i1 = first_argmax(slab)
        m2, i2 = first_argmax(jnp.where(li == i1, NEG, slab))
        gs.append(m1 + m2)
        i1s.append(i1)
        i2s.append(i2)
    best, gidx = gs[0], jnp.zeros((1, tm), jnp.int32)
    for g in range(1, N_GROUPS):
        better = gs[g] > best
        best = jnp.where(better, gs[g], best)
        gidx = jnp.where(better, g, gidx)
    l1 = jnp.zeros((1, tm), jnp.int32)
    l2 = jnp.zeros((1, tm), jnp.int32)
    s1 = jnp.zeros((1, tm), F32)
    s2 = jnp.zeros((1, tm), F32)
    for g in range(N_GROUPS):
        here = gidx == g
        slab = scores[g * EPG:(g + 1) * EPG, :]
        l1 = jnp.where(here, i1s[g], l1)
        l2 = jnp.where(here, i2s[g], l2)
        s1 = jnp.where(here, jnp.sum(jnp.where(li == i1s[g], slab, 0.0), axis=0, keepdims=True), s1)
        s2 = jnp.where(here, jnp.sum(jnp.where(li == i2s[g], slab, 0.0), axis=0, keepdims=True), s2)
    tot = s1 + s2
    return (gidx * EPG + l1, gidx * EPG + l2), (s1 / tot, s2 / tot)


N_ROW_BUFS = 2
N_SPARE_ROWS = N_ROW_BUFS * MOE_BLOCK
N_WEIGHT_SETS = 2
WEIGHT_DMA_PRIORITY = 1


def _experts_kernel(be_ref, wset_ref, ahead_ref, lead_ref, nused_ref, src_ref, dst_ref,
                    h_hbm, wg_hbm, wu_hbm, wd_hbm, ys_hbm,
                    xb0, xb1, ob0, ob1, wgl, wul, wdl, wgb, wub, wdb, gsem, ssem, wsem, *, layer, n_slots):
    i = pl.program_id(0)
    n_used = nused_ref[0]
    xbufs = (xb0, xb1)
    obufs = (ob0, ob1)

    def gather_start(blk, s):
        base = (blk + 1) * MOE_BLOCK
        for r in range(MOE_BLOCK):
            pltpu.make_async_copy(h_hbm.at[pl.ds(src_ref[base + r], 1), :],
                                  xbufs[s].at[pl.ds(r, 1), :], gsem.at[s]).start()

    def scatter_start(blk, s):
        base = (blk + 1) * MOE_BLOCK
        for r in range(MOE_BLOCK):
            pltpu.make_async_copy(obufs[s].at[pl.ds(r, 1), :],
                                  ys_hbm.at[pl.ds(dst_ref[base + r], 1), :], ssem.at[s]).start()

    def gather_wait(s):
        pltpu.make_async_copy(h_hbm.at[pl.ds(0, MOE_BLOCK), :], xbufs[s], gsem.at[s]).wait()

    def scatter_wait(s):
        pltpu.make_async_copy(obufs[s], ys_hbm.at[pl.ds(0, MOE_BLOCK), :], ssem.at[s]).wait()

    w_hbm = (wg_hbm, wu_hbm, wd_hbm)
    w_land = (wgl, wul, wdl)
    w_work = (wgb, wub, wdb)

    def weight_copy(e, p, k):
        return pltpu.make_async_copy(w_hbm[k].at[layer, e], w_land[k].at[p], wsem.at[p, k])

    @pl.when(i == 0)
    def _():
        ob1[...] = jnp.zeros_like(ob1)
        spare = pltpu.make_async_copy(ob1, ys_hbm.at[pl.ds(n_slots + MOE_BLOCK, MOE_BLOCK), :], ssem.at[1])
        spare.start()
        spare.wait()
        gather_start(0, 0)
        for p in range(N_WEIGHT_SETS):
            @pl.when(lead_ref[p] >= 0)
            def _():
                for k in range(len(w_hbm)):
                    weight_copy(lead_ref[p], p, k).start(priority=WEIGHT_DMA_PRIORITY)

    @pl.when(i < n_used)
    def _():
        for p in range(N_WEIGHT_SETS):
            @pl.when(wset_ref[i] == p)
            def _():
                ahead = ahead_ref[i]
                for k in range(len(w_hbm)):
                    weight_copy(be_ref[i], p, k).wait()
                    w_work[k][...] = w_land[k][p].astype(BF16)

                    @pl.when(ahead >= 0)
                    def _():
                        weight_copy(ahead, p, k).start(priority=WEIGHT_DMA_PRIORITY)

        for s in range(N_ROW_BUFS):
            @pl.when(i % N_ROW_BUFS == s)
            def _():
                gather_wait(s)

                @pl.when(i >= 1)
                def _():
                    scatter_wait(s)

                x = _unpack_bf16_pairs(xbufs[s][...]).astype(BF16)
                hg = jnp.dot(x, wgb[...], preferred_element_type=F32)
                hu = jnp.dot(x, wub[...], preferred_element_type=F32)
                act = (_silu(hg) * hu).astype(BF16)
                obufs[s][...] = _pack_bf16_pairs(jnp.dot(act, wdb[...], preferred_element_type=F32))
                gather_start(i + 1, 1 - s)
                scatter_start(i - 1, 1 - s)

                @pl.when(i == n_used - 1)
                def _():
                    gather_wait(1 - s)
                    scatter_wait(1 - s)
                    scatter_start(i, s)
                    scatter_wait(s)


def _experts(meta, hp, w_gate, w_up, w_down, layer, n_slots):
    d, de = w_gate.shape[2], w_gate.shape[3]
    nb = meta[0].shape[0]
    any_spec = pl.BlockSpec(memory_space=pl.ANY)
    rowbuf = pltpu.VMEM((MOE_BLOCK, hp.shape[1]), jnp.uint32)
    return pl.pallas_call(
        functools.partial(_experts_kernel, layer=layer, n_slots=n_slots),
        grid_spec=pltpu.PrefetchScalarGridSpec(
            num_scalar_prefetch=len(meta),
            grid=(nb,),
            in_specs=[any_spec, any_spec, any_spec, any_spec],
            out_specs=any_spec,
            scratch_shapes=[rowbuf, rowbuf, rowbuf, rowbuf,
                            pltpu.VMEM((N_WEIGHT_SETS, d, de), F32), pltpu.VMEM((N_WEIGHT_SETS, d, de), F32),
                            pltpu.VMEM((N_WEIGHT_SETS, de, d), F32),
                            pltpu.VMEM((d, de), BF16), pltpu.VMEM((d, de), BF16), pltpu.VMEM((de, d), BF16),
                            pltpu.SemaphoreType.DMA((N_ROW_BUFS,)), pltpu.SemaphoreType.DMA((N_ROW_BUFS,)),
                            pltpu.SemaphoreType.DMA((N_WEIGHT_SETS, 3))],
        ),
        out_shape=jax.ShapeDtypeStruct((n_slots + N_SPARE_ROWS, hp.shape[1]), jnp.uint32),
        compiler_params=_params(("arbitrary",), 56),
        name="experts",
    )(*meta, hp, w_gate, w_up, w_down)


def _combine_ln_kernel(y0_ref, y1_ref, gate_ref, h_ref, g_ref, b_ref, of_ref, *maybe_ob_ref, alpha):
    gt = gate_ref[...]
    moe = gt[:, 0:1] * _unpack_bf16_pairs(y0_ref[...]) + gt[:, 1:2] * _unpack_bf16_pairs(y1_ref[...])
    out = _layer_norm_rows(alpha * h_ref[...] + moe, g_ref[...], b_ref[...])
    of_ref[...] = out
    for ob_ref in maybe_ob_ref:
        ob_ref[...] = out.astype(BF16)


def _combine_ln(ys, gate_t, h, g, b, alpha, emit_bf16, tm=512):
    t, d = h.shape
    row = pl.BlockSpec((tm, d), lambda i: (i, 0))
    vec = pl.BlockSpec((1, d), lambda i: (0, 0))
    dp = ys.shape[1]
    out_shape = [jax.ShapeDtypeStruct((t, d), F32)] + ([jax.ShapeDtypeStruct((t, d), BF16)] if emit_bf16 else [])
    res = pl.pallas_call(
        functools.partial(_combine_ln_kernel, alpha=alpha),
        grid=(t // tm,),
        in_specs=[pl.BlockSpec((tm, dp), lambda i: (i, 0)), pl.BlockSpec((tm, dp), lambda i: (t // tm + i, 0)),
                  pl.BlockSpec((tm, TOP_K), lambda i: (i, 0)), row, vec, vec],
        out_specs=[row] * len(out_shape),
        out_shape=out_shape,
        compiler_params=_params(("parallel",), 40),
        name="combine_ln",
    )(ys, ys, gate_t, h, g.reshape(1, d), b.reshape(1, d))
    return (res[0], res[1]) if emit_bf16 else (res[0], None)


def _invert_rows_kernel(dest_ref, fill_hbm, out_ref, sem):
    fill = pltpu.make_async_copy(fill_hbm, out_ref, sem)
    fill.start()
    fill.wait()

    def place(s, carry):
        out_ref[dest_ref[s]] = s
        return carry

    lax.fori_loop(0, dest_ref.shape[0], place, 0, unroll=16)


def _invert_rows(dest, rows):
    smem = pl.BlockSpec(memory_space=pltpu.SMEM)
    return pl.pallas_call(
        _invert_rows_kernel,
        in_specs=[smem, pl.BlockSpec(memory_space=pl.ANY)],
        out_specs=smem,
        out_shape=jax.ShapeDtypeStruct((rows,), jnp.int32),
        scratch_shapes=[pltpu.SemaphoreType.DMA(())],
        name="invert_rows",
    )(dest, jnp.full((rows,), -1, jnp.int32))


def _moe_layer(h, hp, e_idx, gate, w_gate, w_up, w_down, layer, ln_g, ln_b, alpha, emit_bf16):
    t, d = h.shape
    n_slots = t * TOP_K
    e_flat = e_idx.reshape(n_slots)
    experts = jnp.arange(N_EXPERTS, dtype=jnp.int32)
    onehot = (e_flat[:, None] == experts[None, :]).astype(jnp.int32)
    csum = jnp.cumsum(onehot, axis=0)
    counts = csum[-1]
    padded = ((counts + MOE_BLOCK - 1) // MOE_BLOCK) * MOE_BLOCK
    pad_end = jnp.cumsum(padded)
    pad_start = pad_end - padded
    dest = jnp.sum(onehot * (csum - 1 + pad_start[None, :]), axis=1).astype(jnp.int32)
    n_blocks = n_slots // MOE_BLOCK + N_EXPERTS
    rows = n_blocks * MOE_BLOCK
    row_slot = _invert_rows(dest, rows)
    row_id = jnp.arange(-MOE_BLOCK, rows, dtype=jnp.int32)
    row_slot = jnp.concatenate([jnp.full((MOE_BLOCK,), -1, jnp.int32), row_slot])
    valid = row_slot >= 0
    row_src = jnp.where(valid, row_slot % t, row_id % t)
    spare = n_slots + ((row_id // MOE_BLOCK + 1) % N_ROW_BUFS) * MOE_BLOCK + row_id % MOE_BLOCK
    row_dst = jnp.where(valid, row_slot, spare)
    blk_row0 = jnp.arange(n_blocks, dtype=jnp.int32) * MOE_BLOCK
    block_e = jnp.sum((pad_end[None, :] <= blk_row0[:, None]).astype(jnp.int32), axis=1)
    block_e = jnp.minimum(block_e, N_EXPERTS - 1)
    first = jnp.concatenate([jnp.ones((1,), bool), block_e[1:] != block_e[:-1]])
    nonempty = counts > 0
    place = jnp.cumsum(nonempty.astype(jnp.int32)) - 1
    blk_place = jnp.sum(jnp.where(block_e[:, None] == experts[None, :], place[None, :], 0), axis=1)
    wset = jnp.where(first, blk_place % N_WEIGHT_SETS, -1).astype(jnp.int32)

    def expert_at(p):
        hit = nonempty[None, :] & (place[None, :] == p.reshape(-1, 1))
        return (jnp.sum(jnp.where(hit, experts[None, :] + 1, 0), axis=1) - 1).astype(jnp.int32).reshape(p.shape)

    ahead = expert_at(blk_place + N_WEIGHT_SETS)
    lead = expert_at(jnp.arange(N_WEIGHT_SETS, dtype=jnp.int32))
    n_used = (pad_end[-1] // MOE_BLOCK).astype(jnp.int32).reshape(1)
    meta = (block_e, wset, ahead, lead, n_used, row_src.astype(jnp.int32), row_dst.astype(jnp.int32))
    ys = _experts(meta, hp, w_gate, w_up, w_down, layer, n_slots)
    return _combine_ln(ys, gate.T, h, ln_g, ln_b, alpha, emit_bf16)


def _expand_heads(x, e_bf16):
    hi = x.astype(BF16)
    r1 = x - hi.astype(F32)
    mid = r1.astype(BF16)
    lo = (r1 - mid.astype(F32)).astype(BF16)
    dot = lambda a: jnp.dot(a, e_bf16, preferred_element_type=F32)
    return dot(hi) + dot(mid) + dot(lo)


def _ssd_kernel(z_ref, xbc_ref, prev_ref, dt_ref, cw_ref, cb_ref, dtb_ref, alog_ref, dexp_ref, nw_ref, e_ref,
                o_ref, state_sc, *, d_inner):
    c = pl.program_id(1)
    L = SSM_CHUNK
    gw = d_inner // SSM_GROUPS
    hpg = gw // SSM_HEAD_DIM

    @pl.when(c == 0)
    def _():
        state_sc[...] = jnp.zeros_like(state_sc)

    has_prev = c > 0
    tile = SUBLANES
    row_in_tile = lax.broadcasted_iota(jnp.int32, (tile, 1), 0)

    def conv_silu(col0, width):
        cols = slice(col0, col0 + width)
        x = xbc_ref[0, :, cols].astype(F32)
        tail = jnp.where(has_prev, prev_ref[0, prev_ref.shape[1] - tile:, cols].astype(F32), 0.0)
        acc = cb_ref[:, cols] + x * cw_ref[SSM_CONV - 1:SSM_CONV, cols]
        for j in range(1, SSM_CONV):
            xr = pltpu.roll(x, j, 0)
            head = jnp.where(row_in_tile < j, pltpu.roll(tail, j, 0), xr[0:tile, :])
            shifted = jnp.concatenate([head, xr[tile:, :]], axis=0)
            acc = acc + shifted * cw_ref[SSM_CONV - 1 - j:SSM_CONV - j, cols]
        return _silu(acc)

    dtv = dt_ref[0] + dtb_ref[...]
    dt = jnp.maximum(dtv, 0.0) + jnp.log1p(jnp.exp(-jnp.abs(dtv)))
    a = dt * (-jnp.exp(alog_ref[...]) * LOG2E)
    ri = lax.broadcasted_iota(jnp.int32, (L, L), 0)
    ci = lax.broadcasted_iota(jnp.int32, (L, L), 1)
    causal = ri >= ci
    a_cum = jnp.dot(causal.astype(F32), a, precision=HIGHEST, preferred_element_type=F32)
    a_cum_t = a_cum.T
    e_mat = e_ref[...]
    acum_x = _expand_heads(a_cum, e_mat)
    dt_x = _expand_heads(dt, e_mat)
    lane = lax.broadcasted_iota(jnp.int32, (L, LANES), 1)
    first_head = lane < SSM_HEAD_DIM

    for g in range(SSM_GROUPS):
        cols = slice(g * gw, (g + 1) * gw)
        xg = conv_silu(g * gw, gw)
        bg = conv_silu(d_inner + g * SSM_STATE, SSM_STATE)
        cg = conv_silu(d_inner + (SSM_GROUPS + g) * SSM_STATE, SSM_STATE)
        ac = acum_x[:, cols]
        last = ac[L - 1:L, :]
        xdt = xg * dt_x[:, cols]
        cb16 = cg.astype(BF16)
        bb16 = bg.astype(BF16)
        cbm = lax.dot_general(cb16, bb16, (((1,), (1,)), ((), ())), preferred_element_type=F32)
        st = state_sc[g]
        y_off = jnp.dot(cb16, st.astype(BF16), preferred_element_type=F32) * jnp.exp2(ac)
        xd = (xdt * jnp.exp2(last - ac)).astype(BF16)
        state_sc[g] = st * jnp.exp2(last) + jnp.dot(bg.T.astype(BF16), xd, preferred_element_type=F32)
        xdt16 = xdt.astype(BF16)
        pieces = []
        for jp in range(hpg // 2):
            lhs = []
            for hh in (g * hpg + 2 * jp, g * hpg + 2 * jp + 1):
                seg = jnp.where(causal, a_cum[:, hh:hh + 1] - a_cum_t[hh:hh + 1, :], NEG)
                lhs.append((cbm * jnp.exp2(seg)).astype(BF16))
            xp = xdt16[:, jp * LANES:(jp + 1) * LANES]
            zero = jnp.zeros_like(xp)
            rhs = jnp.concatenate([jnp.where(first_head, xp, zero), jnp.where(first_head, zero, xp)], axis=0)
            pieces.append(jnp.dot(jnp.concatenate(lhs, axis=1), rhs, preferred_element_type=F32))
        y = jnp.concatenate(pieces, axis=1) + y_off + dexp_ref[:, cols] * xg
        y = y * _silu(z_ref[0, :, cols].astype(F32))
        y = y * lax.rsqrt(jnp.mean(y * y, axis=1, keepdims=True) + RMS_EPS) * nw_ref[:, cols]
        o_ref[0, :, cols] = y.astype(o_ref.dtype)


def _ssd(z3, xbc3, dt3, conv_w, conv_b, dt_bias_p, a_log_p, d_exp, norm_w, e_mat):
    b, s, d_inner = z3.shape
    cdim = xbc3.shape[2]
    nc = s // SSM_CHUNK
    gw = d_inner // SSM_GROUPS
    prev_rows = SUBLANES * (4 // xbc3.dtype.itemsize)
    rows_per_prev = SSM_CHUNK // prev_rows
    full = lambda shape: pl.BlockSpec(shape, lambda bi, c: tuple(0 for _ in shape))
    return pl.pallas_call(
        functools.partial(_ssd_kernel, d_inner=d_inner),
        grid=(b, nc),
        in_specs=[pl.BlockSpec((1, SSM_CHUNK, d_inner), lambda bi, c: (bi, c, 0)),
                  pl.BlockSpec((1, SSM_CHUNK, cdim), lambda bi, c: (bi, c, 0)),
                  pl.BlockSpec((1, prev_rows, cdim), lambda bi, c: (bi, jnp.maximum(c * rows_per_prev - 1, 0), 0)),
                  pl.BlockSpec((1, SSM_CHUNK, LANES), lambda bi, c: (bi, c, 0)),
                  full((SSM_CONV, cdim)), full((1, cdim)), full((1, LANES)), full((1, LANES)),
                  full((1, d_inner)), full((1, d_inner)), full((LANES, d_inner))],
        out_specs=pl.BlockSpec((1, SSM_CHUNK, d_inner), lambda bi, c: (bi, c, 0)),
        out_shape=jax.ShapeDtypeStruct((b, s, d_inner), BF16),
        scratch_shapes=[pltpu.VMEM((SSM_GROUPS, SSM_STATE, gw), F32)],
        compiler_params=_params(("parallel", "arbitrary"), 56),
        name="ssd",
    )(z3, xbc3, xbc3, dt3, conv_w, conv_b, dt_bias_p, a_log_p, d_exp, norm_w, e_mat)


def kernel(x, positions, attn_w_qkv, attn_w_o, ssm_w_in, ssm_conv_w, ssm_conv_b, ssm_dt_bias, ssm_A_log, ssm_D, ssm_norm_w, ssm_w_out, router_w, router_bias, moe_w_gate, moe_w_up, moe_w_down, ln_mix_g, ln_mix_b, ln_ffn_g, ln_ffn_b):
    bsz, seq, d_model = x.shape
    t = bsz * seq
    depth = moe_w_gate.shape[0]
    alpha = (2 * depth) ** 0.25
    n_heads = d_model // HEAD_DIM
    patterns = ((128, 1), (512, 4), (2048, 16))
    attn_blk = 512

    half = ROT_DIM // 2
    inv_freq = ROPE_THETA ** (-jnp.arange(0, ROT_DIM, 2, dtype=F32) / ROT_DIM)
    ang = positions.astype(F32).reshape(t, 1) * inv_freq
    cos, sin = jnp.cos(ang), jnp.sin(ang)
    zeros_r = jnp.zeros((t, LANES - ROT_DIM), F32)
    cos_t = jnp.concatenate([cos, cos, jnp.ones((t, LANES - ROT_DIM), F32)], axis=1)
    sa_t = jnp.concatenate([-sin, jnp.zeros((t, half), F32), zeros_r], axis=1)
    sb_t = jnp.concatenate([jnp.zeros((t, half), F32), sin, zeros_r], axis=1)
    bias = _branch_count_bias(seq, attn_blk, patterns)

    bias_col = router_bias.reshape(N_EXPERTS, 1).astype(F32)

    h = x.reshape(t, d_model)
    hb = h.astype(BF16)
    for i in range(depth):
        j = i // 2
        if i % 2 == 0:
            qkv = _qkv_rope(h, attn_w_qkv[j].astype(BF16), cos_t, sa_t, sb_t, d_model)
            att = _attention(qkv.reshape(bsz, seq, 3 * d_model), bias, n_heads, attn_blk)
            mix, w_mix = att.reshape(t, d_model), attn_w_o[j]
        else:
            d_inner = ssm_norm_w.shape[1]
            n_ssm_heads = ssm_dt_bias.shape[1]
            cdim = ssm_conv_w.shape[2]
            w_in = ssm_w_in[j].astype(BF16)
            z = _mm(hb, w_in, 0, d_inner, BF16, tn=2048)
            xbc = _mm(hb, w_in, d_inner, cdim, BF16, tn=2048)
            w_dt = jnp.pad(w_in[:, d_inner + cdim:], ((0, 0), (0, LANES - n_ssm_heads)))
            dt = _mm(hb, w_dt, 0, LANES, F32)
            hpad = (0, LANES - n_ssm_heads)
            head_of_ch = jnp.arange(d_inner, dtype=jnp.int32) // SSM_HEAD_DIM
            e_mat = (jnp.arange(LANES, dtype=jnp.int32)[:, None] == head_of_ch[None, :]).astype(BF16)
            y = _ssd(z.reshape(bsz, seq, d_inner), xbc.reshape(bsz, seq, cdim), dt.reshape(bsz, seq, LANES),
                     ssm_conv_w[j], ssm_conv_b[j].reshape(1, cdim),
                     jnp.pad(ssm_dt_bias[j], hpad).reshape(1, LANES), jnp.pad(ssm_A_log[j], hpad).reshape(1, LANES),
                     jnp.repeat(ssm_D[j], SSM_HEAD_DIM).reshape(1, d_inner), ssm_norm_w[j].reshape(1, d_inner), e_mat)
            mix, w_mix = y.reshape(t, d_inner), ssm_w_out[j]
        h, hp, e_idx, gate = _mm_res_ln_route(mix, w_mix.astype(BF16), h, ln_mix_g[i], ln_mix_b[i],
                                              router_w, bias_col, alpha)
        next_is_ssm = i + 1 < depth and (i + 1) % 2 == 1
        h, hb = _moe_layer(h, hp, e_idx, gate, moe_w_gate, moe_w_up, moe_w_down, i,
                           ln_ffn_g[i], ln_ffn_b[i], alpha, next_is_ssm)
    return h.reshape(bsz, seq, d_model)
```
